```python
import math
import jax, jax.numpy as jnp
from jax import lax
import numpy as np

D_MODEL = 2048
BATCH = 4
SEQ = 8192
DEPTH = 4
DEC_BATCH = 1
DEC_SEQ = 16384
PAST_LEN = 128

N_MIXERS = 2
N_HYENA = (DEPTH + 1) // 2
N_ATTN = DEPTH // 2
HY_ORDER = 2
HY_SHORT = 3
HY_EMB = 33
HY_BANDS = (HY_EMB - 1) // 2
HY_FILTER_W = 64
HY_FAST_PCT = 0.3
HY_SLOW_PCT = 1.5
HY_TARGET = 1e-2
N_HEADS = 16
HEAD_DIM = 128
N_KV = 4
GQA_G = N_HEADS // N_KV
WINDOW = 128
BLOCK = 128
N_BUCKETS = 32
MAX_DIST = 128
D_FF = -(-(8 * D_MODEL) // (3 * 256)) * 256
EPS = 1e-6
NEG = -1e30

kernel_name = "hyena_swa_gqa_hybrid_encoder"


def rmsnorm(x, g):
    xf = x.astype(jnp.float32)
    y = xf * lax.rsqrt(jnp.mean(xf * xf, axis=-1, keepdims=True) + EPS)
    return (y * g.astype(jnp.float32)).astype(x.dtype)


def swiglu(h, w_gate_up, w_down):
    gu = h @ w_gate_up
    g, u = gu[..., :D_FF], gu[..., D_FF:]
    return (jax.nn.silu(g) * u) @ w_down


def hyena_filters(L, f_w1, f_b1, f_w2, f_b2, f_w3, f_b3, f_wout, f_freq):
    t = jnp.linspace(0.0, 1.0, L, dtype=jnp.float32)[:, None]
    w = 2.0 * math.pi * jnp.arange(L, dtype=jnp.float32) / L
    f = jnp.linspace(1e-4, HY_BANDS - 1, HY_BANDS, dtype=jnp.float32)
    ang = w[:, None] * f[None, :]
    feats = jnp.concatenate([t, jnp.cos(ang), -jnp.sin(ang)], axis=-1)
    fr = f_freq.astype(jnp.float32)
    a = jnp.sin(fr * (feats @ f_w1 + f_b1))
    a = jnp.sin(fr * (a @ f_w2 + f_b2))
    a = jnp.sin(fr * (a @ f_w3 + f_b3))
    h = (a @ f_wout).astype(jnp.float32).reshape(L, HY_ORDER, 2, D_MODEL)
    deltas = np.abs(np.linspace(math.log(HY_TARGET) / HY_SLOW_PCT,
                                math.log(HY_TARGET) / HY_FAST_PCT, D_MODEL)).astype(np.float32)
    decay = jnp.exp(-t * deltas[None, :])
    h = h * decay[:, None, None, :]
    hf = h[:, :, 0]
    hb = h[:, :, 1]
    k = jnp.concatenate([hf, jnp.zeros((1, HY_ORDER, D_MODEL), jnp.float32), hb[:0:-1]], axis=0)
    k = k / jnp.sum(jnp.abs(k), axis=0, keepdims=True)
    return jnp.fft.rfft(k, n=2 * L, axis=0)


def long_conv(z, kf):
    L = z.shape[1]
    Z = jnp.fft.rfft(z.astype(jnp.float32), n=2 * L, axis=1)
    y = jnp.fft.irfft(Z * kf[None], n=2 * L, axis=1)[:, :L]
    return y.astype(z.dtype)


def hyena_mixer(h, w_in, b_in, conv_w, conv_b, f_w1, f_b1, f_w2, f_b2, f_w3, f_b3,
                f_wout, f_freq, skip, w_out, b_out):
    B, L, _ = h.shape
    u = h @ w_in + b_in
    up = jnp.pad(u, ((0, 0), (1, 1), (0, 0)))
    uc = conv_w[0] * up[:, :-2] + conv_w[1] * up[:, 1:-1] + conv_w[2] * up[:, 2:] + conv_b
    v, x1, x2 = uc[..., :D_MODEL], uc[..., D_MODEL:2 * D_MODEL], uc[..., 2 * D_MODEL:]
    kf = hyena_filters(L, f_w1, f_b1, f_w2, f_b2, f_w3, f_b3, f_wout, f_freq)
    z = v
    for o, gate in enumerate((x1, x2)):
        z = gate * (long_conv(z, kf[:, o]) + skip[o] * z)
    return z @ w_out + b_out


def _band_structure():
    qi = np.arange(BLOCK)[:, None]
    ki = np.arange(3 * BLOCK)[None, :]
    rel = ki - BLOCK - qi
    nb = N_BUCKETS // 2
    max_exact = nb // 2
    n = np.abs(rel)
    large = max_exact + (np.log(np.maximum(n, 1) / max_exact) / math.log(MAX_DIST / max_exact)
                         * (nb - max_exact)).astype(np.int32)
    large = np.minimum(large, nb - 1)
    buckets = (rel > 0).astype(np.int32) * nb + np.where(n < max_exact, n, large).astype(np.int32)
    band = n <= WINDOW
    return buckets, band


def window_attention(h, w_qkv, q_g, k_g, sink, w_o, rel_bias):
    B, L, _ = h.shape
    nb = L // BLOCK
    qkv = h @ w_qkv
    nq, nk = N_HEADS * HEAD_DIM, N_KV * HEAD_DIM
    q = rmsnorm(qkv[..., :nq].reshape(B, L, N_HEADS, HEAD_DIM), q_g)
    k = rmsnorm(qkv[..., nq:nq + nk].reshape(B, L, N_KV, HEAD_DIM), k_g)
    v = qkv[..., nq + nk:].reshape(B, L, N_KV, HEAD_DIM)
    qb = q.reshape(B, nb, BLOCK, N_KV, GQA_G, HEAD_DIM)

    def windows(t):
        tp = jnp.pad(t, ((0, 0), (BLOCK, BLOCK), (0, 0), (0, 0))).reshape(B, nb + 2, BLOCK, N_KV, HEAD_DIM)
        return jnp.concatenate([tp[:, :-2], tp[:, 1:-1], tp[:, 2:]], axis=2)

    kw, vw = windows(k), windows(v)
    s = jnp.einsum("bnqhgd,bnkhd->bnhgqk", qb, kw).astype(jnp.float32) * (HEAD_DIM ** -0.5)
    buckets, band = _band_structure()
    bias = rel_bias[buckets].astype(jnp.float32)
    bias = jnp.transpose(bias, (2, 0, 1)).reshape(N_KV, GQA_G, BLOCK, 3 * BLOCK)
    kpos = np.arange(nb)[:, None] * BLOCK + np.arange(3 * BLOCK)[None, :] - BLOCK
    valid = band[None] & ((kpos >= 0) & (kpos < L))[:, None, :]
    s = jnp.where(valid[None, :, None, None], s + bias, NEG)
    sk = sink.astype(jnp.float32).reshape(1, 1, N_KV, GQA_G, 1, 1)
    m = jnp.maximum(jnp.max(s, axis=-1, keepdims=True), sk)
    p = jnp.exp(s - m)
    p = p / (jnp.sum(p, axis=-1, keepdims=True) + jnp.exp(sk - m))
    o = jnp.einsum("bnhgqk,bnkhd->bnqhgd", p.astype(vw.dtype), vw).reshape(B, L, nq)
    return o @ w_o


def trunk(x, norm_mix_g, norm_ffn_g, hy_w_in, hy_b_in, hy_conv_w, hy_conv_b,
          hy_f_w1, hy_f_b1, hy_f_w2, hy_f_b2, hy_f_w3, hy_f_b3, hy_f_wout, hy_f_freq,
          hy_skip, hy_w_out, hy_b_out, at_w_qkv, at_q_g, at_k_g, at_sink, at_w_o,
          rel_bias, ffn_w_gate_up, ffn_w_down):
    for i in range(DEPTH):
        j = i // N_MIXERS
        h = rmsnorm(x, norm_mix_g[i])
        if i % N_MIXERS == 0:
            y = hyena_mixer(h, hy_w_in[j], hy_b_in[j], hy_conv_w[j], hy_conv_b[j],
                            hy_f_w1[j], hy_f_b1[j], hy_f_w2[j], hy_f_b2[j], hy_f_w3[j], hy_f_b3[j],
                            hy_f_wout[j], hy_f_freq[j], hy_skip[j], hy_w_out[j], hy_b_out[j])
        else:
            y = window_attention(h, at_w_qkv[j], at_q_g[j], at_k_g[j], at_sink[j], at_w_o[j], rel_bias)
        x = x + y
        x = x + swiglu(rmsnorm(x, norm_ffn_g[i]), ffn_w_gate_up[i], ffn_w_down[i])
    return x


def setup_inputs(seed: int = 0) -> dict:
    key = jax.random.key(seed)
    ks = jax.random.split(key, 32)

    def nrm(k, shape, scale):
        return jax.random.normal(k, shape, jnp.float32) * scale

    D = D_MODEL
    qkv_w = (N_HEADS + 2 * N_KV) * HEAD_DIM
    return {
        "x_prompt": nrm(ks[0], (BATCH, SEQ, D), 1.0),
        "x_sample": nrm(ks[1], (DEC_BATCH, DEC_SEQ, D), 1.0),
        "norm_mix_g": 1.0 + nrm(ks[2], (DEPTH, D), 0.02),
        "norm_ffn_g": 1.0 + nrm(ks[3], (DEPTH, D), 0.02),
        "hy_w_in": nrm(ks[4], (N_HYENA, D, 3 * D), D ** -0.5),
        "hy_b_in": nrm(ks[5], (N_HYENA, 3 * D), 0.02),
        "hy_conv_w": nrm(ks[6], (N_HYENA, HY_SHORT, 3 * D), HY_SHORT ** -0.5),
        "hy_conv_b": nrm(ks[7], (N_HYENA, 3 * D), 0.02),
        "hy_f_w1": nrm(ks[8], (N_HYENA, HY_EMB, HY_FILTER_W), HY_EMB ** -0.5),
        "hy_f_b1": nrm(ks[9], (N_HYENA, HY_FILTER_W), 0.02),
        "hy_f_w2": nrm(ks[10], (N_HYENA, HY_FILTER_W, HY_FILTER_W), HY_FILTER_W ** -0.5),
        "hy_f_b2": nrm(ks[11], (N_HYENA, HY_FILTER_W), 0.02),
        "hy_f_w3": nrm(ks[12], (N_HYENA, HY_FILTER_W, HY_FILTER_W), HY_FILTER_W ** -0.5),
        "hy_f_b3": nrm(ks[13], (N_HYENA, HY_FILTER_W), 0.02),
        "hy_f_wout": nrm(ks[14], (N_HYENA, HY_FILTER_W, HY_ORDER * 2 * D), HY_FILTER_W ** -0.5),
        "hy_f_freq": 1.0 + nrm(ks[15], (N_HYENA, HY_FILTER_W), 0.1),
        "hy_skip": nrm(ks[16], (N_HYENA, HY_ORDER, D), 1.0),
        "hy_w_out": nrm(ks[17], (N_HYENA, D, D), D ** -0.5),
        "hy_b_out": nrm(ks[18], (N_HYENA, D), 0.02),
        "at_w_qkv": nrm(ks[19], (N_ATTN, D, qkv_w), D ** -0.5),
        "at_q_g": 1.0 + nrm(ks[20], (N_ATTN, HEAD_DIM), 0.02),
        "at_k_g": 1.0 + nrm(ks[21], (N_ATTN, HEAD_DIM), 0.02),
        "at_sink": nrm(ks[22], (N_ATTN, N_HEADS), 1.0),
        "at_w_o": nrm(ks[23], (N_ATTN, N_HEADS * HEAD_DIM, D), (N_HEADS * HEAD_DIM) ** -0.5),
        "rel_bias": nrm(ks[24], (N_BUCKETS, N_HEADS), 0.1),
        "ffn_w_gate_up": nrm(ks[25], (DEPTH, D, 2 * D_FF), D ** -0.5),
        "ffn_w_down": nrm(ks[26], (DEPTH, D_FF, D), D_FF ** -0.5),
    }


def reference(x_prompt, x_sample, norm_mix_g, norm_ffn_g, hy_w_in, hy_b_in, hy_conv_w, hy_conv_b,
              hy_f_w1, hy_f_b1, hy_f_w2, hy_f_b2, hy_f_w3, hy_f_b3, hy_f_wout, hy_f_freq,
              hy_skip, hy_w_out, hy_b_out, at_w_qkv, at_q_g, at_k_g, at_sink, at_w_o,
              rel_bias, ffn_w_gate_up, ffn_w_down):
    y_prompt = trunk(x_prompt, norm_mix_g, norm_ffn_g, hy_w_in, hy_b_in, hy_conv_w, hy_conv_b,
                     hy_f_w1, hy_f_b1, hy_f_w2, hy_f_b2, hy_f_w3, hy_f_b3, hy_f_wout, hy_f_freq,
                     hy_skip, hy_w_out, hy_b_out, at_w_qkv, at_q_g, at_k_g, at_sink, at_w_o,
                     rel_bias, ffn_w_gate_up, ffn_w_down)
    y_sample = trunk(x_sample, norm_mix_g, norm_ffn_g, hy_w_in, hy_b_in, hy_conv_w, hy_conv_b,
                     hy_f_w1, hy_f_b1, hy_f_w2, hy_f_b2, hy_f_w3, hy_f_b3, hy_f_wout, hy_f_freq,
                     hy_skip, hy_w_out, hy_b_out, at_w_qkv, at_q_g, at_k_g, at_sink, at_w_o,
                     rel_bias, ffn_w_gate_up, ffn_w_down)
    return (y_prompt, y_sample)
```

```python
import functools
import math

import jax
import jax.numpy as jnp
import numpy as np
from jax import lax
from jax.experimental import pallas as pl
from jax.experimental.pallas import tpu as pltpu

F32 = jnp.float32
BF16 = jnp.bfloat16

D_MODEL = 2048
DEPTH = 4
N_MIXERS = 2
HY_ORDER = 2
HY_EMB = 33
HY_BANDS = (HY_EMB - 1) // 2
HY_FILTER_W = 64
HY_FAST_PCT = 0.3
HY_SLOW_PCT = 1.5
HY_TARGET = 1e-2
N_HEADS = 16
HEAD_DIM = 128
N_KV = 4
GQA_G = N_HEADS // N_KV
WINDOW = 128
BLOCK = 128
N_BUCKETS = 32
MAX_DIST = 128
D_FF = -(-(8 * D_MODEL) // (3 * 256)) * 256
EPS = 1e-6
NEG = -1e30

V7X_LANES = 128
V7X_SUBLANES = 8
VMEM_LIMIT = 56 * 1024 * 1024

FFT_P = V7X_LANES
CH_GROUP = V7X_SUBLANES


def _cparams(*sem):
    return pltpu.CompilerParams(dimension_semantics=sem, vmem_limit_bytes=VMEM_LIMIT)


def _rms_bf16(x, g):
    ms = jnp.mean(x * x, axis=-1, keepdims=True)
    return (x * lax.rsqrt(ms + EPS) * g).astype(BF16)


def _norm_mm_body(x_ref, g_ref, w_ref, o_ref, hn_ref):
    @pl.when(pl.program_id(1) == 0)
    def _():
        hn_ref[...] = _rms_bf16(x_ref[...], g_ref[...])

    o_ref[...] = jnp.dot(hn_ref[...], w_ref[...], preferred_element_type=F32).astype(o_ref.dtype)


def norm_matmul(x, g, w, *, bm, bn, out_dtype=F32):
    T, K = x.shape
    N = w.shape[1]
    return pl.pallas_call(
        _norm_mm_body,
        grid=(T // bm, N // bn),
        in_specs=[
            pl.BlockSpec((bm, K), lambda i, j: (i, 0)),
            pl.BlockSpec((1, K), lambda i, j: (0, 0)),
            pl.BlockSpec((K, bn), lambda i, j: (0, j)),
        ],
        out_specs=pl.BlockSpec((bm, bn), lambda i, j: (i, j)),
        out_shape=jax.ShapeDtypeStruct((T, N), out_dtype),
        scratch_shapes=[pltpu.VMEM((bm, K), BF16)],
        compiler_params=_cparams("parallel", "arbitrary"),
        name="norm_matmul",
    )(x, g.reshape(1, K), w)


def _norm_mm_t_body(x_ref, g_ref, w_ref, b_ref, o_ref, hn_ref):
    @pl.when(pl.program_id(2) == 0)
    def _():
        hn_ref[...] = _rms_bf16(x_ref[0], g_ref[...])

    acc = lax.dot_general(w_ref[...], hn_ref[...], (((1,), (1,)), ((), ())), preferred_element_type=F32)
    o_ref[0] = acc + b_ref[...]


def norm_matmul_t(x, g, w_t, bias, *, bt, bc):
    B, L, K = x.shape
    C = w_t.shape[0]
    return pl.pallas_call(
        _norm_mm_t_body,
        grid=(B, L // bt, C // bc),
        in_specs=[
            pl.BlockSpec((1, bt, K), lambda b, t, c: (b, t, 0)),
            pl.BlockSpec((1, K), lambda b, t, c: (0, 0)),
            pl.BlockSpec((bc, K), lambda b, t, c: (c, 0)),
            pl.BlockSpec((bc, 1), lambda b, t, c: (c, 0)),
        ],
        out_specs=pl.BlockSpec((1, bc, bt), lambda b, t, c: (b, c, t)),
        out_shape=jax.ShapeDtypeStruct((B, C, L), F32),
        scratch_shapes=[pltpu.VMEM((bt, K), BF16)],
        compiler_params=_cparams("parallel", "parallel", "arbitrary"),
        name="norm_matmul_t",
    )(x, g.reshape(1, K), w_t, bias.reshape(C, 1))


def _res_mm_body(a_ref, w_ref, x_ref, o_ref):
    o_ref[...] = x_ref[...] + jnp.dot(a_ref[...], w_ref[...], preferred_element_type=F32)


def residual_matmul(x, a, w, *, bm, bn):
    T, K = a.shape
    N = w.shape[1]
    return pl.pallas_call(
        _res_mm_body,
        grid=(T // bm, N // bn),
        in_specs=[
            pl.BlockSpec((bm, K), lambda i, j: (i, 0)),
            pl.BlockSpec((K, bn), lambda i, j: (0, j)),
            pl.BlockSpec((bm, bn), lambda i, j: (i, j)),
        ],
        out_specs=pl.BlockSpec((bm, bn), lambda i, j: (i, j)),
        out_shape=jax.ShapeDtypeStruct((T, N), F32),
        compiler_params=_cparams("parallel", "parallel"),
        name="residual_matmul",
    )(a, w, x)


def _res_mm_t_body(z_ref, w_ref, b_ref, x_ref, o_ref):
    z = z_ref[0].astype(BF16)
    y = lax.dot_general(z, w_ref[...], (((0,), (0,)), ((), ())), preferred_element_type=F32)
    o_ref[0] = x_ref[0] + y + b_ref[...]


def residual_matmul_t(x, z_t, w, bias, *, bt):
    B, K, L = z_t.shape
    N = w.shape[1]
    return pl.pallas_call(
        _res_mm_t_body,
        grid=(B, L // bt),
        in_specs=[
            pl.BlockSpec((1, K, bt), lambda b, t: (b, 0, t)),
            pl.BlockSpec((K, N), lambda b, t: (0, 0)),
            pl.BlockSpec((1, N), lambda b, t: (0, 0)),
            pl.BlockSpec((1, bt, N), lambda b, t: (b, t, 0)),
        ],
        out_specs=pl.BlockSpec((1, bt, N), lambda b, t: (b, t, 0)),
        out_shape=jax.ShapeDtypeStruct((B, L, N), F32),
        compiler_params=_cparams("parallel", "parallel"),
        name="residual_matmul_t",
    )(z_t, w, bias.reshape(1, N), x)


def _ffn_body(x_ref, g_ref, wg_ref, wu_ref, wd_ref, o_ref, hn_ref):
    @pl.when(pl.program_id(1) == 0)
    def _():
        x = x_ref[...]
        hn_ref[...] = _rms_bf16(x, g_ref[...])
        o_ref[...] = x

    h = hn_ref[...]
    gate = jnp.dot(h, wg_ref[...], preferred_element_type=F32)
    up = jnp.dot(h, wu_ref[...], preferred_element_type=F32)
    act = (gate * jax.nn.sigmoid(gate) * up).astype(BF16)
    o_ref[...] += jnp.dot(act, wd_ref[...], preferred_element_type=F32)


def ffn_block(x, g, w_gate_up, w_down, *, bm, bf):
    T, K = x.shape
    nf = D_FF // bf
    return pl.pallas_call(
        _ffn_body,
        grid=(T // bm, nf),
        in_specs=[
            pl.BlockSpec((bm, K), lambda i, f: (i, 0)),
            pl.BlockSpec((1, K), lambda i, f: (0, 0)),
            pl.BlockSpec((K, bf), lambda i, f: (0, f)),
            pl.BlockSpec((K, bf), lambda i, f: (0, f + nf)),
            pl.BlockSpec((bf, K), lambda i, f: (f, 0)),
        ],
        out_specs=pl.BlockSpec((bm, K), lambda i, f: (i, 0)),
        out_shape=jax.ShapeDtypeStruct((T, K), F32),
        scratch_shapes=[pltpu.VMEM((bm, K), BF16)],
        compiler_params=_cparams("parallel", "arbitrary"),
        name="ffn_block",
    )(x, g.reshape(1, K), w_gate_up, w_gate_up, w_down)


ATT_TQ = 4 * BLOCK
KV_W = N_KV * HEAD_DIM


def _band_structure():
    qi = np.arange(BLOCK)[:, None]
    ki = np.arange(3 * BLOCK)[None, :]
    rel = ki - BLOCK - qi
    nb = N_BUCKETS // 2
    max_exact = nb // 2
    n = np.abs(rel)
    large = max_exact + (np.log(np.maximum(n, 1) / max_exact) / math.log(MAX_DIST / max_exact)
                         * (nb - max_exact)).astype(np.int32)
    large = np.minimum(large, nb - 1)
    buckets = (rel > 0).astype(np.int32) * nb + np.where(n < max_exact, n, large).astype(np.int32)
    band = n <= WINDOW
    return buckets, band


def _attn_body(q_ref, kp_ref, kc_ref, kn_ref, vp_ref, vc_ref, vn_ref, bias_ref, sink_ref,
               qg_ref, kg_ref, o_ref, kbuf, vbuf):
    i = pl.program_id(1)
    last = pl.num_programs(1) - 1
    qg = qg_ref[...]
    kg = kg_ref[...]

    def head_norm(t, gain):
        ms = jnp.mean(t * t, axis=-1, keepdims=True)
        return (t * lax.rsqrt(ms + EPS) * gain).astype(BF16)

    def k_norm(k):
        return jnp.concatenate(
            [head_norm(k[:, h * HEAD_DIM:(h + 1) * HEAD_DIM], kg) for h in range(N_KV)], axis=1)

    kbuf[0:BLOCK] = k_norm(kp_ref[0])
    kbuf[BLOCK:BLOCK + ATT_TQ] = k_norm(kc_ref[0])
    kbuf[BLOCK + ATT_TQ:] = k_norm(kn_ref[0])
    vbuf[0:BLOCK] = vp_ref[0].astype(BF16)
    vbuf[BLOCK:BLOCK + ATT_TQ] = vc_ref[0].astype(BF16)
    vbuf[BLOCK + ATT_TQ:] = vn_ref[0].astype(BF16)

    lane = lax.broadcasted_iota(jnp.int32, (1, 3 * BLOCK), 1)
    first_edge = jnp.where((lane < BLOCK) & (i == 0), NEG, 0.0).astype(F32)
    last_edge = jnp.where((lane >= 2 * BLOCK) & (i == last), NEG, 0.0).astype(F32)
    scale = HEAD_DIM ** -0.5

    for j in range(ATT_TQ // BLOCK):
        r0 = j * BLOCK
        for g in range(N_KV):
            qs = jnp.concatenate(
                [head_norm(q_ref[0, r0:r0 + BLOCK, (GQA_G * g + h) * HEAD_DIM:(GQA_G * g + h + 1) * HEAD_DIM], qg)
                 for h in range(GQA_G)], axis=0)
            kw = kbuf[r0:r0 + 3 * BLOCK, g * HEAD_DIM:(g + 1) * HEAD_DIM]
            vw = vbuf[r0:r0 + 3 * BLOCK, g * HEAD_DIM:(g + 1) * HEAD_DIM]
            s = lax.dot_general(qs, kw, (((1,), (1,)), ((), ())), preferred_element_type=F32)
            s = s * scale + bias_ref[g]
            if j == 0:
                s = s + first_edge
            if j == ATT_TQ // BLOCK - 1:
                s = s + last_edge
            sk = sink_ref[g][:, 0:1]
            m = jnp.maximum(jnp.max(s, axis=-1, keepdims=True), sk)
            p = jnp.exp(s - m)
            denom = jnp.sum(p, axis=-1, keepdims=True) + jnp.exp(sk - m)
            o = jnp.dot(p.astype(BF16), vw, preferred_element_type=F32) / denom
            for h in range(GQA_G):
                c0 = (GQA_G * g + h) * HEAD_DIM
                o_ref[0, r0:r0 + BLOCK, c0:c0 + HEAD_DIM] = o[h * BLOCK:(h + 1) * BLOCK].astype(o_ref.dtype)


def window_attention(qkv, q_g, k_g, sink, rel_bias):
    B, L, _ = qkv.shape
    nq = N_HEADS * HEAD_DIM
    sub = ATT_TQ // BLOCK
    nblk = L // BLOCK
    kcol = nq // KV_W
    vcol = kcol + 1

    buckets, band = _band_structure()
    bias = rel_bias[buckets].astype(F32)
    bias = jnp.where(band[:, :, None], bias, NEG)
    bias = jnp.transpose(bias, (2, 0, 1)).reshape(N_KV, GQA_G * BLOCK, 3 * BLOCK)
    sink_rows = jnp.broadcast_to(sink.astype(F32).reshape(N_KV, GQA_G, 1, 1),
                                 (N_KV, GQA_G, BLOCK, V7X_LANES)).reshape(N_KV, GQA_G * BLOCK, V7X_LANES)

    return pl.pallas_call(
        _attn_body,
        grid=(B, L // ATT_TQ),
        in_specs=[
            pl.BlockSpec((1, ATT_TQ, nq), lambda b, i: (b, i, 0)),
            pl.BlockSpec((1, BLOCK, KV_W), lambda b, i: (b, jnp.maximum(sub * i - 1, 0), kcol)),
            pl.BlockSpec((1, ATT_TQ, KV_W), lambda b, i: (b, i, kcol)),
            pl.BlockSpec((1, BLOCK, KV_W), lambda b, i: (b, jnp.minimum(sub * i + sub, nblk - 1), kcol)),
            pl.BlockSpec((1, BLOCK, KV_W), lambda b, i: (b, jnp.maximum(sub * i - 1, 0), vcol)),
            pl.BlockSpec((1, ATT_TQ, KV_W), lambda b, i: (b, i, vcol)),
            pl.BlockSpec((1, BLOCK, KV_W), lambda b, i: (b, jnp.minimum(sub * i + sub, nblk - 1), vcol)),
            pl.BlockSpec((N_KV, GQA_G * BLOCK, 3 * BLOCK), lambda b, i: (0, 0, 0)),
            pl.BlockSpec((N_KV, GQA_G * BLOCK, V7X_LANES), lambda b, i: (0, 0, 0)),
            pl.BlockSpec((1, HEAD_DIM), lambda b, i: (0, 0)),
            pl.BlockSpec((1, HEAD_DIM), lambda b, i: (0, 0)),
        ],
        out_specs=pl.BlockSpec((1, ATT_TQ, nq), lambda b, i: (b, i, 0)),
        out_shape=jax.ShapeDtypeStruct((B, L, nq), BF16),
        scratch_shapes=[pltpu.VMEM((ATT_TQ + 2 * BLOCK, KV_W), BF16),
                        pltpu.VMEM((ATT_TQ + 2 * BLOCK, KV_W), BF16)],
        compiler_params=_cparams("parallel", "parallel"),
        name="window_attention",
    )(qkv, qkv, qkv, qkv, qkv, qkv, qkv, bias, sink_rows,
      q_g.astype(F32).reshape(1, HEAD_DIM), k_g.astype(F32).reshape(1, HEAD_DIM))


FEAT_PAD = V7X_LANES


def _filter_mlp_body(feat_ref, w1_ref, b1_ref, w2_ref, b2_ref, w3_ref, b3_ref, fr_ref,
                     wo_ref, delta_ref, t_ref, o_ref, a_ref):
    @pl.when(pl.program_id(1) == 0)
    def _():
        fr = fr_ref[...]
        a = jnp.sin(fr * (jnp.dot(feat_ref[...], w1_ref[...], preferred_element_type=F32) + b1_ref[...]))
        a = jnp.sin(fr * (jnp.dot(a.astype(BF16), w2_ref[...], preferred_element_type=F32) + b2_ref[...]))
        a = jnp.sin(fr * (jnp.dot(a.astype(BF16), w3_ref[...], preferred_element_type=F32) + b3_ref[...]))
        a_ref[...] = a.astype(BF16)

    h = lax.dot_general(wo_ref[...], a_ref[...], (((1,), (1,)), ((), ())), preferred_element_type=F32)
    o_ref[...] = h * jnp.exp(-(delta_ref[...] * t_ref[...]))


def hyena_filter_taps(L, f_w1, f_b1, f_w2, f_b2, f_w3, f_b3, f_wout, f_freq, *, bt=1024, bc=1024):
    t = jnp.linspace(0.0, 1.0, L, dtype=F32)[:, None]
    w = 2.0 * math.pi * jnp.arange(L, dtype=F32) / L
    f = jnp.linspace(1e-4, HY_BANDS - 1, HY_BANDS, dtype=F32)
    ang = w[:, None] * f[None, :]
    feats = jnp.concatenate([t, jnp.cos(ang), -jnp.sin(ang), jnp.zeros((L, FEAT_PAD - HY_EMB), F32)], axis=-1)
    deltas = np.abs(np.linspace(math.log(HY_TARGET) / HY_SLOW_PCT,
                                math.log(HY_TARGET) / HY_FAST_PCT, D_MODEL)).astype(np.float32)
    C = HY_ORDER * 2 * D_MODEL
    bc = min(bc, C)
    delta_col = np.tile(deltas, HY_ORDER * 2).reshape(C, 1)
    w1 = jnp.concatenate([f_w1.astype(F32), jnp.zeros((FEAT_PAD - HY_EMB, HY_FILTER_W), F32)], axis=0)
    W = HY_FILTER_W
    return pl.pallas_call(
        _filter_mlp_body,
        grid=(L // bt, C // bc),
        in_specs=[
            pl.BlockSpec((bt, FEAT_PAD), lambda i, c: (i, 0)),
            pl.BlockSpec((FEAT_PAD, W), lambda i, c: (0, 0)),
            pl.BlockSpec((1, W), lambda i, c: (0, 0)),
            pl.BlockSpec((W, W), lambda i, c: (0, 0)),
            pl.BlockSpec((1, W), lambda i, c: (0, 0)),
            pl.BlockSpec((W, W), lambda i, c: (0, 0)),
            pl.BlockSpec((1, W), lambda i, c: (0, 0)),
            pl.BlockSpec((1, W), lambda i, c: (0, 0)),
            pl.BlockSpec((bc, W), lambda i, c: (c, 0)),
            pl.BlockSpec((bc, 1), lambda i, c: (c, 0)),
            pl.BlockSpec((1, bt), lambda i, c: (0, i)),
        ],
        out_specs=pl.BlockSpec((bc, bt), lambda i, c: (c, i)),
        out_shape=jax.ShapeDtypeStruct((C, L), F32),
        scratch_shapes=[pltpu.VMEM((bt, W), BF16)],
        compiler_params=_cparams("parallel", "arbitrary"),
        name="hyena_filter_taps",
    )(feats.astype(BF16), w1.astype(BF16), f_b1.reshape(1, W), f_w2.astype(BF16), f_b2.reshape(1, W),
      f_w3.astype(BF16), f_b3.reshape(1, W), f_freq.astype(F32).reshape(1, W),
      f_wout.T.astype(BF16), jnp.asarray(delta_col), t.reshape(1, L))


def _dft_tables(L, paired):
    N = 2 * L
    P = FFT_P
    Q = N // P
    S = L // P
    b = np.arange(Q, dtype=np.float64)
    fq = np.exp(-2j * np.pi * np.outer(b, np.arange(S)) / Q)
    fq_full = np.exp(-2j * np.pi * np.outer(b, np.arange(Q)) / Q)
    fp = np.exp(-2j * np.pi * np.outer(np.arange(P), np.arange(P)) / P)
    tw = np.exp(-2j * np.pi * np.outer(b, np.arange(P)) / N)
    ci = np.conj(fq).T / N

    def stack(c):
        return np.block([[c.real, -c.imag], [c.imag, c.real]])

    if paired:
        g1 = stack(fq)
        g4 = stack(ci)
    else:
        g1 = np.concatenate([fq.real, fq.imag], axis=0)
        g4 = np.concatenate([ci.real, -ci.imag], axis=1)
    g1_full = np.concatenate([fq_full.real, fq_full.imag], axis=0)
    g2 = np.block([[fp.real, fp.imag], [-fp.imag, fp.real]])
    g2c = np.block([[fp.real, -fp.imag], [fp.imag, fp.real]])
    twr = np.tile(tw.real, (1, 2))
    twi = np.tile(tw.imag, (1, 2))
    f32 = lambda a: np.ascontiguousarray(a, dtype=np.float32)
    return dict(g1=f32(g1), g4=f32(g4), g1_full=f32(g1_full), g2=f32(g2), g2c=f32(g2c), twr=f32(twr), twi=f32(twi),
                Q=Q, S=S, N=N)


def _to_tiles(x, n_tiles):
    chunks = [x[:, s * FFT_P:(s + 1) * FFT_P] for s in range(n_tiles)]
    return jnp.swapaxes(jnp.stack(chunks, axis=0), 0, 1)


def _from_tiles(x):
    y = jnp.swapaxes(x, 0, 1)
    return jnp.concatenate([y[s] for s in range(y.shape[0])], axis=1)


def _fwd_fft(re_tiles, im_tiles, g1, twr, twi, g2):
    G = len(re_tiles)
    Q = twr.shape[0]
    P = FFT_P
    rows = []
    for c in range(0, G, 2):
        top = jnp.concatenate([re_tiles[c], re_tiles[c + 1]], axis=1)
        if im_tiles is None:
            rhs = top
        else:
            rhs = jnp.concatenate([top, jnp.concatenate([im_tiles[c], im_tiles[c + 1]], axis=1)], axis=0)
        y = jnp.dot(g1, rhs.astype(BF16), preferred_element_type=F32)
        yr, yi = y[:Q], y[Q:]
        zr = yr * twr - yi * twi
        zi = yr * twi + yi * twr
        rows.append(jnp.concatenate([zr[:, :P], zi[:, :P]], axis=1))
        rows.append(jnp.concatenate([zr[:, P:], zi[:, P:]], axis=1))
    lhs = jnp.concatenate(rows, axis=0).astype(BF16)
    return jnp.dot(lhs, g2, preferred_element_type=F32)


def _inv_fft(spec, g2c, twr, twi, g4, want_imag):
    Q = twr.shape[0]
    P = FFT_P
    G = spec.shape[0] // Q
    S = g4.shape[0] // 2 if want_imag else g4.shape[0]
    y = jnp.dot(spec.astype(BF16), g2c, preferred_element_type=F32)
    out_re, out_im = [], []
    for c in range(0, G, 2):
        ya = y[c * Q:(c + 1) * Q]
        yb = y[(c + 1) * Q:(c + 2) * Q]
        yr = jnp.concatenate([ya[:, :P], yb[:, :P]], axis=1)
        yi = jnp.concatenate([ya[:, P:], yb[:, P:]], axis=1)
        zr = yr * twr + yi * twi
        zi = yi * twr - yr * twi
        rhs = jnp.concatenate([zr, zi], axis=0).astype(BF16)
        o = jnp.dot(g4, rhs, preferred_element_type=F32)
        out_re += [o[:S, :P], o[:S, P:]]
        if want_imag:
            out_im += [o[S:, :P], o[S:, P:]]
    return out_re, out_im


def _filter_fft_body(k_ref, g1_ref, twr_ref, twi_ref, g2_ref, o_ref, *, Q):
    g1, twr, twi, g2 = g1_ref[...], twr_ref[...], twi_ref[...], g2_ref[...]
    n_groups = k_ref.shape[0] // CH_GROUP

    def group(gi, carry):
        c0 = pl.multiple_of(gi * CH_GROUP, CH_GROUP)
        k = k_ref[pl.ds(c0, CH_GROUP), :]
        norm = jnp.sum(jnp.abs(k), axis=-1, keepdims=True)
        tiles = _to_tiles(k / norm, Q)
        spec = _fwd_fft([tiles[c] for c in range(CH_GROUP)], None, g1, twr, twi, g2)
        o_ref[pl.ds(c0, CH_GROUP)] = spec.reshape(CH_GROUP, Q, 2 * FFT_P)
        return carry

    lax.fori_loop(0, n_groups, group, 0)


def filter_spectrum(k_time, tabs, *, cb):
    C, N = k_time.shape
    Q = tabs["Q"]
    P2 = 2 * FFT_P
    g1 = jnp.asarray(tabs["g1_full"]).astype(BF16)
    g2 = jnp.asarray(tabs["g2"]).astype(BF16)
    const = lambda a: pl.BlockSpec(a.shape, lambda i: (0,) * a.ndim)
    twr, twi = jnp.asarray(tabs["twr"]), jnp.asarray(tabs["twi"])
    return pl.pallas_call(
        functools.partial(_filter_fft_body, Q=Q),
        grid=(C // cb,),
        in_specs=[pl.BlockSpec((cb, N), lambda i: (i, 0)), const(g1), const(twr), const(twi), const(g2)],
        out_specs=pl.BlockSpec((cb, Q, P2), lambda i: (i, 0, 0)),
        out_shape=jax.ShapeDtypeStruct((C, Q, P2), F32),
        compiler_params=_cparams("parallel"),
        name="filter_spectrum",
    )(k_time, g1, twr, twi, g2)


def _hyena_body(v_ref, x1_ref, x2_ref, kf_ref, par_ref, g1_ref, twr_ref, twi_ref, g2_ref, g2c_ref, g4_ref,
                o_ref, *, S, Q, paired):
    g1, twr, twi = g1_ref[...], twr_ref[...], twi_ref[...]
    g2, g2c, g4 = g2_ref[...], g2c_ref[...], g4_ref[...]
    nb = 2 if paired else 1
    n_groups = v_ref.shape[1] // CH_GROUP
    P = FFT_P
    G = CH_GROUP

    lane = lax.broadcasted_iota(jnp.int32, (G, S, P), 2)
    row = lax.broadcasted_iota(jnp.int32, (G, S, P), 1)

    def short_conv(u, w0, w1, w2, bias):
        lr = pltpu.roll(u, 1, axis=2)
        prev = jnp.where(lane == 0, pltpu.roll(lr, 1, axis=1), lr)
        prev = jnp.where((lane == 0) & (row == 0), 0.0, prev)
        ll = pltpu.roll(u, P - 1, axis=2)
        nxt = jnp.where(lane == P - 1, pltpu.roll(ll, S - 1, axis=1), ll)
        nxt = jnp.where((lane == P - 1) & (row == S - 1), 0.0, nxt)
        return w0 * prev + w1 * u + w2 * nxt + bias

    def group(gi, carry):
        c0 = pl.multiple_of(gi * G, G)
        pw = lambda idx: par_ref[idx, pl.ds(c0, G)]

        def load(ref, b, base):
            u = _to_tiles(ref[b, pl.ds(c0, G), :], S)
            return short_conv(u, pw(base), pw(base + 1), pw(base + 2), pw(base + 3))

        z = [load(v_ref, b, 0) for b in range(nb)]
        gates = [[load(x1_ref, b, 4) for b in range(nb)], [load(x2_ref, b, 8) for b in range(nb)]]
        for o in range(HY_ORDER):
            re = [z[0][c] for c in range(G)]
            im = [z[1][c] for c in range(G)] if paired else None
            spec = _fwd_fft(re, im, g1, twr, twi, g2)
            kf = kf_ref[o, pl.ds(c0, G)].reshape(G * Q, 2 * P)
            xr, xi = spec[:, :P], spec[:, P:]
            kr, ki = kf[:, :P], kf[:, P:]
            prod = jnp.concatenate([xr * kr - xi * ki, xr * ki + xi * kr], axis=1)
            out_re, out_im = _inv_fft(prod, g2c, twr, twi, g4, paired)
            skip = pw(12 + o)
            conv = [jnp.stack(out_re, axis=0)] + ([jnp.stack(out_im, axis=0)] if paired else [])
            z = [gates[o][b] * (conv[b] + skip * z[b]) for b in range(nb)]
        for b in range(nb):
            o_ref[b, pl.ds(c0, G), :] = _from_tiles(z[b])
        return carry

    lax.fori_loop(0, n_groups, group, 0)


def hyena_operator(u_t, kf, conv_w, conv_b, skip, tabs, *, cb):
    B, C3, L = u_t.shape
    D = C3 // 3
    Q, S = tabs["Q"], tabs["S"]
    P = FFT_P
    paired = B % 2 == 0
    nb = 2 if paired else 1
    cols = []
    for part in range(3):
        sl = slice(part * D, (part + 1) * D)
        cols += [conv_w[0, sl], conv_w[1, sl], conv_w[2, sl], conv_b[sl]]
    cols += [skip[0], skip[1]]
    n_par = len(cols)
    par = jnp.broadcast_to(jnp.stack(cols, axis=0).astype(F32)[:, :, None, None], (n_par, D, 1, P))

    g1 = jnp.asarray(tabs["g1"]).astype(BF16)
    g2 = jnp.asarray(tabs["g2"]).astype(BF16)
    g2c = jnp.asarray(tabs["g2c"]).astype(BF16)
    g4 = jnp.asarray(tabs["g4"]).astype(BF16)
    twr, twi = jnp.asarray(tabs["twr"]), jnp.asarray(tabs["twi"])
    const = lambda a: pl.BlockSpec(a.shape, lambda c, p: (0,) * a.ndim)
    ncb = D // cb
    return pl.pallas_call(
        functools.partial(_hyena_body, S=S, Q=Q, paired=paired),
        grid=(ncb, B // nb),
        in_specs=[
            pl.BlockSpec((nb, cb, L), lambda c, p: (p, c, 0)),
            pl.BlockSpec((nb, cb, L), lambda c, p: (p, c + ncb, 0)),
            pl.BlockSpec((nb, cb, L), lambda c, p: (p, c + 2 * ncb, 0)),
            pl.BlockSpec((HY_ORDER, cb, Q, 2 * P), lambda c, p: (0, c, 0, 0)),
            pl.BlockSpec((n_par, cb, 1, P), lambda c, p: (0, c, 0, 0)),
            const(g1), const(twr), const(twi), const(g2), const(g2c), const(g4),
        ],
        out_specs=pl.BlockSpec((nb, cb, L), lambda c, p: (p, c, 0)),
        out_shape=jax.ShapeDtypeStruct((B, D, L), F32),
        compiler_params=_cparams("parallel", "arbitrary"),
        name="hyena_operator",
    )(u_t, u_t, u_t, kf, par, g1, twr, twi, g2, g2c, g4)


def _hyena_filters_spectrum(L, f_w1, f_b1, f_w2, f_b2, f_w3, f_b3, f_wout, f_freq, tabs, cb):
    h_t = hyena_filter_taps(L, f_w1, f_b1, f_w2, f_b2, f_w3, f_b3, f_wout, f_freq)
    h_t = h_t.reshape(HY_ORDER, 2, D_MODEL, L)
    hf, hb = h_t[:, 0], h_t[:, 1]
    k_time = jnp.concatenate([hf, jnp.zeros((HY_ORDER, D_MODEL, 1), F32), jnp.flip(hb[..., 1:], axis=-1)], axis=-1)
    kf = filter_spectrum(k_time.reshape(HY_ORDER * D_MODEL, 2 * L), tabs, cb=cb)
    return kf.reshape(HY_ORDER, D_MODEL, tabs["Q"], 2 * FFT_P)


def _trunk(x, p, cfg):
    B, L, D = x.shape
    tabs = _dft_tables(L, paired=(B % 2 == 0))
    for i in range(DEPTH):
        j = i // N_MIXERS
        if i % N_MIXERS == 0:
            kf = _hyena_filters_spectrum(L, p["hy_f_w1"][j], p["hy_f_b1"][j], p["hy_f_w2"][j], p["hy_f_b2"][j],
                                         p["hy_f_w3"][j], p["hy_f_b3"][j], p["hy_f_wout"][j], p["hy_f_freq"][j],
                                         tabs, cfg["filt_cb"])
            u_t = norm_matmul_t(x, p["norm_mix_g"][i], p["hy_w_in_t"][j], p["hy_b_in"][j], bt=1024, bc=1024)
            z_t = hyena_operator(u_t, kf, p["hy_conv_w"][j], p["hy_conv_b"][j], p["hy_skip"][j], tabs, cb=cfg["hy_cb"])
            x = residual_matmul_t(x, z_t, p["hy_w_out"][j], p["hy_b_out"][j], bt=512)
        else:
            x2 = x.reshape(B * L, D)
            qkv = norm_matmul(x2, p["norm_mix_g"][i], p["at_w_qkv"][j], bm=1024, bn=1024)
            att = window_attention(qkv.reshape(B, L, -1), p["at_q_g"][j], p["at_k_g"][j], p["at_sink"][j],
                                   p["rel_bias"])
            x = residual_matmul(x2, att.reshape(B * L, -1), p["at_w_o"][j], bm=1024, bn=1024).reshape(B, L, D)
        x = ffn_block(x.reshape(B * L, D), p["norm_ffn_g"][i], p["ffn_w_gate_up"][i], p["ffn_w_down"][i],
                      bm=512, bf=512).reshape(B, L, D)
    return x


def kernel(x_prompt, x_sample, norm_mix_g, norm_ffn_g, hy_w_in, hy_b_in, hy_conv_w, hy_conv_b, hy_f_w1, hy_f_b1,
           hy_f_w2, hy_f_b2, hy_f_w3, hy_f_b3, hy_f_wout, hy_f_freq, hy_skip, hy_w_out, hy_b_out, at_w_qkv, at_q_g,
           at_k_g, at_sink, at_w_o, rel_bias, ffn_w_gate_up, ffn_w_down):
    p = dict(
        norm_mix_g=norm_mix_g.astype(F32), norm_ffn_g=norm_ffn_g.astype(F32),
        hy_w_in_t=jnp.swapaxes(hy_w_in, 1, 2).astype(BF16), hy_b_in=hy_b_in,
        hy_conv_w=hy_conv_w, hy_conv_b=hy_conv_b,
        hy_f_w1=hy_f_w1, hy_f_b1=hy_f_b1, hy_f_w2=hy_f_w2, hy_f_b2=hy_f_b2, hy_f_w3=hy_f_w3, hy_f_b3=hy_f_b3,
        hy_f_wout=hy_f_wout, hy_f_freq=hy_f_freq, hy_skip=hy_skip,
        hy_w_out=hy_w_out.astype(BF16), hy_b_out=hy_b_out,
        at_w_qkv=at_w_qkv.astype(BF16), at_q_g=at_q_g, at_k_g=at_k_g, at_sink=at_sink,
        at_w_o=at_w_o.astype(BF16), rel_bias=rel_bias,
        ffn_w_gate_up=ffn_w_gate_up.astype(BF16), ffn_w_down=ffn_w_down.astype(BF16),
    )
    y_prompt = _trunk(x_prompt, p, dict(hy_cb=32, filt_cb=32))
    y_sample = _trunk(x_sample, p, dict(hy_cb=16, filt_cb=16))
    return (y_prompt, y_sample)
```

```python
import functools
import math

import jax
import jax.numpy as jnp
import numpy as np
from jax import lax
from jax.experimental import pallas as pl
from jax.experimental.pallas import tpu as pltpu

F32 = jnp.float32
BF16 = jnp.bfloat16

D_MODEL = 2048
DEPTH = 4
N_MIXERS = 2
HY_ORDER = 2
HY_EMB = 33
HY_BANDS = (HY_EMB - 1) // 2
HY_FILTER_W = 64
HY_FAST_PCT = 0.3
HY_SLOW_PCT = 1.5
HY_TARGET = 1e-2
N_HEADS = 16
HEAD_DIM = 128
N_KV = 4
GQA_G = N_HEADS // N_KV
WINDOW = 128
BLOCK = 128
N_BUCKETS = 32
MAX_DIST = 128
D_FF = -(-(8 * D_MODEL) // (3 * 256)) * 256
EPS = 1e-6
NEG = -1e30

V7X_LANES = 128
V7X_SUBLANES = 8
VMEM_LIMIT = 56 * 1024 * 1024

FFT_P = V7X_LANES
CH_GROUP = V7X_SUBLANES


def _cparams(*sem):
    return pltpu.CompilerParams(dimension_semantics=sem, vmem_limit_bytes=VMEM_LIMIT)


def _rms_bf16(x, g):
    ms = jnp.mean(x * x, axis=-1, keepdims=True)
    return (x * lax.rsqrt(ms + EPS) * g).astype(BF16)


def _norm_mm_body(x_ref, g_ref, w_ref, o_ref, hn_ref):
    @pl.when(pl.program_id(1) == 0)
    def _():
        hn_ref[...] = _rms_bf16(x_ref[...], g_ref[...])

    o_ref[...] = jnp.dot(hn_ref[...], w_ref[...], preferred_element_type=F32).astype(o_ref.dtype)


def norm_matmul(x, g, w, *, bm, bn, out_dtype=F32):
    T, K = x.shape
    N = w.shape[1]
    return pl.pallas_call(
        _norm_mm_body,
        grid=(T // bm, N // bn),
        in_specs=[
            pl.BlockSpec((bm, K), lambda i, j: (i, 0)),
            pl.BlockSpec((1, K), lambda i, j: (0, 0)),
            pl.BlockSpec((K, bn), lambda i, j: (0, j)),
        ],
        out_specs=pl.BlockSpec((bm, bn), lambda i, j: (i, j)),
        out_shape=jax.ShapeDtypeStruct((T, N), out_dtype),
        scratch_shapes=[pltpu.VMEM((bm, K), BF16)],
        compiler_params=_cparams("parallel", "arbitrary"),
        name="norm_matmul",
    )(x, g.reshape(1, K), w)


def _norm_mm_t_body(x_ref, g_ref, w_ref, b_ref, o_ref, hn_ref):
    @pl.when(pl.program_id(2) == 0)
    def _():
        hn_ref[...] = _rms_bf16(x_ref[0], g_ref[...])

    acc = lax.dot_general(w_ref[...], hn_ref[...], (((1,), (1,)), ((), ())), preferred_element_type=F32)
    o_ref[0] = acc + b_ref[...]


def norm_matmul_t(x, g, w_t, bias, *, bt, bc):
    B, L, K = x.shape
    C = w_t.shape[0]
    return pl.pallas_call(
        _norm_mm_t_body,
        grid=(B, L // bt, C // bc),
        in_specs=[
            pl.BlockSpec((1, bt, K), lambda b, t, c: (b, t, 0)),
            pl.BlockSpec((1, K), lambda b, t, c: (0, 0)),
            pl.BlockSpec((bc, K), lambda b, t, c: (c, 0)),
            pl.BlockSpec((bc, 1), lambda b, t, c: (c, 0)),
        ],
        out_specs=pl.BlockSpec((1, bc, bt), lambda b, t, c: (b, c, t)),
        out_shape=jax.ShapeDtypeStruct((B, C, L), F32),
        scratch_shapes=[pltpu.VMEM((bt, K), BF16)],
        compiler_params=_cparams("parallel", "parallel", "arbitrary"),
        name="norm_matmul_t",
    )(x, g.reshape(1, K), w_t, bias.reshape(C, 1))


def _res_mm_body(a_ref, w_ref, x_ref, o_ref):
    o_ref[...] = x_ref[...] + jnp.dot(a_ref[...], w_ref[...], preferred_element_type=F32)


def residual_matmul(x, a, w, *, bm, bn):
    T, K = a.shape
    N = w.shape[1]
    return pl.pallas_call(
        _res_mm_body,
        grid=(T // bm, N // bn),
        in_specs=[
            pl.BlockSpec((bm, K), lambda i, j: (i, 0)),
            pl.BlockSpec((K, bn), lambda i, j: (0, j)),
            pl.BlockSpec((bm, bn), lambda i, j: (i, j)),
        ],
        out_specs=pl.BlockSpec((bm, bn), lambda i, j: (i, j)),
        out_shape=jax.ShapeDtypeStruct((T, N), F32),
        compiler_params=_cparams("parallel", "parallel"),
        name="residual_matmul",
    )(a, w, x)


def _res_mm_t_body(z_ref, w_ref, b_ref, x_ref, o_ref):
    z = z_ref[0].astype(BF16)
    y = lax.dot_general(z, w_ref[...], (((0,), (0,)), ((), ())), preferred_element_type=F32)
    o_ref[0] = x_ref[0] + y + b_ref[...]


def residual_matmul_t(x, z_t, w, bias, *, bt):
    B, K, L = z_t.shape
    N = w.shape[1]
    return pl.pallas_call(
        _res_mm_t_body,
        grid=(B, L // bt),
        in_specs=[
            pl.BlockSpec((1, K, bt), lambda b, t: (b, 0, t)),
            pl.BlockSpec((K, N), lambda b, t: (0, 0)),
            pl.BlockSpec((1, N), lambda b, t: (0, 0)),
            pl.BlockSpec((1, bt, N), lambda b, t: (b, t, 0)),
        ],
        out_specs=pl.BlockSpec((1, bt, N), lambda b, t: (b, t, 0)),
        out_shape=jax.ShapeDtypeStruct((B, L, N), F32),
        compiler_params=_cparams("parallel", "parallel"),
        name="residual_matmul_t",
    )(z_t, w, bias.reshape(1, N), x)


def _ffn_body(x_ref, g_ref, wg_ref, wu_ref, wd_ref, o_ref, hn_ref):
    @pl.when(pl.program_id(1) == 0)
    def _():
        x = x_ref[...]
        hn_ref[...] = _rms_bf16(x, g_ref[...])
        o_ref[...] = x

    h = hn_ref[...]
    gate = jnp.dot(h, wg_ref[...], preferred_element_type=F32)
    up = jnp.dot(h, wu_ref[...], preferred_element_type=F32)
    act = (gate * jax.nn.sigmoid(gate) * up).astype(BF16)
    o_ref[...] += jnp.dot(act, wd_ref[...], preferred_element_type=F32)


def ffn_block(x, g, w_gate_up, w_down, *, bm, bf):
    T, K = x.shape
    nf = D_FF // bf
    return pl.pallas_call(
        _ffn_body,
        grid=(T // bm, nf),
        in_specs=[
            pl.BlockSpec((bm, K), lambda i, f: (i, 0)),
            pl.BlockSpec((1, K), lambda i, f: (0, 0)),
            pl.BlockSpec((K, bf), lambda i, f: (0, f)),
            pl.BlockSpec((K, bf), lambda i, f: (0, f + nf)),
            pl.BlockSpec((bf, K), lambda i, f: (f, 0)),
        ],
        out_specs=pl.BlockSpec((bm, K), lambda i, f: (i, 0)),
        out_shape=jax.ShapeDtypeStruct((T, K), F32),
        scratch_shapes=[pltpu.VMEM((bm, K), BF16)],
        compiler_params=_cparams("parallel", "arbitrary"),
        name="ffn_block",
    )(x, g.reshape(1, K), w_gate_up, w_gate_up, w_down)


ATT_TQ = 4 * BLOCK
KV_W = N_KV * HEAD_DIM


def _band_structure():
    qi = np.arange(BLOCK)[:, None]
    ki = np.arange(3 * BLOCK)[None, :]
    rel = ki - BLOCK - qi
    nb = N_BUCKETS // 2
    max_exact = nb // 2
    n = np.abs(rel)
    large = max_exact + (np.log(np.maximum(n, 1) / max_exact) / math.log(MAX_DIST / max_exact)
                         * (nb - max_exact)).astype(np.int32)
    large = np.minimum(large, nb - 1)
    buckets = (rel > 0).astype(np.int32) * nb + np.where(n < max_exact, n, large).astype(np.int32)
    band = n <= WINDOW
    return buckets, band


def _attn_body(q_ref, kp_ref, kc_ref, kn_ref, vp_ref, vc_ref, vn_ref, bias_ref, sink_ref,
               qg_ref, kg_ref, o_ref, kbuf, vbuf):
    i = pl.program_id(1)
    last = pl.num_programs(1) - 1
    qg = qg_ref[...]
    kg = kg_ref[...]

    def head_norm(t, gain):
        ms = jnp.mean(t * t, axis=-1, keepdims=True)
        return (t * lax.rsqrt(ms + EPS) * gain).astype(BF16)

    def k_norm(k):
        return jnp.concatenate(
            [head_norm(k[:, h * HEAD_DIM:(h + 1) * HEAD_DIM], kg) for h in range(N_KV)], axis=1)

    kbuf[0:BLOCK] = k_norm(kp_ref[0])
    kbuf[BLOCK:BLOCK + ATT_TQ] = k_norm(kc_ref[0])
    kbuf[BLOCK + ATT_TQ:] = k_norm(kn_ref[0])
    vbuf[0:BLOCK] = vp_ref[0].astype(BF16)
    vbuf[BLOCK:BLOCK + ATT_TQ] = vc_ref[0].astype(BF16)
    vbuf[BLOCK + ATT_TQ:] = vn_ref[0].astype(BF16)

    lane = lax.broadcasted_iota(jnp.int32, (1, 3 * BLOCK), 1)
    first_edge = jnp.where((lane < BLOCK) & (i == 0), NEG, 0.0).astype(F32)
    last_edge = jnp.where((lane >= 2 * BLOCK) & (i == last), NEG, 0.0).astype(F32)
    scale = HEAD_DIM ** -0.5

    for j in range(ATT_TQ // BLOCK):
        r0 = j * BLOCK
        for g in range(N_KV):
            qs = jnp.concatenate(
                [head_norm(q_ref[0, r0:r0 + BLOCK, (GQA_G * g + h) * HEAD_DIM:(GQA_G * g + h + 1) * HEAD_DIM], qg)
                 for h in range(GQA_G)], axis=0)
            kw = kbuf[r0:r0 + 3 * BLOCK, g * HEAD_DIM:(g + 1) * HEAD_DIM]
            vw = vbuf[r0:r0 + 3 * BLOCK, g * HEAD_DIM:(g + 1) * HEAD_DIM]
            s = lax.dot_general(qs, kw, (((1,), (1,)), ((), ())), preferred_element_type=F32)
            s = s * scale + bias_ref[g]
            if j == 0:
                s = s + first_edge
            if j == ATT_TQ // BLOCK - 1:
                s = s + last_edge
            sk = sink_ref[g][:, 0:1]
            m = jnp.maximum(jnp.max(s, axis=-1, keepdims=True), sk)
            p = jnp.exp(s - m)
            denom = jnp.sum(p, axis=-1, keepdims=True) + jnp.exp(sk - m)
            o = jnp.dot(p.astype(BF16), vw, preferred_element_type=F32) / denom
            for h in range(GQA_G):
                c0 = (GQA_G * g + h) * HEAD_DIM
                o_ref[0, r0:r0 + BLOCK, c0:c0 + HEAD_DIM] = o[h * BLOCK:(h + 1) * BLOCK].astype(o_ref.dtype)


def window_attention(qkv, q_g, k_g, sink, rel_bias):
    B, L, _ = qkv.shape
    nq = N_HEADS * HEAD_DIM
    sub = ATT_TQ // BLOCK
    nblk = L // BLOCK
    kcol = nq // KV_W
    vcol = kcol + 1

    buckets, band = _band_structure()
    bias = rel_bias[buckets].astype(F32)
    bias = jnp.where(band[:, :, None], bias, NEG)
    bias = jnp.transpose(bias, (2, 0, 1)).reshape(N_KV, GQA_G * BLOCK, 3 * BLOCK)
    sink_rows = jnp.broadcast_to(sink.astype(F32).reshape(N_KV, GQA_G, 1, 1),
                                 (N_KV, GQA_G, BLOCK, V7X_LANES)).reshape(N_KV, GQA_G * BLOCK, V7X_LANES)

    return pl.pallas_call(
        _attn_body,
        grid=(B, L // ATT_TQ),
        in_specs=[
            pl.BlockSpec((1, ATT_TQ, nq), lambda b, i: (b, i, 0)),
            pl.BlockSpec((1, BLOCK, KV_W), lambda b, i: (b, jnp.maximum(sub * i - 1, 0), kcol)),
            pl.BlockSpec((1, ATT_TQ, KV_W), lambda b, i: (b, i, kcol)),
            pl.BlockSpec((1, BLOCK, KV_W), lambda b, i: (b, jnp.minimum(sub * i + sub, nblk - 1), kcol)),
            pl.BlockSpec((1, BLOCK, KV_W), lambda b, i: (b, jnp.maximum(sub * i - 1, 0), vcol)),
            pl.BlockSpec((1, ATT_TQ, KV_W), lambda b, i: (b, i, vcol)),
            pl.BlockSpec((1, BLOCK, KV_W), lambda b, i: (b, jnp.minimum(sub * i + sub, nblk - 1), vcol)),
            pl.BlockSpec((N_KV, GQA_G * BLOCK, 3 * BLOCK), lambda b, i: (0, 0, 0)),
            pl.BlockSpec((N_KV, GQA_G * BLOCK, V7X_LANES), lambda b, i: (0, 0, 0)),
            pl.BlockSpec((1, HEAD_DIM), lambda b, i: (0, 0)),
            pl.BlockSpec((1, HEAD_DIM), lambda b, i: (0, 0)),
        ],
        out_specs=pl.BlockSpec((1, ATT_TQ, nq), lambda b, i: (b, i, 0)),
        out_shape=jax.ShapeDtypeStruct((B, L, nq), BF16),
        scratch_shapes=[pltpu.VMEM((ATT_TQ + 2 * BLOCK, KV_W), BF16),
                        pltpu.VMEM((ATT_TQ + 2 * BLOCK, KV_W), BF16)],
        compiler_params=_cparams("parallel", "parallel"),
        name="window_attention",
    )(qkv, qkv, qkv, qkv, qkv, qkv, qkv, bias, sink_rows,
      q_g.astype(F32).reshape(1, HEAD_DIM), k_g.astype(F32).reshape(1, HEAD_DIM))


FEAT_PAD = V7X_LANES


def _filter_mlp_body(feat_ref, w1_ref, b1_ref, w2_ref, b2_ref, w3_ref, b3_ref, fr_ref, a_ref):
    fr = fr_ref[...]
    a = jnp.sin(fr * (jnp.dot(feat_ref[...], w1_ref[...], preferred_element_type=F32) + b1_ref[...]))
    a = jnp.sin(fr * (jnp.dot(a.astype(BF16), w2_ref[...], preferred_element_type=F32) + b2_ref[...]))
    a = jnp.sin(fr * (jnp.dot(a.astype(BF16), w3_ref[...], preferred_element_type=F32) + b3_ref[...]))
    a_ref[...] = a.astype(BF16)


def hyena_filter_mlp(L, f_w1, f_b1, f_w2, f_b2, f_w3, f_b3, f_freq, *, bt=1024):
    t = jnp.linspace(0.0, 1.0, L, dtype=F32)[:, None]
    w = 2.0 * math.pi * jnp.arange(L, dtype=F32) / L
    f = jnp.linspace(1e-4, HY_BANDS - 1, HY_BANDS, dtype=F32)
    ang = w[:, None] * f[None, :]
    feats = jnp.concatenate([t, jnp.cos(ang), -jnp.sin(ang), jnp.zeros((L, FEAT_PAD - HY_EMB), F32)], axis=-1)
    w1 = jnp.concatenate([f_w1.astype(F32), jnp.zeros((FEAT_PAD - HY_EMB, HY_FILTER_W), F32)], axis=0)
    W = HY_FILTER_W
    const = lambda r, c: pl.BlockSpec((r, c), lambda i: (0, 0))
    return pl.pallas_call(
        _filter_mlp_body,
        grid=(L // bt,),
        in_specs=[pl.BlockSpec((bt, FEAT_PAD), lambda i: (i, 0)), const(FEAT_PAD, W), const(1, W), const(W, W),
                  const(1, W), const(W, W), const(1, W), const(1, W)],
        out_specs=pl.BlockSpec((bt, W), lambda i: (i, 0)),
        out_shape=jax.ShapeDtypeStruct((L, W), BF16),
        compiler_params=_cparams("parallel"),
        name="hyena_filter_mlp",
    )(feats.astype(BF16), w1.astype(BF16), f_b1.reshape(1, W), f_w2.astype(BF16), f_b2.reshape(1, W),
      f_w3.astype(BF16), f_b3.reshape(1, W), f_freq.astype(F32).reshape(1, W))


def _dft_tables(L, paired):
    N = 2 * L
    P = FFT_P
    Q = N // P
    S = L // P
    b = np.arange(Q, dtype=np.float64)
    fq = np.exp(-2j * np.pi * np.outer(b, np.arange(S)) / Q)
    fp =np.exp(-2j * np.pi * np.outer(np.arange(P), np.arange(P)) / P)
    tw = np.exp(-2j * np.pi * np.outer(b, np.arange(P)) / N)
    ci = np.conj(fq).T / N

    def stack(c):
        return np.block([[c.real, -c.imag], [c.imag, c.real]])

    if paired:
        g1 = stack(fq)
        g4 = stack(ci)
    else:
        g1 = np.concatenate([fq.real, fq.imag], axis=0)
        g4 = np.concatenate([ci.real, -ci.imag], axis=1)
    g1r = np.concatenate([fq.real, fq.imag], axis=0)
    g1rc = np.concatenate([fq.real, -fq.imag], axis=0)
    g2 =np.block([[fp.real, fp.imag], [-fp.imag, fp.real]])
    g2c = np.block([[fp.real, -fp.imag], [fp.imag, fp.real]])
    twr = np.tile(tw.real, (1, 2))
    twi = np.tile(tw.imag, (1, 2))
    f32 = lambda a: np.ascontiguousarray(a, dtype=np.float32)
    return dict(g1=f32(g1), g4=f32(g4), g1r=f32(g1r), g1rc=f32(g1rc), g2=f32(g2), g2c=f32(g2c),
                twr=f32(twr), twi=f32(twi), Q=Q, S=S, N=N)


def _to_tiles(x, n_tiles):
    chunks = [x[:, s * FFT_P:(s + 1) * FFT_P] for s in range(n_tiles)]
    return jnp.swapaxes(jnp.stack(chunks, axis=0), 0, 1)


def _from_tiles(x):
    y = jnp.swapaxes(x, 0, 1)
    return jnp.concatenate([y[s] for s in range(y.shape[0])], axis=1)


def _fwd_fft(re_tiles, im_tiles, g1, twr, twi, g2):
    G = len(re_tiles)
    Q = twr.shape[0]
    P = FFT_P
    rows = []
    for c in range(0, G, 2):
        top = jnp.concatenate([re_tiles[c], re_tiles[c + 1]], axis=1)
        if im_tiles is None:
            rhs = top
        else:
            rhs = jnp.concatenate([top, jnp.concatenate([im_tiles[c], im_tiles[c + 1]], axis=1)], axis=0)
        y = jnp.dot(g1, rhs.astype(BF16), preferred_element_type=F32)
        yr, yi = y[:Q], y[Q:]
        zr = yr * twr - yi * twi
        zi = yr * twi + yi * twr
        rows.append(jnp.concatenate([zr[:, :P], zi[:, :P]], axis=1))
        rows.append(jnp.concatenate([zr[:, P:], zi[:, P:]], axis=1))
    lhs = jnp.concatenate(rows, axis=0).astype(BF16)
    return jnp.dot(lhs, g2, preferred_element_type=F32)


def _inv_fft(spec, g2c, twr, twi, g4, want_imag):
    Q = twr.shape[0]
    P = FFT_P
    G = spec.shape[0] // Q
    S = g4.shape[0] // 2 if want_imag else g4.shape[0]
    y = jnp.dot(spec.astype(BF16), g2c, preferred_element_type=F32)
    out_re, out_im = [], []
    for c in range(0, G, 2):
        ya = y[c * Q:(c + 1) * Q]
        yb = y[(c + 1) * Q:(c + 2) * Q]
        yr = jnp.concatenate([ya[:, :P], yb[:, :P]], axis=1)
        yi = jnp.concatenate([ya[:, P:], yb[:, P:]], axis=1)
        zr = yr * twr + yi * twi
        zi = yi * twr - yr * twi
        rhs = jnp.concatenate([zr, zi], axis=0).astype(BF16)
        o = jnp.dot(g4, rhs, preferred_element_type=F32)
        out_re += [o[:S, :P], o[:S, P:]]
        if want_imag:
            out_im += [o[S:, :P], o[S:, P:]]
    return out_re, out_im


def _hyena_body(v_ref, x1_ref, x2_ref, a_ref, wo_ref, delta_ref, t_ref, par_ref, g1_ref, g1r_ref, g1rc_ref,
                twr_ref, twi_ref, g2_ref, g2c_ref, g4_ref, o_ref, taps_ref, kf_ref, *, S, Q, paired):
    g1, twr, twi = g1_ref[...], twr_ref[...], twi_ref[...]
    g2, g2c, g4 = g2_ref[...], g2c_ref[...], g4_ref[...]
    nb = 2 if paired else 1
    cb = v_ref.shape[1]
    n_groups = cb // CH_GROUP
    P = FFT_P
    G = CH_GROUP

    lane = lax.broadcasted_iota(jnp.int32, (G, S, P), 2)
    row = lax.broadcasted_iota(jnp.int32, (G, S, P), 1)

    @pl.when(pl.program_id(1) == 0)
    def _():
        g1r, g1rc = g1r_ref[...], g1rc_ref[...]
        ntwi = -twi
        n_rows = 2 * HY_ORDER
        decay = jnp.exp(-(delta_ref[...] * t_ref[...]))
        wo = wo_ref[...].reshape(n_rows * cb, HY_FILTER_W)
        h = lax.dot_general(wo, a_ref[...], (((1,), (1,)), ((), ())), preferred_element_type=F32)
        for q in range(n_rows):
            taps_ref[q] = h[q * cb:(q + 1) * cb] * decay

        def filter_group(gi, carry):
            c0 = pl.multiple_of(gi * G, G)
            for o in range(HY_ORDER):
                hf = _to_tiles(taps_ref[2 * o, pl.ds(c0, G), :], S)
                hb = _to_tiles(taps_ref[2 * o + 1, pl.ds(c0, G), :], S)
                hb = jnp.where((lane == 0) & (row == 0), 0.0, hb)
                norm = jnp.sum(jnp.abs(hf) + jnp.abs(hb), axis=(1, 2), keepdims=True)
                sf = _fwd_fft([hf[c] for c in range(G)], None, g1r, twr, twi, g2)
                sb = _fwd_fft([hb[c] for c in range(G)], None, g1rc, twr, ntwi, g2c)
                kf_ref[o, pl.ds(c0, G)] = (sf + sb).reshape(G, Q, 2 * P) * (1.0 / norm)
            return carry

        lax.fori_loop(0, n_groups, filter_group, 0)

    def short_conv(u, w0, w1, w2, bias):
        lr = pltpu.roll(u, 1, axis=2)
        prev = jnp.where(lane == 0, pltpu.roll(lr, 1, axis=1), lr)
        prev = jnp.where((lane == 0) & (row == 0), 0.0, prev)
        ll = pltpu.roll(u, P - 1, axis=2)
        nxt = jnp.where(lane == P - 1, pltpu.roll(ll, S - 1, axis=1), ll)
        nxt = jnp.where((lane == P - 1) & (row == S - 1), 0.0, nxt)
        return w0 * prev + w1 * u + w2 * nxt + bias

    def group(gi, carry):
        c0 = pl.multiple_of(gi * G, G)
        pw = lambda idx: par_ref[idx, pl.ds(c0, G)]

        def load(ref, b, base):
            u = _to_tiles(ref[b, pl.ds(c0, G), :], S)
            return short_conv(u, pw(base), pw(base + 1), pw(base + 2), pw(base + 3))

        z = [load(v_ref, b, 0) for b in range(nb)]
        gates = [[load(x1_ref, b, 4) for b in range(nb)], [load(x2_ref, b, 8) for b in range(nb)]]
        for o in range(HY_ORDER):
            re = [z[0][c] for c in range(G)]
            im = [z[1][c] for c in range(G)] if paired else None
            spec = _fwd_fft(re, im, g1, twr, twi, g2)
            kf = kf_ref[o, pl.ds(c0, G)].reshape(G * Q, 2 * P)
            xr, xi = spec[:, :P], spec[:, P:]
            kr, ki = kf[:, :P], kf[:, P:]
            prod = jnp.concatenate([xr * kr - xi * ki, xr * ki + xi * kr], axis=1)
            out_re, out_im = _inv_fft(prod, g2c, twr, twi, g4, paired)
            skip = pw(12 + o)
            conv = [jnp.stack(out_re, axis=0)] + ([jnp.stack(out_im, axis=0)] if paired else [])
            z = [gates[o][b] * (conv[b] + skip * z[b]) for b in range(nb)]
        for b in range(nb):
            o_ref[b, pl.ds(c0, G), :] = _from_tiles(z[b])
        return carry

    lax.fori_loop(0, n_groups, group, 0)


def hyena_operator(u_t, a, f_wout, conv_w, conv_b, skip, tabs, *, cb):
    B, C3, L = u_t.shape
    D = C3 // 3
    Q, S = tabs["Q"], tabs["S"]
    P = FFT_P
    W = HY_FILTER_W
    paired = B % 2 == 0
    nb = 2 if paired else 1
    n_rows = 2 * HY_ORDER
    wo = f_wout.T.reshape(n_rows, D, W).astype(BF16)
    deltas = np.abs(np.linspace(math.log(HY_TARGET) / HY_SLOW_PCT,
                                math.log(HY_TARGET) / HY_FAST_PCT, D)).astype(np.float32).reshape(D, 1)
    t_row = jnp.linspace(0.0, 1.0, L, dtype=F32).reshape(1, L)
    cols = []
    for part in range(3):
        sl = slice(part * D, (part + 1) * D)
        cols += [conv_w[0, sl], conv_w[1, sl], conv_w[2, sl], conv_b[sl]]
    cols += [skip[0], skip[1]]
    n_par = len(cols)
    par = jnp.broadcast_to(jnp.stack(cols, axis=0).astype(F32)[:, :, None, None], (n_par, D, 1, P))

    bf = lambda name: jnp.asarray(tabs[name]).astype(BF16)
    g1, g1r, g1rc, g2, g2c, g4 = bf("g1"), bf("g1r"), bf("g1rc"), bf("g2"), bf("g2c"), bf("g4")
    twr, twi = jnp.asarray(tabs["twr"]), jnp.asarray(tabs["twi"])
    const = lambda arr: pl.BlockSpec(arr.shape, lambda c, p: (0,) * arr.ndim)
    ncb = D // cb
    return pl.pallas_call(
        functools.partial(_hyena_body, S=S, Q=Q, paired=paired),
        grid=(ncb, B // nb),
        in_specs=[
            pl.BlockSpec((nb, cb, L), lambda c, p: (p, c, 0)),
            pl.BlockSpec((nb, cb, L), lambda c, p: (p, c + ncb, 0)),
            pl.BlockSpec((nb, cb, L), lambda c, p: (p, c + 2 * ncb, 0)),
            const(a),
            pl.BlockSpec((n_rows, cb, W), lambda c, p: (0, c, 0)),
            pl.BlockSpec((cb, 1), lambda c, p: (c, 0)),
            const(t_row),
            pl.BlockSpec((n_par, cb, 1, P), lambda c, p: (0, c, 0, 0)),
            const(g1), const(g1r), const(g1rc), const(twr), const(twi), const(g2), const(g2c), const(g4),
        ],
        out_specs=pl.BlockSpec((nb, cb, L), lambda c, p: (p, c, 0)),
        out_shape=jax.ShapeDtypeStruct((B, D, L), F32),
        scratch_shapes=[pltpu.VMEM((n_rows, cb, L), F32), pltpu.VMEM((HY_ORDER, cb, Q, 2 * P), F32)],
        compiler_params=_cparams("parallel", "arbitrary"),
        name="hyena_operator",
    )(u_t, u_t, u_t, a, wo, jnp.asarray(deltas), t_row, par, g1, g1r, g1rc, twr, twi, g2, g2c, g4)


def _trunk(x, p, cfg):
    B, L, D = x.shape
    tabs = _dft_tables(L, paired=(B % 2 == 0))
    for i in range(DEPTH):
        j = i // N_MIXERS
        if i % N_MIXERS == 0:
            a = hyena_filter_mlp(L, p["hy_f_w1"][j], p["hy_f_b1"][j], p["hy_f_w2"][j], p["hy_f_b2"][j],
                                 p["hy_f_w3"][j], p["hy_f_b3"][j], p["hy_f_freq"][j])
            u_t = norm_matmul_t(x, p["norm_mix_g"][i], p["hy_w_in_t"][j], p["hy_b_in"][j], bt=1024, bc=1024)
            z_t = hyena_operator(u_t, a, p["hy_f_wout"][j], p["hy_conv_w"][j], p["hy_conv_b"][j], p["hy_skip"][j],
                                 tabs, cb=cfg["hy_cb"])
            x = residual_matmul_t(x, z_t, p["hy_w_out"][j], p["hy_b_out"][j], bt=512)
        else:
            x2 = x.reshape(B * L, D)
            qkv = norm_matmul(x2, p["norm_mix_g"][i], p["at_w_qkv"][j], bm=1024, bn=1024)
            att = window_attention(qkv.reshape(B, L, -1), p["at_q_g"][j], p["at_k_g"][j], p["at_sink"][j],
                                   p["rel_bias"])
            x = residual_matmul(x2, att.reshape(B * L, -1), p["at_w_o"][j], bm=1024, bn=1024).reshape(B, L, D)
        x = ffn_block(x.reshape(B * L, D), p["norm_ffn_g"][i], p["ffn_w_gate_up"][i], p["ffn_w_down"][i],
                      bm=512, bf=512).reshape(B, L, D)
    return x


def kernel(x_prompt, x_sample, norm_mix_g, norm_ffn_g, hy_w_in, hy_b_in, hy_conv_w, hy_conv_b, hy_f_w1, hy_f_b1,
           hy_f_w2, hy_f_b2, hy_f_w3, hy_f_b3, hy_f_wout, hy_f_freq, hy_skip, hy_w_out, hy_b_out, at_w_qkv, at_q_g,
           at_k_g, at_sink, at_w_o, rel_bias, ffn_w_gate_up, ffn_w_down):
    p = dict(
        norm_mix_g=norm_mix_g.astype(F32), norm_ffn_g=norm_ffn_g.astype(F32),
        hy_w_in_t=jnp.swapaxes(hy_w_in, 1, 2).astype(BF16), hy_b_in=hy_b_in,
        hy_conv_w=hy_conv_w, hy_conv_b=hy_conv_b,
        hy_f_w1=hy_f_w1, hy_f_b1=hy_f_b1, hy_f_w2=hy_f_w2, hy_f_b2=hy_f_b2, hy_f_w3=hy_f_w3, hy_f_b3=hy_f_b3,
        hy_f_wout=hy_f_wout, hy_f_freq=hy_f_freq, hy_skip=hy_skip,
        hy_w_out=hy_w_out.astype(BF16), hy_b_out=hy_b_out,
        at_w_qkv=at_w_qkv.astype(BF16), at_q_g=at_q_g, at_k_g=at_k_g, at_sink=at_sink,
        at_w_o=at_w_o.astype(BF16), rel_bias=rel_bias,
        ffn_w_gate_up=ffn_w_gate_up.astype(BF16), ffn_w_down=ffn_w_down.astype(BF16),
    )
    y_prompt = _trunk(x_prompt, p, dict(hy_cb=32))
    y_sample = _trunk(x_sample, p, dict(hy_cb=16))
    return (y_prompt, y_sample)
```

```python
import functools
import math

import jax
import jax.numpy as jnp
import numpy as np
from jax import lax
from jax.experimental import pallas as pl
from jax.experimental.pallas import tpu as pltpu

F32 = jnp.float32
BF16 = jnp.bfloat16

D_MODEL = 2048
DEPTH = 4
N_MIXERS = 2
HY_ORDER = 2
HY_EMB = 33
HY_BANDS = (HY_EMB - 1) // 2
HY_FILTER_W = 64
HY_FAST_PCT = 0.3
HY_SLOW_PCT = 1.5
HY_TARGET = 1e-2
N_HEADS = 16
HEAD_DIM = 128
N_KV = 4
GQA_G = N_HEADS // N_KV
WINDOW = 128
BLOCK = 128
N_BUCKETS = 32
MAX_DIST = 128
D_FF = -(-(8 * D_MODEL) // (3 * 256)) * 256
EPS = 1e-6
NEG = -1e30

V7X_LANES = 128
V7X_SUBLANES = 8
VMEM_LIMIT = 56 * 1024 * 1024

FFT_P = V7X_LANES
CH_GROUP = V7X_SUBLANES


def _cparams(*sem):
    return pltpu.CompilerParams(dimension_semantics=sem, vmem_limit_bytes=VMEM_LIMIT)


def _rms_bf16(x, g):
    ms = jnp.mean(x * x, axis=-1, keepdims=True)
    return (x * lax.rsqrt(ms + EPS) * g).astype(BF16)


def _norm_mm_body(x_ref, g_ref, w_ref, o_ref, hn_ref):
    @pl.when(pl.program_id(1) == 0)
    def _():
        hn_ref[...] = _rms_bf16(x_ref[...], g_ref[...])

    o_ref[...] = jnp.dot(hn_ref[...], w_ref[...], preferred_element_type=F32).astype(o_ref.dtype)


def norm_matmul(x, g, w, *, bm, bn, out_dtype=F32):
    T, K = x.shape
    N = w.shape[1]
    return pl.pallas_call(
        _norm_mm_body,
        grid=(T // bm, N // bn),
        in_specs=[
            pl.BlockSpec((bm, K), lambda i, j: (i, 0)),
            pl.BlockSpec((1, K), lambda i, j: (0, 0)),
            pl.BlockSpec((K, bn), lambda i, j: (0, j)),
        ],
        out_specs=pl.BlockSpec((bm, bn), lambda i, j: (i, j)),
        out_shape=jax.ShapeDtypeStruct((T, N), out_dtype),
        scratch_shapes=[pltpu.VMEM((bm, K), BF16)],
        compiler_params=_cparams("parallel", "arbitrary"),
        name="norm_matmul",
    )(x, g.reshape(1, K), w)


def _norm_mm_t_body(x_ref, g_ref, w_ref, b_ref, o_ref, hn_ref):
    @pl.when(pl.program_id(2) == 0)
    def _():
        hn_ref[...] = _rms_bf16(x_ref[0], g_ref[...])

    acc = lax.dot_general(w_ref[...], hn_ref[...], (((1,), (1,)), ((), ())), preferred_element_type=F32)
    o_ref[0] = acc + b_ref[...]


def norm_matmul_t(x, g, w_t, bias, *, bt, bc):
    B, L, K = x.shape
    C = w_t.shape[0]
    return pl.pallas_call(
        _norm_mm_t_body,
        grid=(B, L // bt, C // bc),
        in_specs=[
            pl.BlockSpec((1, bt, K), lambda b, t, c: (b, t, 0)),
            pl.BlockSpec((1, K), lambda b, t, c: (0, 0)),
            pl.BlockSpec((bc, K), lambda b, t, c: (c, 0)),
            pl.BlockSpec((bc, 1), lambda b, t, c: (c, 0)),
        ],
        out_specs=pl.BlockSpec((1, bc, bt), lambda b, t, c: (b, c, t)),
        out_shape=jax.ShapeDtypeStruct((B, C, L), F32),
        scratch_shapes=[pltpu.VMEM((bt, K), BF16)],
        compiler_params=_cparams("parallel", "parallel", "arbitrary"),
        name="norm_matmul_t",
    )(x, g.reshape(1, K), w_t, bias.reshape(C, 1))


def _res_mm_body(a_ref, w_ref, x_ref, o_ref):
    o_ref[...] = x_ref[...] + jnp.dot(a_ref[...], w_ref[...], preferred_element_type=F32)


def residual_matmul(x, a, w, *, bm, bn):
    T, K = a.shape
    N = w.shape[1]
    return pl.pallas_call(
        _res_mm_body,
        grid=(T // bm, N // bn),
        in_specs=[
            pl.BlockSpec((bm, K), lambda i, j: (i, 0)),
            pl.BlockSpec((K, bn), lambda i, j: (0, j)),
            pl.BlockSpec((bm, bn), lambda i, j: (i, j)),
        ],
        out_specs=pl.BlockSpec((bm, bn), lambda i, j: (i, j)),
        out_shape=jax.ShapeDtypeStruct((T, N), F32),
        compiler_params=_cparams("parallel", "parallel"),
        name="residual_matmul",
    )(a, w, x)


def _res_mm_t_body(z_ref, w_ref, b_ref, x_ref, o_ref):
    z = z_ref[0].astype(BF16)
    y = lax.dot_general(z, w_ref[...], (((0,), (0,)), ((), ())), preferred_element_type=F32)
    o_ref[0] = x_ref[0] + y + b_ref[...]


def residual_matmul_t(x, z_t, w, bias, *, bt):
    B, K, L = z_t.shape
    N = w.shape[1]
    return pl.pallas_call(
        _res_mm_t_body,
        grid=(B, L // bt),
        in_specs=[
            pl.BlockSpec((1, K, bt), lambda b, t: (b, 0, t)),
            pl.BlockSpec((K, N), lambda b, t: (0, 0)),
            pl.BlockSpec((1, N), lambda b, t: (0, 0)),
            pl.BlockSpec((1, bt, N), lambda b, t: (b, t, 0)),
        ],
        out_specs=pl.BlockSpec((1, bt, N), lambda b, t: (b, t, 0)),
        out_shape=jax.ShapeDtypeStruct((B, L, N), F32),
        compiler_params=_cparams("parallel", "parallel"),
        name="residual_matmul_t",
    )(z_t, w, bias.reshape(1, N), x)


def _ffn_body(x_ref, g_ref, wg_ref, wu_ref, wd_ref, o_ref, hn_ref):
    @pl.when(pl.program_id(1) == 0)
    def _():
        x = x_ref[...]
        hn_ref[...] = _rms_bf16(x, g_ref[...])
        o_ref[...] = x

    h = hn_ref[...]
    gate = jnp.dot(h, wg_ref[...], preferred_element_type=F32)
    up = jnp.dot(h, wu_ref[...], preferred_element_type=F32)
    act = (gate * jax.nn.sigmoid(gate) * up).astype(BF16)
    o_ref[...] += jnp.dot(act, wd_ref[...], preferred_element_type=F32)


def ffn_block(x, g, w_gate_up, w_down, *, bm, bf):
    T, K = x.shape
    nf = D_FF // bf
    return pl.pallas_call(
        _ffn_body,
        grid=(T // bm, nf),
        in_specs=[
            pl.BlockSpec((bm, K), lambda i, f: (i, 0)),
            pl.BlockSpec((1, K), lambda i, f: (0, 0)),
            pl.BlockSpec((K, bf), lambda i, f: (0, f)),
            pl.BlockSpec((K, bf), lambda i, f: (0, f + nf)),
            pl.BlockSpec((bf, K), lambda i, f: (f, 0)),
        ],
        out_specs=pl.BlockSpec((bm, K), lambda i, f: (i, 0)),
        out_shape=jax.ShapeDtypeStruct((T, K), F32),
        scratch_shapes=[pltpu.VMEM((bm, K), BF16)],
        compiler_params=_cparams("parallel", "arbitrary"),
        name="ffn_block",
    )(x, g.reshape(1, K), w_gate_up, w_gate_up, w_down)


ATT_TQ = 4 * BLOCK
KV_W = N_KV * HEAD_DIM


def _band_structure():
    qi = np.arange(BLOCK)[:, None]
    ki = np.arange(3 * BLOCK)[None, :]
    rel = ki - BLOCK - qi
    nb = N_BUCKETS // 2
    max_exact = nb // 2
    n = np.abs(rel)
    large = max_exact + (np.log(np.maximum(n, 1) / max_exact) / math.log(MAX_DIST / max_exact)
                         * (nb - max_exact)).astype(np.int32)
    large = np.minimum(large, nb - 1)
    buckets = (rel > 0).astype(np.int32) * nb + np.where(n < max_exact, n, large).astype(np.int32)
    band = n <= WINDOW
    return buckets, band


def _attn_body(q_ref, kp_ref, kc_ref, kn_ref, vp_ref, vc_ref, vn_ref, bias_ref, sink_ref,
               qg_ref, kg_ref, o_ref, kbuf, vbuf):
    i = pl.program_id(1)
    last = pl.num_programs(1) - 1
    qg = qg_ref[...]
    kg = kg_ref[...]

    def head_norm(t, gain):
        ms = jnp.mean(t * t, axis=-1, keepdims=True)
        return (t * lax.rsqrt(ms + EPS) * gain).astype(BF16)

    def k_norm(k):
        return jnp.concatenate(
            [head_norm(k[:, h * HEAD_DIM:(h + 1) * HEAD_DIM], kg) for h in range(N_KV)], axis=1)

    kbuf[0:BLOCK] = k_norm(kp_ref[0])
    kbuf[BLOCK:BLOCK + ATT_TQ] = k_norm(kc_ref[0])
    kbuf[BLOCK + ATT_TQ:] = k_norm(kn_ref[0])
    vbuf[0:BLOCK] = vp_ref[0].astype(BF16)
    vbuf[BLOCK:BLOCK + ATT_TQ] = vc_ref[0].astype(BF16)
    vbuf[BLOCK + ATT_TQ:] = vn_ref[0].astype(BF16)

    lane = lax.broadcasted_iota(jnp.int32, (1, 3 * BLOCK), 1)
    first_edge = jnp.where((lane < BLOCK) & (i == 0), NEG, 0.0).astype(F32)
    last_edge = jnp.where((lane >= 2 * BLOCK) & (i == last), NEG, 0.0).astype(F32)
    scale = HEAD_DIM ** -0.5

    for j in range(ATT_TQ // BLOCK):
        r0 = j * BLOCK
        for g in range(N_KV):
            qs = jnp.concatenate(
                [head_norm(q_ref[0, r0:r0 + BLOCK, (GQA_G * g + h) * HEAD_DIM:(GQA_G * g + h + 1) * HEAD_DIM], qg)
                 for h in range(GQA_G)], axis=0)
            kw = kbuf[r0:r0 + 3 * BLOCK, g * HEAD_DIM:(g + 1) * HEAD_DIM]
            vw = vbuf[r0:r0 + 3 * BLOCK, g * HEAD_DIM:(g + 1) * HEAD_DIM]
            s = lax.dot_general(qs, kw, (((1,), (1,)), ((), ())), preferred_element_type=F32)
            s = s * scale + bias_ref[g]
            if j == 0:
                s = s + first_edge
            if j == ATT_TQ // BLOCK - 1:
                s = s + last_edge
            sk = sink_ref[g][:, 0:1]
            m = jnp.maximum(jnp.max(s, axis=-1, keepdims=True), sk)
            p = jnp.exp(s - m)
            denom = jnp.sum(p, axis=-1, keepdims=True) + jnp.exp(sk - m)
            o = jnp.dot(p.astype(BF16), vw, preferred_element_type=F32) / denom
            for h in range(GQA_G):
                c0 = (GQA_G * g + h) * HEAD_DIM
                o_ref[0, r0:r0 + BLOCK, c0:c0 + HEAD_DIM] = o[h * BLOCK:(h + 1) * BLOCK].astype(o_ref.dtype)


def window_attention(qkv, q_g, k_g, sink, rel_bias):
    B, L, _ = qkv.shape
    nq = N_HEADS * HEAD_DIM
    sub = ATT_TQ // BLOCK
    nblk = L // BLOCK
    kcol = nq // KV_W
    vcol = kcol + 1

    buckets, band = _band_structure()
    bias = rel_bias[buckets].astype(F32)
    bias = jnp.where(band[:, :, None], bias, NEG)
    bias = jnp.transpose(bias, (2, 0, 1)).reshape(N_KV, GQA_G * BLOCK, 3 * BLOCK)
    sink_rows = jnp.broadcast_to(sink.astype(F32).reshape(N_KV, GQA_G, 1, 1),
                                 (N_KV, GQA_G, BLOCK, V7X_LANES)).reshape(N_KV, GQA_G * BLOCK, V7X_LANES)

    return pl.pallas_call(
        _attn_body,
        grid=(B, L // ATT_TQ),
        in_specs=[
            pl.BlockSpec((1, ATT_TQ, nq), lambda b, i: (b, i, 0)),
            pl.BlockSpec((1, BLOCK, KV_W), lambda b, i: (b, jnp.maximum(sub * i - 1, 0), kcol)),
            pl.BlockSpec((1, ATT_TQ, KV_W), lambda b, i: (b, i, kcol)),
            pl.BlockSpec((1, BLOCK, KV_W), lambda b, i: (b, jnp.minimum(sub * i + sub, nblk - 1), kcol)),
            pl.BlockSpec((1, BLOCK, KV_W), lambda b, i: (b, jnp.maximum(sub * i - 1, 0), vcol)),
            pl.BlockSpec((1, ATT_TQ, KV_W), lambda b, i: (b, i, vcol)),
            pl.BlockSpec((1, BLOCK, KV_W), lambda b, i: (b, jnp.minimum(sub * i + sub, nblk - 1), vcol)),
            pl.BlockSpec((N_KV, GQA_G * BLOCK, 3 * BLOCK), lambda b, i: (0, 0, 0)),
            pl.BlockSpec((N_KV, GQA_G * BLOCK, V7X_LANES), lambda b, i: (0, 0, 0)),
            pl.BlockSpec((1, HEAD_DIM), lambda b, i: (0, 0)),
            pl.BlockSpec((1, HEAD_DIM), lambda b, i: (0, 0)),
        ],
        out_specs=pl.BlockSpec((1, ATT_TQ, nq), lambda b, i: (b, i, 0)),
        out_shape=jax.ShapeDtypeStruct((B, L, nq), BF16),
        scratch_shapes=[pltpu.VMEM((ATT_TQ + 2 * BLOCK, KV_W), BF16),
                        pltpu.VMEM((ATT_TQ + 2 * BLOCK, KV_W), BF16)],
        compiler_params=_cparams("parallel", "parallel"),
        name="window_attention",
    )(qkv, qkv, qkv, qkv, qkv, qkv, qkv, bias, sink_rows,
      q_g.astype(F32).reshape(1, HEAD_DIM), k_g.astype(F32).reshape(1, HEAD_DIM))


FEAT_PAD = V7X_LANES


def _filter_mlp_body(feat_ref, w1_ref, b1_ref, w2_ref, b2_ref, w3_ref, b3_ref, fr_ref, a_ref):
    fr = fr_ref[...]
    a = jnp.sin(fr * (jnp.dot(feat_ref[...], w1_ref[...], preferred_element_type=F32) + b1_ref[...]))
    a = jnp.sin(fr * (jnp.dot(a.astype(BF16), w2_ref[...], preferred_element_type=F32) + b2_ref[...]))
    a = jnp.sin(fr * (jnp.dot(a.astype(BF16), w3_ref[...], preferred_element_type=F32) + b3_ref[...]))
    a_ref[...] = a.astype(BF16)


def hyena_filter_mlp(L, f_w1, f_b1, f_w2, f_b2, f_w3, f_b3, f_freq, *, bt=1024):
    t = jnp.linspace(0.0, 1.0, L, dtype=F32)[:, None]
    w = 2.0 * math.pi * jnp.arange(L, dtype=F32) / L
    f = jnp.linspace(1e-4, HY_BANDS - 1, HY_BANDS, dtype=F32)
    ang = w[:, None] * f[None, :]
    feats = jnp.concatenate([t, jnp.cos(ang), -jnp.sin(ang), jnp.zeros((L, FEAT_PAD - HY_EMB), F32)], axis=-1)
    w1 = jnp.concatenate([f_w1.astype(F32), jnp.zeros((FEAT_PAD - HY_EMB, HY_FILTER_W), F32)], axis=0)
    W = HY_FILTER_W
    const = lambda r, c: pl.BlockSpec((r, c), lambda i: (0, 0))
    return pl.pallas_call(
        _filter_mlp_body,
        grid=(L // bt,),
        in_specs=[pl.BlockSpec((bt, FEAT_PAD), lambda i: (i, 0)), const(FEAT_PAD, W), const(1, W), const(W, W),
                  const(1, W), const(W, W), const(1, W), const(1, W)],
        out_specs=pl.BlockSpec((bt, W), lambda i: (i, 0)),
        out_shape=jax.ShapeDtypeStruct((L, W), BF16),
        compiler_params=_cparams("parallel"),
        name="hyena_filter_mlp",
    )(feats.astype(BF16), w1.astype(BF16), f_b1.reshape(1, W), f_w2.astype(BF16), f_b2.reshape(1, W),
      f_w3.astype(BF16), f_b3.reshape(1, W), f_freq.astype(F32).reshape(1, W))


def _dft_tables(L, paired):
    N = 2 * L
    P = FFT_P
    Q = N // P
    S = L // P
    b = np.arange(Q, dtype=np.float64)
    fq = np.exp(-2j * np.pi * np.outer(b, np.arange(S)) / Q)
    fp =np.exp(-2j * np.pi * np.outer(np.arange(P), np.arange(P)) / P)
    tw = np.exp(-2j * np.pi * np.outer(b, np.arange(P)) / N)
    ci = np.conj(fq).T / N

    def stack(c):
        return np.block([[c.real, -c.imag], [c.imag, c.real]])

    if paired:
        g1 = stack(fq)
        g4 = stack(ci)
    else:
        g1 = np.concatenate([fq.real, fq.imag], axis=0)
        g4 = np.concatenate([ci.real, -ci.imag], axis=1)
    fq_full = np.exp(-2j * np.pi * np.outer(b, np.arange(Q)) / Q)
    g1_full = np.concatenate([fq_full.real, fq_full.imag], axis=0)
    g2 =np.block([[fp.real, fp.imag], [-fp.imag, fp.real]])
    g2c = np.block([[fp.real, -fp.imag], [fp.imag, fp.real]])
    twr = np.tile(tw.real, (1, 2))
    twi = np.tile(tw.imag, (1, 2))
    f32 = lambda a: np.ascontiguousarray(a, dtype=np.float32)
    return dict(g1=f32(g1), g4=f32(g4), g1_full=f32(g1_full), g2=f32(g2), g2c=f32(g2c),
                twr=f32(twr), twi=f32(twi), Q=Q, S=S, N=N)


def _to_tiles(x, n_tiles):
    chunks = [x[:, s * FFT_P:(s + 1) * FFT_P] for s in range(n_tiles)]
    return jnp.swapaxes(jnp.stack(chunks, axis=0), 0, 1)


def _from_tiles(x):
    y = jnp.swapaxes(x, 0, 1)
    return jnp.concatenate([y[s] for s in range(y.shape[0])], axis=1)


def _fwd_fft(re_tiles, im_tiles, g1, twr, twi, g2):
    G = len(re_tiles)
    Q = twr.shape[0]
    P = FFT_P
    rows = []
    for c in range(0, G, 2):
        top = jnp.concatenate([re_tiles[c], re_tiles[c + 1]], axis=1)
        if im_tiles is None:
            rhs = top
        else:
            rhs = jnp.concatenate([top, jnp.concatenate([im_tiles[c], im_tiles[c + 1]], axis=1)], axis=0)
        y = jnp.dot(g1, rhs.astype(BF16), preferred_element_type=F32)
        yr, yi = y[:Q], y[Q:]
        zr = yr * twr - yi * twi
        zi = yr * twi + yi * twr
        rows.append(jnp.concatenate([zr[:, :P], zi[:, :P]], axis=1))
        rows.append(jnp.concatenate([zr[:, P:], zi[:, P:]], axis=1))
    lhs = jnp.concatenate(rows, axis=0).astype(BF16)
    return jnp.dot(lhs, g2, preferred_element_type=F32)


def _inv_fft(spec, g2c, twr, twi, g4, want_imag):
    Q = twr.shape[0]
    P = FFT_P
    G = spec.shape[0] // Q
    S = g4.shape[0] // 2 if want_imag else g4.shape[0]
    y = jnp.dot(spec.astype(BF16), g2c, preferred_element_type=F32)
    out_re, out_im = [], []
    for c in range(0, G, 2):
        ya = y[c * Q:(c + 1) * Q]
        yb = y[(c + 1) * Q:(c + 2) * Q]
        yr = jnp.concatenate([ya[:, :P], yb[:, :P]], axis=1)
        yi = jnp.concatenate([ya[:, P:], yb[:, P:]], axis=1)
        zr = yr * twr + yi * twi
        zi = yi * twr - yr * twi
        rhs = jnp.concatenate([zr, zi], axis=0).astype(BF16)
        o = jnp.dot(g4, rhs, preferred_element_type=F32)
        out_re += [o[:S, :P], o[:S, P:]]
        if want_imag:
            out_im += [o[S:, :P], o[S:, P:]]
    return out_re, out_im


def _hyena_body(v_ref, x1_ref, x2_ref, af_ref, ab_ref, wof_ref, wob_ref, delta_ref, tf_ref, tb_ref, par_ref,
                g1_ref, g1f_ref, twr_ref, twi_ref, g2_ref, g2c_ref, g4_ref, o_ref, taps_ref, kf_ref,
                *, S, Q, paired):
    g1, twr, twi = g1_ref[...], twr_ref[...], twi_ref[...]
    g2, g2c, g4 = g2_ref[...], g2c_ref[...], g4_ref[...]
    nb = 2 if paired else 1
    cb = v_ref.shape[1]
    n_groups = cb // CH_GROUP
    P = FFT_P
    G = CH_GROUP

    lane = lax.broadcasted_iota(jnp.int32, (G, S, P), 2)
    row = lax.broadcasted_iota(jnp.int32, (G, S, P), 1)

    @pl.when(pl.program_id(1) == 0)
    def _():
        g1f = g1f_ref[...]
        L = S * P
        delta = delta_ref[...]
        halves = ((wof_ref, af_ref, jnp.exp(-(delta * tf_ref[...]))),
                  (wob_ref, ab_ref, jnp.where(lax.broadcasted_iota(jnp.int32, (cb, L), 1) == 0, 0.0,
                                              jnp.exp(-(delta * tb_ref[...])))))
        for half, (wo_ref, a_ref, decay) in enumerate(halves):
            wo = wo_ref[...].reshape(HY_ORDER * cb, HY_FILTER_W)
            h = lax.dot_general(wo, a_ref[...], (((1,), (1,)), ((), ())), preferred_element_type=F32)
            for o in range(HY_ORDER):
                taps_ref[o, :, half * L:(half + 1) * L] = h[o * cb:(o + 1) * cb] * decay

        def filter_group(gi, carry):
            c0 = pl.multiple_of(gi * G, G)
            for o in range(HY_ORDER):
                k = taps_ref[o, pl.ds(c0, G), :]
                norm = jnp.sum(jnp.abs(k), axis=-1, keepdims=True)
                tiles = _to_tiles(k, Q)
                spec = _fwd_fft([tiles[c] for c in range(G)], None, g1f, twr, twi, g2)
                kf_ref[o, pl.ds(c0, G)] = spec.reshape(G, Q, 2 * P) * (1.0 / norm)[:, :, None]
            return carry

        lax.fori_loop(0, n_groups, filter_group, 0)

    def short_conv(u, w0, w1, w2, bias):
        lr = pltpu.roll(u, 1, axis=2)
        prev = jnp.where(lane == 0, pltpu.roll(lr, 1, axis=1), lr)
        prev = jnp.where((lane == 0) & (row == 0), 0.0, prev)
        ll = pltpu.roll(u, P - 1, axis=2)
        nxt = jnp.where(lane == P - 1, pltpu.roll(ll, S - 1, axis=1), ll)
        nxt = jnp.where((lane == P - 1) & (row == S - 1), 0.0, nxt)
        return w0 * prev + w1 * u + w2 * nxt + bias

    def group(gi, carry):
        c0 = pl.multiple_of(gi * G, G)
        pw = lambda idx: par_ref[idx, pl.ds(c0, G)]

        def load(ref, b, base):
            u = _to_tiles(ref[b, pl.ds(c0, G), :], S)
            return short_conv(u, pw(base), pw(base + 1), pw(base + 2), pw(base + 3))

        z = [load(v_ref, b, 0) for b in range(nb)]
        gates = [[load(x1_ref, b, 4) for b in range(nb)], [load(x2_ref, b, 8) for b in range(nb)]]
        for o in range(HY_ORDER):
            re = [z[0][c] for c in range(G)]
            im = [z[1][c] for c in range(G)] if paired else None
            spec = _fwd_fft(re, im, g1, twr, twi, g2)
            kf = kf_ref[o, pl.ds(c0, G)].reshape(G * Q, 2 * P)
            xr, xi = spec[:, :P], spec[:, P:]
            kr, ki = kf[:, :P], kf[:, P:]
            prod = jnp.concatenate([xr * kr - xi * ki, xr * ki + xi * kr], axis=1)
            out_re, out_im = _inv_fft(prod, g2c, twr, twi, g4, paired)
            skip = pw(12 + o)
            conv = [jnp.stack(out_re, axis=0)] + ([jnp.stack(out_im, axis=0)] if paired else [])
            z = [gates[o][b] * (conv[b] + skip * z[b]) for b in range(nb)]
        for b in range(nb):
            o_ref[b, pl.ds(c0, G), :] = _from_tiles(z[b])
        return carry

    lax.fori_loop(0, n_groups, group, 0)


def hyena_operator(u_t, a, f_wout, conv_w, conv_b, skip, tabs, *, cb):
    B, C3, L = u_t.shape
    D = C3 // 3
    Q, S = tabs["Q"], tabs["S"]
    P = FFT_P
    W = HY_FILTER_W
    paired = B % 2 == 0
    nb = 2 if paired else 1
    wo = f_wout.T.reshape(HY_ORDER, 2, D, W).astype(BF16)
    wo_f, wo_b = wo[:, 0], wo[:, 1]
    deltas = np.abs(np.linspace(math.log(HY_TARGET) / HY_SLOW_PCT,
                                math.log(HY_TARGET) / HY_FAST_PCT, D)).astype(np.float32).reshape(D, 1)
    t_f = jnp.linspace(0.0, 1.0, L, dtype=F32).reshape(1, L)
    a_b = jnp.roll(jnp.flip(a, axis=0), 1, axis=0)
    t_b = jnp.roll(jnp.flip(t_f, axis=1), 1, axis=1)
    cols = []
    for part in range(3):
        sl = slice(part * D, (part + 1) * D)
        cols += [conv_w[0, sl], conv_w[1, sl], conv_w[2, sl], conv_b[sl]]
    cols += [skip[0], skip[1]]
    n_par = len(cols)
    par = jnp.broadcast_to(jnp.stack(cols, axis=0).astype(F32)[:, :, None, None], (n_par, D, 1, P))

    bf = lambda name: jnp.asarray(tabs[name]).astype(BF16)
    g1, g1f, g2, g2c, g4 = bf("g1"), bf("g1_full"), bf("g2"), bf("g2c"), bf("g4")
    twr, twi = jnp.asarray(tabs["twr"]), jnp.asarray(tabs["twi"])
    const = lambda arr: pl.BlockSpec(arr.shape, lambda c, p: (0,) * arr.ndim)
    ncb = D // cb
    return pl.pallas_call(
        functools.partial(_hyena_body, S=S, Q=Q, paired=paired),
        grid=(ncb, B // nb),
        in_specs=[
            pl.BlockSpec((nb, cb, L), lambda c, p: (p, c, 0)),
            pl.BlockSpec((nb, cb, L), lambda c, p: (p, c + ncb, 0)),
            pl.BlockSpec((nb, cb, L), lambda c, p: (p, c + 2 * ncb, 0)),
            const(a), const(a_b),
            pl.BlockSpec((HY_ORDER, cb, W), lambda c, p: (0, c, 0)),
            pl.BlockSpec((HY_ORDER, cb, W), lambda c, p: (0, c, 0)),
            pl.BlockSpec((cb, 1), lambda c, p: (c, 0)),
            const(t_f), const(t_b),
            pl.BlockSpec((n_par, cb, 1, P), lambda c, p: (0, c, 0, 0)),
            const(g1), const(g1f), const(twr), const(twi), const(g2), const(g2c), const(g4),
        ],
        out_specs=pl.BlockSpec((nb, cb, L), lambda c, p: (p, c, 0)),
        out_shape=jax.ShapeDtypeStruct((B, D, L), F32),
        scratch_shapes=[pltpu.VMEM((HY_ORDER, cb, 2 * L), F32), pltpu.VMEM((HY_ORDER, cb, Q, 2 * P), F32)],
        compiler_params=_cparams("parallel", "arbitrary"),
        name="hyena_operator",
    )(u_t, u_t, u_t, a, a_b, wo_f, wo_b, jnp.asarray(deltas), t_f, t_b, par, g1, g1f, twr, twi, g2, g2c, g4)


def _trunk(x, p, cfg):
    B, L, D = x.shape
    tabs = _dft_tables(L, paired=(B % 2 == 0))
    for i in range(DEPTH):
        j = i // N_MIXERS
        if i % N_MIXERS == 0:
            a = hyena_filter_mlp(L, p["hy_f_w1"][j], p["hy_f_b1"][j], p["hy_f_w2"][j], p["hy_f_b2"][j],
                                 p["hy_f_w3"][j], p["hy_f_b3"][j], p["hy_f_freq"][j])
            u_t = norm_matmul_t(x, p["norm_mix_g"][i], p["hy_w_in_t"][j], p["hy_b_in"][j], bt=1024, bc=1024)
            z_t = hyena_operator(u_t, a, p["hy_f_wout"][j], p["hy_conv_w"][j], p["hy_conv_b"][j], p["hy_skip"][j],
                                 tabs, cb=cfg["hy_cb"])
            x = residual_matmul_t(x, z_t, p["hy_w_out"][j], p["hy_b_out"][j], bt=512)
        else:
            x2 = x.reshape(B * L, D)
            qkv = norm_matmul(x2, p["norm_mix_g"][i], p["at_w_qkv"][j], bm=1024, bn=1024)
            att = window_attention(qkv.reshape(B, L, -1), p["at_q_g"][j], p["at_k_g"][j], p["at_sink"][j],
                                   p["rel_bias"])
            x = residual_matmul(x2, att.reshape(B * L, -1), p["at_w_o"][j], bm=1024, bn=1024).reshape(B, L, D)
        x = ffn_block(x.reshape(B * L, D), p["norm_ffn_g"][i], p["ffn_w_gate_up"][i], p["ffn_w_down"][i],
                      bm=1024, bf=512).reshape(B, L, D)
    return x


def kernel(x_prompt, x_sample, norm_mix_g, norm_ffn_g, hy_w_in, hy_b_in, hy_conv_w, hy_conv_b, hy_f_w1, hy_f_b1,
           hy_f_w2, hy_f_b2, hy_f_w3, hy_f_b3, hy_f_wout, hy_f_freq, hy_skip, hy_w_out, hy_b_out, at_w_qkv, at_q_g,
           at_k_g, at_sink, at_w_o, rel_bias, ffn_w_gate_up, ffn_w_down):
    p = dict(
        norm_mix_g=norm_mix_g.astype(F32), norm_ffn_g=norm_ffn_g.astype(F32),
        hy_w_in_t=jnp.swapaxes(hy_w_in, 1, 2).astype(BF16), hy_b_in=hy_b_in,
        hy_conv_w=hy_conv_w, hy_conv_b=hy_conv_b,
        hy_f_w1=hy_f_w1, hy_f_b1=hy_f_b1, hy_f_w2=hy_f_w2, hy_f_b2=hy_f_b2, hy_f_w3=hy_f_w3, hy_f_b3=hy_f_b3,
        hy_f_wout=hy_f_wout, hy_f_freq=hy_f_freq, hy_skip=hy_skip,
        hy_w_out=hy_w_out.astype(BF16), hy_b_out=hy_b_out,
        at_w_qkv=at_w_qkv.astype(BF16), at_q_g=at_q_g, at_k_g=at_k_g, at_sink=at_sink,
        at_w_o=at_w_o.astype(BF16), rel_bias=rel_bias,
        ffn_w_gate_up=ffn_w_gate_up.astype(BF16), ffn_w_down=ffn_w_down.astype(BF16),
    )
    y_prompt = _trunk(x_prompt, p, dict(hy_cb=32))
    y_sample = _trunk(x_sample, p, dict(hy_cb=16))
    return (y_prompt, y_sample)
```

```python
import functools
import math

import jax
import jax.numpy as jnp
import numpy as np
from jax import lax
from jax.experimental import pallas as pl
from jax.experimental.pallas import tpu as pltpu

F32 = jnp.float32
BF16 = jnp.bfloat16

D_MODEL = 2048
DEPTH = 4
N_MIXERS = 2
HY_ORDER = 2
HY_EMB = 33
HY_BANDS = (HY_EMB - 1) // 2
HY_FILTER_W = 64
HY_FAST_PCT = 0.3
HY_SLOW_PCT = 1.5
HY_TARGET = 1e-2
N_HEADS = 16
HEAD_DIM = 128
N_KV = 4
GQA_G = N_HEADS // N_KV
WINDOW = 128
BLOCK = 128
N_BUCKETS = 32
MAX_DIST = 128
D_FF = -(-(8 * D_MODEL) // (3 * 256)) * 256
EPS = 1e-6
NEG = -1e30

V7X_LANES = 128
V7X_SUBLANES = 8
VMEM_LIMIT = 56 * 1024 * 1024

FFT_P = V7X_LANES
CH_GROUP = V7X_SUBLANES


def _cparams(*sem):
    return pltpu.CompilerParams(dimension_semantics=sem, vmem_limit_bytes=VMEM_LIMIT)


def _rms_bf16(x, g):
    ms = jnp.mean(x * x, axis=-1, keepdims=True)
    return (x * lax.rsqrt(ms + EPS) * g).astype(BF16)


def _qkv_proj_body(x_ref, g_ref, w_ref, gain_ref, norm_ref, o_ref, hn_ref):
    @pl.when(pl.program_id(1) == 0)
    def _():
        hn_ref[...] = _rms_bf16(x_ref[...], g_ref[...])

    acc = jnp.dot(hn_ref[...], w_ref[...], preferred_element_type=F32)
    for h in range(acc.shape[1] // HEAD_DIM):
        cols = slice(h * HEAD_DIM, (h + 1) * HEAD_DIM)
        t = acc[:, cols]
        ms = jnp.mean(t * t, axis=-1, keepdims=True)
        inv = jnp.where(norm_ref[:, cols] > 0.0, lax.rsqrt(ms + EPS), 1.0)
        o_ref[:, cols] = (t * inv * gain_ref[:, cols]).astype(o_ref.dtype)


def qkv_projection(x, g, w, q_g, k_g, *, bm, bn):
    T, K = x.shape
    N = w.shape[1]
    nq, nk = N_HEADS * HEAD_DIM, N_KV * HEAD_DIM
    gain = jnp.concatenate([jnp.tile(q_g.astype(F32), N_HEADS), jnp.tile(k_g.astype(F32), N_KV),
                            jnp.ones((nk,), F32)]).reshape(1, N)
    norm = jnp.concatenate([jnp.ones((nq + nk,), F32), jnp.zeros((nk,), F32)]).reshape(1, N)
    return pl.pallas_call(
        _qkv_proj_body,
        grid=(T // bm, N // bn),
        in_specs=[
            pl.BlockSpec((bm, K), lambda i, j: (i, 0)),
            pl.BlockSpec((1, K), lambda i, j: (0, 0)),
            pl.BlockSpec((K, bn), lambda i, j: (0, j)),
            pl.BlockSpec((1, bn), lambda i, j: (0, j)),
            pl.BlockSpec((1, bn), lambda i, j: (0, j)),
        ],
        out_specs=pl.BlockSpec((bm, bn), lambda i, j: (i, j)),
        out_shape=jax.ShapeDtypeStruct((T, N), BF16),
        scratch_shapes=[pltpu.VMEM((bm, K), BF16)],
        compiler_params=_cparams("parallel", "arbitrary"),
        name="qkv_projection",
    )(x, g.reshape(1, K), w, gain, norm)


def _norm_mm_t_body(x_ref, g_ref, w_ref, b_ref, o_ref, hn_ref):
    @pl.when(pl.program_id(2) == 0)
    def _():
        hn_ref[...] = _rms_bf16(x_ref[0], g_ref[...])

    acc = lax.dot_general(w_ref[...], hn_ref[...], (((1,), (1,)), ((), ())), preferred_element_type=F32)
    o_ref[0] = acc + b_ref[...]


def norm_matmul_t(x, g, w_t, bias, *, bt, bc):
    B, L, K = x.shape
    C = w_t.shape[0]
    return pl.pallas_call(
        _norm_mm_t_body,
        grid=(B, L // bt, C // bc),
        in_specs=[
            pl.BlockSpec((1, bt, K), lambda b, t, c: (b, t, 0)),
            pl.BlockSpec((1, K), lambda b, t, c: (0, 0)),
            pl.BlockSpec((bc, K), lambda b, t, c: (c, 0)),
            pl.BlockSpec((bc, 1), lambda b, t, c: (c, 0)),
        ],
        out_specs=pl.BlockSpec((1, bc, bt), lambda b, t, c: (b, c, t)),
        out_shape=jax.ShapeDtypeStruct((B, C, L), F32),
        scratch_shapes=[pltpu.VMEM((bt, K), BF16)],
        compiler_params=_cparams("parallel", "parallel", "arbitrary"),
        name="norm_matmul_t",
    )(x, g.reshape(1, K), w_t, bias.reshape(C, 1))


def _res_mm_body(a_ref, w_ref, x_ref, o_ref):
    o_ref[...] = x_ref[...] + jnp.dot(a_ref[...], w_ref[...], preferred_element_type=F32)


def residual_matmul(x, a, w, *, bm, bn):
    T, K = a.shape
    N = w.shape[1]
    return pl.pallas_call(
        _res_mm_body,
        grid=(T // bm, N // bn),
        in_specs=[
            pl.BlockSpec((bm, K), lambda i, j: (i, 0)),
            pl.BlockSpec((K, bn), lambda i, j: (0, j)),
            pl.BlockSpec((bm, bn), lambda i, j: (i, j)),
        ],
        out_specs=pl.BlockSpec((bm, bn), lambda i, j: (i, j)),
        out_shape=jax.ShapeDtypeStruct((T, N), F32),
        compiler_params=_cparams("parallel", "parallel"),
        name="residual_matmul",
    )(a, w, x)


def _res_mm_t_body(z_ref, w_ref, b_ref, x_ref, o_ref):
    z = z_ref[0].astype(BF16)
    y = lax.dot_general(z, w_ref[...], (((0,), (0,)), ((), ())), preferred_element_type=F32)
    o_ref[0] = x_ref[0] + y + b_ref[...]


def residual_matmul_t(x, z_t, w, bias, *, bt):
    B, K, L = z_t.shape
    N = w.shape[1]
    return pl.pallas_call(
        _res_mm_t_body,
        grid=(B, L // bt),
        in_specs=[
            pl.BlockSpec((1, K, bt), lambda b, t: (b, 0, t)),
            pl.BlockSpec((K, N), lambda b, t: (0, 0)),
            pl.BlockSpec((1, N), lambda b, t: (0, 0)),
            pl.BlockSpec((1, bt, N), lambda b, t: (b, t, 0)),
        ],
        out_specs=pl.BlockSpec((1, bt, N), lambda b, t: (b, t, 0)),
        out_shape=jax.ShapeDtypeStruct((B, L, N), F32),
        compiler_params=_cparams("parallel", "parallel"),
        name="residual_matmul_t",
    )(z_t, w, bias.reshape(1, N), x)


def _ffn_body(x_ref, g_ref, wg_ref, wu_ref, wd_ref, o_ref, hn_ref):
    @pl.when(pl.program_id(1) == 0)
    def _():
        x = x_ref[...]
        hn_ref[...] = _rms_bf16(x, g_ref[...])
        o_ref[...] = x

    h = hn_ref[...]
    gate = jnp.dot(h, wg_ref[...], preferred_element_type=F32)
    up = jnp.dot(h, wu_ref[...], preferred_element_type=F32)
    act = (gate * jax.nn.sigmoid(gate) * up).astype(BF16)
    o_ref[...] += jnp.dot(act, wd_ref[...], preferred_element_type=F32)


def ffn_block(x, g, w_gate_up, w_down, *, bm, bf):
    T, K = x.shape
    nf = D_FF // bf
    return pl.pallas_call(
        _ffn_body,
        grid=(T // bm, nf),
        in_specs=[
            pl.BlockSpec((bm, K), lambda i, f: (i, 0)),
            pl.BlockSpec((1, K), lambda i, f: (0, 0)),
            pl.BlockSpec((K, bf), lambda i, f: (0, f)),
            pl.BlockSpec((K, bf), lambda i, f: (0, f + nf)),
            pl.BlockSpec((bf, K), lambda i, f: (f, 0)),
        ],
        out_specs=pl.BlockSpec((bm, K), lambda i, f: (i, 0)),
        out_shape=jax.ShapeDtypeStruct((T, K), F32),
        scratch_shapes=[pltpu.VMEM((bm, K), BF16)],
        compiler_params=_cparams("parallel", "arbitrary"),
        name="ffn_block",
    )(x, g.reshape(1, K), w_gate_up, w_gate_up, w_down)


ATT_TQ = 4 * BLOCK
KV_W = N_KV * HEAD_DIM


def _band_structure():
    qi = np.arange(BLOCK)[:, None]
    ki = np.arange(3 * BLOCK)[None, :]
    rel = ki - BLOCK - qi
    nb = N_BUCKETS // 2
    max_exact = nb // 2
    n = np.abs(rel)
    large = max_exact + (np.log(np.maximum(n, 1) / max_exact) / math.log(MAX_DIST / max_exact)
                         * (nb - max_exact)).astype(np.int32)
    large = np.minimum(large, nb - 1)
    buckets = (rel > 0).astype(np.int32) * nb + np.where(n < max_exact, n, large).astype(np.int32)
    band = n <= WINDOW
    return buckets, band


def _attn_body(q_ref, kp_ref, kc_ref, kn_ref, vp_ref, vc_ref, vn_ref, bias_ref, sink_ref,
               o_ref, kbuf, vbuf):
    i = pl.program_id(1)
    last = pl.num_programs(1) - 1

    kbuf[0:BLOCK] = kp_ref[0]
    kbuf[BLOCK:BLOCK + ATT_TQ] = kc_ref[0]
    kbuf[BLOCK + ATT_TQ:] = kn_ref[0]
    ones = jnp.ones((ATT_TQ + 2 * BLOCK, HEAD_DIM), BF16)
    for g in range(N_KV):
        src = slice(g * HEAD_DIM, (g + 1) * HEAD_DIM)
        dst = slice(2 * g * HEAD_DIM, (2 * g + 1) * HEAD_DIM)
        vbuf[0:BLOCK, dst] = vp_ref[0, :, src]
        vbuf[BLOCK:BLOCK + ATT_TQ, dst] = vc_ref[0, :, src]
        vbuf[BLOCK + ATT_TQ:, dst] = vn_ref[0, :, src]
        vbuf[:, (2 * g + 1) * HEAD_DIM:(2 * g + 2) * HEAD_DIM] = ones

    lane = lax.broadcasted_iota(jnp.int32, (1, 3 * BLOCK), 1)
    first_edge = jnp.where((lane < BLOCK) & (i == 0), NEG, 0.0).astype(F32)
    last_edge = jnp.where((lane >= 2 * BLOCK) & (i == last), NEG, 0.0).astype(F32)
    exp2_scale = HEAD_DIM ** -0.5 * math.log2(math.e)

    n_sub = ATT_TQ // BLOCK
    groups = range(N_KV)

    def scores(j):
        r0 = j * BLOCK
        s = []
        for g in groups:
            qs = jnp.concatenate(
                [q_ref[0, r0:r0 + BLOCK, (GQA_G * g + h) * HEAD_DIM:(GQA_G * g + h + 1) * HEAD_DIM]
                 for h in range(GQA_G)], axis=0)
            kw = kbuf[r0:r0 + 3 * BLOCK, g * HEAD_DIM:(g + 1) * HEAD_DIM]
            sg = lax.dot_general(qs, kw, (((1,), (1,)), ((), ())), preferred_element_type=F32)
            sg = sg + bias_ref[g]
            if j == 0:
                sg = sg + first_edge
            if j == n_sub - 1:
                sg = sg + last_edge
            s.append(sg)
        return s

    def softmax_numerators(s):
        sk = [sink_ref[g][:, 0:1] for g in groups]
        m = [jnp.maximum(jnp.max(s[g], axis=-1, keepdims=True), sk[g]) for g in groups]
        p = [jnp.exp2((s[g] - m[g]) * exp2_scale).astype(BF16) for g in groups]
        sink_p = [jnp.exp2((sk[g] - m[g]) * exp2_scale) for g in groups]
        return p, sink_p

    def outputs(j, p, sink_p):
        r0 = j * BLOCK
        for g in groups:
            vw = vbuf[r0:r0 + 3 * BLOCK, 2 * g * HEAD_DIM:(2 * g + 2) * HEAD_DIM]
            pv = jnp.dot(p[g], vw, preferred_element_type=F32)
            o = pv[:, :HEAD_DIM] / (pv[:, HEAD_DIM:] + sink_p[g])
            for h in range(GQA_G):
                c0 = (GQA_G * g + h) * HEAD_DIM
                o_ref[0, r0:r0 + BLOCK, c0:c0 + HEAD_DIM] = o[h * BLOCK:(h + 1) * BLOCK].astype(o_ref.dtype)

    s_q, p_q = {}, {}
    for step in range(n_sub + 2):
        if step >= 2:
            outputs(step - 2, *p_q.pop(step - 2))
        if 1 <= step <= n_sub:
            p_q[step - 1] = softmax_numerators(s_q.pop(step - 1))
        if step < n_sub:
            s_q[step] = scores(step)


def window_attention(qkv, sink, rel_bias):
    B, L, _ = qkv.shape
    nq = N_HEADS * HEAD_DIM
    sub = ATT_TQ // BLOCK
    nblk = L // BLOCK
    kcol = nq // KV_W
    vcol = kcol + 1

    buckets, band = _band_structure()
    bias = rel_bias[buckets].astype(F32)
    inv_scale = HEAD_DIM ** 0.5
    bias = jnp.where(band[:, :, None], bias * inv_scale, NEG)
    bias = jnp.transpose(bias, (2, 0, 1)).reshape(N_KV, GQA_G * BLOCK, 3 * BLOCK)
    sink_rows = jnp.broadcast_to((sink.astype(F32) * inv_scale).reshape(N_KV, GQA_G, 1, 1),
                                 (N_KV, GQA_G, BLOCK, V7X_LANES)).reshape(N_KV, GQA_G * BLOCK, V7X_LANES)

    return pl.pallas_call(
        _attn_body,
        grid=(B, L // ATT_TQ),
        in_specs=[
            pl.BlockSpec((1, ATT_TQ, nq), lambda b, i: (b, i, 0)),
            pl.BlockSpec((1, BLOCK, KV_W), lambda b, i: (b, jnp.maximum(sub * i - 1, 0), kcol)),
            pl.BlockSpec((1, ATT_TQ, KV_W), lambda b, i: (b, i, kcol)),
            pl.BlockSpec((1, BLOCK, KV_W), lambda b, i: (b, jnp.minimum(sub * i + sub, nblk - 1), kcol)),
            pl.BlockSpec((1, BLOCK, KV_W), lambda b, i: (b, jnp.maximum(sub * i - 1, 0), vcol)),
            pl.BlockSpec((1, ATT_TQ, KV_W), lambda b, i: (b, i, vcol)),
            pl.BlockSpec((1, BLOCK, KV_W), lambda b, i: (b, jnp.minimum(sub * i + sub, nblk - 1), vcol)),
            pl.BlockSpec((N_KV, GQA_G * BLOCK, 3 * BLOCK), lambda b, i: (0, 0, 0)),
            pl.BlockSpec((N_KV, GQA_G * BLOCK, V7X_LANES), lambda b, i: (0, 0, 0)),
        ],
        out_specs=pl.BlockSpec((1, ATT_TQ, nq), lambda b, i: (b, i, 0)),
        out_shape=jax.ShapeDtypeStruct((B, L, nq), BF16),
        scratch_shapes=[pltpu.VMEM((ATT_TQ + 2 * BLOCK, KV_W), BF16),
                        pltpu.VMEM((ATT_TQ + 2 * BLOCK, 2 * KV_W), BF16)],
        compiler_params=_cparams("parallel", "parallel"),
        name="window_attention",
    )(qkv, qkv, qkv, qkv, qkv, qkv, qkv, bias, sink_rows)


FEAT_PAD = V7X_LANES


def _filter_mlp_body(feat_ref, w1_ref, b1_ref, w2_ref, b2_ref, w3_ref, b3_ref, fr_ref, a_ref):
    fr = fr_ref[...]
    a = jnp.sin(fr * (jnp.dot(feat_ref[...], w1_ref[...], preferred_element_type=F32) + b1_ref[...]))
    a = jnp.sin(fr * (jnp.dot(a.astype(BF16), w2_ref[...], preferred_element_type=F32) + b2_ref[...]))
    a = jnp.sin(fr * (jnp.dot(a.astype(BF16), w3_ref[...], preferred_element_type=F32) + b3_ref[...]))
    a_ref[...] = a.astype(BF16)


def hyena_filter_mlp(L, f_w1, f_b1, f_w2, f_b2, f_w3, f_b3, f_freq, *, bt=1024):
    t = jnp.linspace(0.0, 1.0, L, dtype=F32)[:, None]
    w = 2.0 * math.pi * jnp.arange(L, dtype=F32) / L
    f = jnp.linspace(1e-4, HY_BANDS - 1, HY_BANDS, dtype=F32)
    ang = w[:, None] * f[None, :]
    feats = jnp.concatenate([t, jnp.cos(ang), -jnp.sin(ang), jnp.zeros((L, FEAT_PAD - HY_EMB), F32)], axis=-1)
    w1 = jnp.concatenate([f_w1.astype(F32), jnp.zeros((FEAT_PAD - HY_EMB, HY_FILTER_W), F32)], axis=0)
    W = HY_FILTER_W
    const = lambda r, c: pl.BlockSpec((r, c), lambda i: (0, 0))
    return pl.pallas_call(
        _filter_mlp_body,
        grid=(L // bt,),
        in_specs=[pl.BlockSpec((bt, FEAT_PAD), lambda i: (i, 0)), const(FEAT_PAD, W), const(1, W), const(W, W),
                  const(1, W), const(W, W), const(1, W), const(1, W)],
        out_specs=pl.BlockSpec((bt, W), lambda i: (i, 0)),
        out_shape=jax.ShapeDtypeStruct((L, W), BF16),
        compiler_params=_cparams("parallel"),
        name="hyena_filter_mlp",
    )(feats.astype(BF16), w1.astype(BF16), f_b1.reshape(1, W), f_w2.astype(BF16), f_b2.reshape(1, W),
      f_w3.astype(BF16), f_b3.reshape(1, W), f_freq.astype(F32).reshape(1, W))


def _dft_tables(L, paired):
    N = 2 * L
    P = FFT_P
    Q = N // P
    S = L // P
    b = np.arange(Q, dtype=np.float64)
    fq = np.exp(-2j * np.pi * np.outer(b, np.arange(S)) / Q)
    fp =np.exp(-2j * np.pi * np.outer(np.arange(P), np.arange(P)) / P)
    tw = np.exp(-2j * np.pi * np.outer(b, np.arange(P)) / N)
    ci = np.conj(fq).T / N

    def stack(c):
        return np.block([[c.real, -c.imag], [c.imag, c.real]])

    if paired:
        g1 = stack(fq)
        g4 = stack(ci)
    else:
        g1 = np.concatenate([fq.real, fq.imag], axis=0)
        g4 = np.concatenate([ci.real, -ci.imag], axis=1)
    fq_full = np.exp(-2j * np.pi * np.outer(b, np.arange(Q)) / Q)
    g1_full = np.concatenate([fq_full.real, fq_full.imag], axis=0)
    g2 =np.block([[fp.real, fp.imag], [-fp.imag, fp.real]])
    g2c = np.block([[fp.real, -fp.imag], [fp.imag, fp.real]])
    twr = np.tile(tw.real, (1, 2))
    twi = np.tile(tw.imag, (1, 2))
    f32 = lambda a: np.ascontiguousarray(a, dtype=np.float32)
    return dict(g1=f32(g1), g4=f32(g4), g1_full=f32(g1_full), g2=f32(g2), g2c=f32(g2c),
                twr=f32(twr), twi=f32(twi), Q=Q, S=S, N=N)


def _to_tiles(x, n_tiles):
    chunks = [x[:, s * FFT_P:(s + 1) * FFT_P] for s in range(n_tiles)]
    return jnp.swapaxes(jnp.stack(chunks, axis=0), 0, 1)


def _from_tiles(x):
    y = jnp.swapaxes(x, 0, 1)
    return jnp.concatenate([y[s] for s in range(y.shape[0])], axis=1)


def _fwd_fft(re_tiles, im_tiles, g1, twr, twi, g2):
    G = len(re_tiles)
    Q = twr.shape[0]
    P = FFT_P
    rows = []
    for c in range(0, G, 2):
        top = jnp.concatenate([re_tiles[c], re_tiles[c + 1]], axis=1)
        if im_tiles is None:
            rhs = top
        else:
            rhs = jnp.concatenate([top, jnp.concatenate([im_tiles[c], im_tiles[c + 1]], axis=1)], axis=0)
        y = jnp.dot(g1, rhs.astype(BF16), preferred_element_type=F32)
        yr, yi = y[:Q], y[Q:]
        zr = yr * twr - yi * twi
        zi = yr * twi + yi * twr
        rows.append(jnp.concatenate([zr[:, :P], zi[:, :P]], axis=1))
        rows.append(jnp.concatenate([zr[:, P:], zi[:, P:]], axis=1))
    lhs = jnp.concatenate(rows, axis=0).astype(BF16)
    return jnp.dot(lhs, g2, preferred_element_type=F32)


def _inv_fft(spec, g2c, twr, twi, g4, want_imag):
    Q = twr.shape[0]
    P = FFT_P
    G = spec.shape[0] // Q
    S = g4.shape[0] // 2 if want_imag else g4.shape[0]
    y = jnp.dot(spec.astype(BF16), g2c, preferred_element_type=F32)
    out_re, out_im = [], []
    for c in range(0, G, 2):
        ya = y[c * Q:(c + 1) * Q]
        yb = y[(c + 1) * Q:(c + 2) * Q]
        yr = jnp.concatenate([ya[:, :P], yb[:, :P]], axis=1)
        yi = jnp.concatenate([ya[:, P:], yb[:, P:]], axis=1)
        zr = yr * twr + yi * twi
        zi = yi * twr - yr * twi
        rhs = jnp.concatenate([zr, zi], axis=0).astype(BF16)
        o = jnp.dot(g4, rhs, preferred_element_type=F32)
        out_re += [o[:S, :P], o[:S, P:]]
        if want_imag:
            out_im += [o[S:, :P], o[S:, P:]]
    return out_re, out_im


def _hyena_body(v_ref, x1_ref, x2_ref, af_ref, ab_ref, wof_ref, wob_ref, delta_ref, tf_ref, tb_ref, par_ref,
                g1_ref, g1f_ref, twr_ref, twi_ref, g2_ref, g2c_ref, g4_ref, o_ref, taps_ref, kf_ref,
                *, S, Q, paired):
    g1, twr, twi = g1_ref[...], twr_ref[...], twi_ref[...]
    g2, g2c, g4 = g2_ref[...], g2c_ref[...], g4_ref[...]
    nb = 2 if paired else 1
    cb = v_ref.shape[1]
    n_groups = cb // CH_GROUP
    P = FFT_P
    G = CH_GROUP

    lane = lax.broadcasted_iota(jnp.int32, (G, S, P), 2)
    row = lax.broadcasted_iota(jnp.int32, (G, S, P), 1)

    @pl.when(pl.program_id(1) == 0)
    def _():
        g1f = g1f_ref[...]
        L = S * P
        delta = delta_ref[...]
        halves = ((wof_ref, af_ref, jnp.exp(-(delta * tf_ref[...]))),
                  (wob_ref, ab_ref, jnp.where(lax.broadcasted_iota(jnp.int32, (cb, L), 1) == 0, 0.0,
                                              jnp.exp(-(delta * tb_ref[...])))))
        for half, (wo_ref, a_ref, decay) in enumerate(halves):
            wo = wo_ref[...].reshape(HY_ORDER * cb, HY_FILTER_W)
            h = lax.dot_general(wo, a_ref[...], (((1,), (1,)), ((), ())), preferred_element_type=F32)
            for o in range(HY_ORDER):
                taps_ref[o, :, half * L:(half + 1) * L] = h[o * cb:(o + 1) * cb] * decay

        def filter_group(gi, carry):
            c0 = pl.multiple_of(gi * G, G)
            for o in range(HY_ORDER):
                k = taps_ref[o, pl.ds(c0, G), :]
                norm = jnp.sum(jnp.abs(k), axis=-1, keepdims=True)
                tiles = _to_tiles(k, Q)
                spec = _fwd_fft([tiles[c] for c in range(G)], None, g1f, twr, twi, g2)
                kf_ref[o, pl.ds(c0, G)] = spec.reshape(G, Q, 2 * P) * (1.0 / norm)[:, :, None]
            return carry

        lax.fori_loop(0, n_groups, filter_group, 0)

    def short_conv(u, w0, w1, w2, bias):
        lr = pltpu.roll(u, 1, axis=2)
        prev = jnp.where(lane == 0, pltpu.roll(lr, 1, axis=1), lr)
        prev = jnp.where((lane == 0) & (row == 0), 0.0, prev)
        ll = pltpu.roll(u, P - 1, axis=2)
        nxt = jnp.where(lane == P - 1, pltpu.roll(ll, S - 1, axis=1), ll)
        nxt = jnp.where((lane == P - 1) & (row == S - 1), 0.0, nxt)
        return w0 * prev + w1 * u + w2 * nxt + bias

    def group(gi, carry):
        c0 = pl.multiple_of(gi * G, G)
        pw = lambda idx: par_ref[idx, pl.ds(c0, G)]

        def load(ref, b, base):
            u = _to_tiles(ref[b, pl.ds(c0, G), :], S)
            return short_conv(u, pw(base), pw(base + 1), pw(base + 2), pw(base + 3))

        z = [load(v_ref, b, 0) for b in range(nb)]
        gates = [[load(x1_ref, b, 4) for b in range(nb)], [load(x2_ref, b, 8) for b in range(nb)]]
        for o in range(HY_ORDER):
            re = [z[0][c] for c in range(G)]
            im = [z[1][c] for c in range(G)] if paired else None
            spec = _fwd_fft(re, im, g1, twr, twi, g2)
            kf = kf_ref[o, pl.ds(c0, G)].reshape(G * Q, 2 * P)
            xr, xi = spec[:, :P], spec[:, P:]
            kr, ki = kf[:, :P], kf[:, P:]
            prod = jnp.concatenate([xr * kr - xi * ki, xr * ki + xi * kr], axis=1)
            out_re, out_im = _inv_fft(prod, g2c, twr, twi, g4, paired)
            skip = pw(12 + o)
            conv = [jnp.stack(out_re, axis=0)] + ([jnp.stack(out_im, axis=0)] if paired else [])
            z = [gates[o][b] * (conv[b] + skip * z[b]) for b in range(nb)]
        for b in range(nb):
            o_ref[b, pl.ds(c0, G), :] = _from_tiles(z[b])
        return carry

    lax.fori_loop(0, n_groups, group, 0)


def hyena_operator(u_t, a, f_wout, conv_w, conv_b, skip, tabs, *, cb):
    B, C3, L = u_t.shape
    D = C3 // 3
    Q, S = tabs["Q"], tabs["S"]
    P = FFT_P
    W = HY_FILTER_W
    paired = B % 2 == 0
    nb = 2 if paired else 1
    wo = f_wout.T.reshape(HY_ORDER, 2, D, W).astype(BF16)
    wo_f, wo_b = wo[:, 0], wo[:, 1]
    deltas = np.abs(np.linspace(math.log(HY_TARGET) / HY_SLOW_PCT,
                                math.log(HY_TARGET) / HY_FAST_PCT, D)).astype(np.float32).reshape(D, 1)
    t_f = jnp.linspace(0.0, 1.0, L, dtype=F32).reshape(1, L)
    a_b = jnp.roll(jnp.flip(a, axis=0), 1, axis=0)
    t_b = jnp.roll(jnp.flip(t_f, axis=1), 1, axis=1)
    cols = []
    for part in range(3):
        sl = slice(part * D, (part + 1) * D)
        cols += [conv_w[0, sl], conv_w[1, sl], conv_w[2, sl], conv_b[sl]]
    cols += [skip[0], skip[1]]
    n_par = len(cols)
    par = jnp.broadcast_to(jnp.stack(cols, axis=0).astype(F32)[:, :, None, None], (n_par, D, 1, P))

    bf = lambda name: jnp.asarray(tabs[name]).astype(BF16)
    g1, g1f, g2, g2c, g4 = bf("g1"), bf("g1_full"), bf("g2"), bf("g2c"), bf("g4")
    twr, twi = jnp.asarray(tabs["twr"]), jnp.asarray(tabs["twi"])
    const = lambda arr: pl.BlockSpec(arr.shape, lambda c, p: (0,) * arr.ndim)
    ncb = D // cb
    return pl.pallas_call(
        functools.partial(_hyena_body, S=S, Q=Q, paired=paired),
        grid=(ncb, B // nb),
        in_specs=[
            pl.BlockSpec((nb, cb, L), lambda c, p: (p, c, 0)),
            pl.BlockSpec((nb, cb, L), lambda c, p: (p, c + ncb, 0)),
            pl.BlockSpec((nb, cb, L), lambda c, p: (p, c + 2 * ncb, 0)),
            const(a), const(a_b),
            pl.BlockSpec((HY_ORDER, cb, W), lambda c, p: (0, c, 0)),
            pl.BlockSpec((HY_ORDER, cb, W), lambda c, p: (0, c, 0)),
            pl.BlockSpec((cb, 1), lambda c, p: (c, 0)),
            const(t_f), const(t_b),
            pl.BlockSpec((n_par, cb, 1, P), lambda c, p: (0, c, 0, 0)),
            const(g1), const(g1f), const(twr), const(twi), const(g2), const(g2c), const(g4),
        ],
        out_specs=pl.BlockSpec((nb, cb, L), lambda c, p: (p, c, 0)),
        out_shape=jax.ShapeDtypeStruct((B, D, L), F32),
        scratch_shapes=[pltpu.VMEM((HY_ORDER, cb, 2 * L), F32), pltpu.VMEM((HY_ORDER, cb, Q, 2 * P), F32)],
        compiler_params=_cparams("parallel", "arbitrary"),
        name="hyena_operator",
    )(u_t, u_t, u_t, a, a_b, wo_f, wo_b, jnp.asarray(deltas), t_f, t_b, par, g1, g1f, twr, twi, g2, g2c, g4)


def _trunk(x, p, cfg):
    B, L, D = x.shape
    tabs = _dft_tables(L, paired=(B % 2 == 0))
    for i in range(DEPTH):
        j = i // N_MIXERS
        if i % N_MIXERS == 0:
            a = hyena_filter_mlp(L, p["hy_f_w1"][j], p["hy_f_b1"][j], p["hy_f_w2"][j], p["hy_f_b2"][j],
                                 p["hy_f_w3"][j], p["hy_f_b3"][j], p["hy_f_freq"][j])
            u_t = norm_matmul_t(x, p["norm_mix_g"][i], p["hy_w_in_t"][j], p["hy_b_in"][j], bt=1024, bc=1024)
            z_t = hyena_operator(u_t, a, p["hy_f_wout"][j], p["hy_conv_w"][j], p["hy_conv_b"][j], p["hy_skip"][j],
                                 tabs, cb=cfg["hy_cb"])
            x = residual_matmul_t(x, z_t, p["hy_w_out"][j], p["hy_b_out"][j], bt=512)
        else:
            x2 = x.reshape(B * L, D)
            qkv = qkv_projection(x2, p["norm_mix_g"][i], p["at_w_qkv"][j], p["at_q_g"][j], p["at_k_g"][j],
                                 bm=1024, bn=1024)
            att = window_attention(qkv.reshape(B, L, -1), p["at_sink"][j], p["rel_bias"])
            x = residual_matmul(x2, att.reshape(B * L, -1), p["at_w_o"][j], bm=1024, bn=1024).reshape(B, L, D)
        x = ffn_block(x.reshape(B * L, D), p["norm_ffn_g"][i], p["ffn_w_gate_up"][i], p["ffn_w_down"][i],
                      bm=1024, bf=512).reshape(B, L, D)
    return x


def kernel(x_prompt, x_sample, norm_mix_g, norm_ffn_g, hy_w_in, hy_b_in, hy_conv_w, hy_conv_b, hy_f_w1, hy_f_b1,
           hy_f_w2, hy_f_b2, hy_f_w3, hy_f_b3, hy_f_wout, hy_f_freq, hy_skip, hy_w_out, hy_b_out, at_w_qkv, at_q_g,
           at_k_g, at_sink, at_w_o, rel_bias, ffn_w_gate_up, ffn_w_down):
    p = dict(
        norm_mix_g=norm_mix_g.astype(F32), norm_ffn_g=norm_ffn_g.astype(F32),
        hy_w_in_t=jnp.swapaxes(hy_w_in, 1, 2).astype(BF16), hy_b_in=hy_b_in,
        hy_conv_w=hy_conv_w, hy_conv_b=hy_conv_b,
        hy_f_w1=hy_f_w1, hy_f_b1=hy_f_b1, hy_f_w2=hy_f_w2, hy_f_b2=hy_f_b2, hy_f_w3=hy_f_w3, hy_f_b3=hy_f_b3,
        hy_f_wout=hy_f_wout, hy_f_freq=hy_f_freq, hy_skip=hy_skip,
        hy_w_out=hy_w_out.astype(BF16), hy_b_out=hy_b_out,
        at_w_qkv=at_w_qkv.astype(BF16), at_q_g=at_q_g, at_k_g=at_k_g, at_sink=at_sink,
        at_w_o=at_w_o.astype(BF16), rel_bias=rel_bias,
        ffn_w_gate_up=ffn_w_gate_up.astype(BF16), ffn_w_down=ffn_w_down.astype(BF16),
    )
    y_prompt = _trunk(x_prompt, p, dict(hy_cb=32))
    y_sample = _trunk(x_sample, p, dict(hy_cb=16))
    return (y_prompt, y_sample)
```

```python
import functools
import math

import jax
import jax.numpy as jnp
import numpy as np
from jax import lax
from jax.experimental import pallas as pl
from jax.experimental.pallas import tpu as pltpu

F32 = jnp.float32
BF16 = jnp.bfloat16

D_MODEL = 2048
DEPTH = 4
N_MIXERS = 2
HY_ORDER = 2
HY_EMB = 33
HY_BANDS = (HY_EMB - 1) // 2
HY_FILTER_W = 64
HY_FAST_PCT = 0.3
HY_SLOW_PCT = 1.5
HY_TARGET = 1e-2
N_HEADS = 16
HEAD_DIM = 128
N_KV = 4
GQA_G = N_HEADS // N_KV
WINDOW = 128
BLOCK = 128
N_BUCKETS = 32
MAX_DIST = 128
D_FF = -(-(8 * D_MODEL) // (3 * 256)) * 256
EPS = 1e-6
NEG = -1e30

V7X_LANES = 128
V7X_SUBLANES = 8
VMEM_LIMIT = 56 * 1024 * 1024

FFT_P = V7X_LANES
CH_GROUP = V7X_SUBLANES


def _cparams(*sem):
    return pltpu.CompilerParams(dimension_semantics=sem, vmem_limit_bytes=VMEM_LIMIT)


def _rms_bf16(x, g):
    ms = jnp.mean(x * x, axis=-1, keepdims=True)
    return (x * lax.rsqrt(ms + EPS) * g).astype(BF16)


def _qkv_proj_body(x_ref, g_ref, w_ref, qg_ref, kg_ref, o_ref):
    hn = _rms_bf16(x_ref[...], g_ref[...])
    nq, nk = N_HEADS * HEAD_DIM, N_KV * HEAD_DIM
    pair = 2 * HEAD_DIM
    for c0 in range(0, nq + nk, pair):
        acc = jnp.dot(hn, w_ref[:, c0:c0 + pair], preferred_element_type=F32)
        gain = qg_ref[...] if c0 < nq else kg_ref[...]
        for c in (0, HEAD_DIM):
            t = acc[:, c:c + HEAD_DIM]
            ms = jnp.mean(t * t, axis=-1, keepdims=True)
            o_ref[:, c0 + c:c0 + c + HEAD_DIM] = (t * lax.rsqrt(ms + EPS) * gain).astype(o_ref.dtype)
    o_ref[:, nq + nk:] = jnp.dot(hn, w_ref[:, nq + nk:], preferred_element_type=F32).astype(o_ref.dtype)


def qkv_projection(x, g, w, q_g, k_g, *, bm):
    T, K = x.shape
    N = w.shape[1]
    return pl.pallas_call(
        _qkv_proj_body,
        grid=(T // bm,),
        in_specs=[
            pl.BlockSpec((bm, K), lambda i: (i, 0)),
            pl.BlockSpec((1, K), lambda i: (0, 0)),
            pl.BlockSpec((K, N), lambda i: (0, 0), pipeline_mode=pl.Buffered(1)),
            pl.BlockSpec((1, HEAD_DIM), lambda i: (0, 0)),
            pl.BlockSpec((1, HEAD_DIM), lambda i: (0, 0)),
        ],
        out_specs=pl.BlockSpec((bm, N), lambda i: (i, 0)),
        out_shape=jax.ShapeDtypeStruct((T, N), BF16),
        compiler_params=_cparams("parallel"),
        name="qkv_projection",
    )(x, g.reshape(1, K), w, q_g.astype(F32).reshape(1, HEAD_DIM), k_g.astype(F32).reshape(1, HEAD_DIM))


def _norm_mm_t_body(x_ref, g_ref, w_ref, b_ref, o_ref, hn_ref):
    @pl.when(pl.program_id(2) == 0)
    def _():
        hn_ref[...] = _rms_bf16(x_ref[0], g_ref[...])

    acc = lax.dot_general(w_ref[...], hn_ref[...], (((1,), (1,)), ((), ())), preferred_element_type=F32)
    o_ref[0] = acc + b_ref[...]


def norm_matmul_t(x, g, w_t, bias, *, bt, bc):
    B, L, K = x.shape
    C = w_t.shape[0]
    return pl.pallas_call(
        _norm_mm_t_body,
        grid=(B, L // bt, C // bc),
        in_specs=[
            pl.BlockSpec((1, bt, K), lambda b, t, c: (b, t, 0)),
            pl.BlockSpec((1, K), lambda b, t, c: (0, 0)),
            pl.BlockSpec((bc, K), lambda b, t, c: (c, 0)),
            pl.BlockSpec((bc, 1), lambda b, t, c: (c, 0)),
        ],
        out_specs=pl.BlockSpec((1, bc, bt), lambda b, t, c: (b, c, t)),
        out_shape=jax.ShapeDtypeStruct((B, C, L), F32),
        scratch_shapes=[pltpu.VMEM((bt, K), BF16)],
        compiler_params=_cparams("parallel", "parallel", "arbitrary"),
        name="norm_matmul_t",
    )(x, g.reshape(1, K), w_t, bias.reshape(C, 1))


def _res_mm_body(a_ref, w_ref, x_ref, o_ref):
    o_ref[...] = x_ref[...] + jnp.dot(a_ref[...], w_ref[...], preferred_element_type=F32)


def residual_matmul(x, a, w, *, bm):
    T, K = a.shape
    N = w.shape[1]
    return pl.pallas_call(
        _res_mm_body,
        grid=(T // bm,),
        in_specs=[
            pl.BlockSpec((bm, K), lambda i: (i, 0)),
            pl.BlockSpec((K, N), lambda i: (0, 0), pipeline_mode=pl.Buffered(1)),
            pl.BlockSpec((bm, N), lambda i: (i, 0)),
        ],
        out_specs=pl.BlockSpec((bm, N), lambda i: (i, 0)),
        out_shape=jax.ShapeDtypeStruct((T, N), F32),
        compiler_params=_cparams("parallel"),
        name="residual_matmul",
    )(a, w, x)


def _res_mm_t_body(z_ref, w_ref, b_ref, x_ref, o_ref):
    z = z_ref[0].astype(BF16)
    y = lax.dot_general(z, w_ref[...], (((0,), (0,)), ((), ())), preferred_element_type=F32)
    o_ref[0] = x_ref[0] + y + b_ref[...]


def residual_matmul_t(x, z_t, w, bias, *, bt):
    B, K, L = z_t.shape
    N = w.shape[1]
    return pl.pallas_call(
        _res_mm_t_body,
        grid=(B, L // bt),
        in_specs=[
            pl.BlockSpec((1, K, bt), lambda b, t: (b, 0, t)),
            pl.BlockSpec((K, N), lambda b, t: (0, 0), pipeline_mode=pl.Buffered(1)),
            pl.BlockSpec((1, N), lambda b, t: (0, 0)),
            pl.BlockSpec((1, bt, N), lambda b, t: (b, t, 0)),
        ],
        out_specs=pl.BlockSpec((1, bt, N), lambda b, t: (b, t, 0)),
        out_shape=jax.ShapeDtypeStruct((B, L, N), F32),
        compiler_params=_cparams("parallel", "parallel"),
        name="residual_matmul_t",
    )(z_t, w, bias.reshape(1, N), x)


def _ffn_body(x_ref, g_ref, wg_ref, wu_ref, wd_ref, o_ref, hn_ref):
    @pl.when(pl.program_id(1) == 0)
    def _():
        x = x_ref[...]
        hn_ref[...] = _rms_bf16(x, g_ref[...])
        o_ref[...] = x

    h = hn_ref[...]
    gate = jnp.dot(h, wg_ref[...], preferred_element_type=F32)
    up = jnp.dot(h, wu_ref[...], preferred_element_type=F32)
    act = (gate * jax.nn.sigmoid(gate) * up).astype(BF16)
    o_ref[...] += jnp.dot(act, wd_ref[...], preferred_element_type=F32)


def ffn_block(x, g, w_gate_up, w_down, *, bm, bf):
    T, K = x.shape
    nf = D_FF // bf
    return pl.pallas_call(
        _ffn_body,
        grid=(T // bm, nf),
        in_specs=[
            pl.BlockSpec((bm, K), lambda i, f: (i, 0)),
            pl.BlockSpec((1, K), lambda i, f: (0, 0)),
            pl.BlockSpec((K, bf), lambda i, f: (0, f)),
            pl.BlockSpec((K, bf), lambda i, f: (0, f + nf)),
            pl.BlockSpec((bf, K), lambda i, f: (f, 0)),
        ],
        out_specs=pl.BlockSpec((bm, K), lambda i, f: (i, 0)),
        out_shape=jax.ShapeDtypeStruct((T, K), F32),
        scratch_shapes=[pltpu.VMEM((bm, K), BF16)],
        compiler_params=_cparams("parallel", "arbitrary"),
        name="ffn_block",
    )(x, g.reshape(1, K), w_gate_up, w_gate_up, w_down)


ATT_TQ = 4 * BLOCK
KV_W = N_KV * HEAD_DIM


def _band_structure():
    qi = np.arange(BLOCK)[:, None]
    ki = np.arange(3 * BLOCK)[None, :]
    rel = ki - BLOCK - qi
    nb = N_BUCKETS // 2
    max_exact = nb // 2
    n = np.abs(rel)
    large = max_exact + (np.log(np.maximum(n, 1) / max_exact) / math.log(MAX_DIST / max_exact)
                         * (nb - max_exact)).astype(np.int32)
    large = np.minimum(large, nb - 1)
    buckets = (rel > 0).astype(np.int32) * nb + np.where(n < max_exact, n, large).astype(np.int32)
    band = n <= WINDOW
    return buckets, band


def _attn_body(q_ref, kp_ref, kc_ref, kn_ref, vp_ref, vc_ref, vn_ref, bias_ref, sink_ref,
               o_ref, kbuf, vbuf):
    i = pl.program_id(1)
    last = pl.num_programs(1) - 1

    kbuf[0:BLOCK] = kp_ref[0]
    kbuf[BLOCK:BLOCK + ATT_TQ] = kc_ref[0]
    kbuf[BLOCK + ATT_TQ:] = kn_ref[0]
    ones = jnp.ones((ATT_TQ + 2 * BLOCK, HEAD_DIM), BF16)
    for g in range(N_KV):
        src = slice(g * HEAD_DIM, (g + 1) * HEAD_DIM)
        dst = slice(2 * g * HEAD_DIM, (2 * g + 1) * HEAD_DIM)
        vbuf[0:BLOCK, dst] = vp_ref[0, :, src]
        vbuf[BLOCK:BLOCK + ATT_TQ, dst] = vc_ref[0, :, src]
        vbuf[BLOCK + ATT_TQ:, dst] = vn_ref[0, :, src]
        vbuf[:, (2 * g + 1) * HEAD_DIM:(2 * g + 2) * HEAD_DIM] = ones

    lane = lax.broadcasted_iota(jnp.int32, (1, 3 * BLOCK), 1)
    first_edge = jnp.where((lane < BLOCK) & (i == 0), NEG, 0.0).astype(F32)
    last_edge = jnp.where((lane >= 2 * BLOCK) & (i == last), NEG, 0.0).astype(F32)
    exp2_scale = HEAD_DIM ** -0.5 * math.log2(math.e)

    n_sub = ATT_TQ // BLOCK
    groups = range(N_KV)

    def scores(j):
        r0 = j * BLOCK
        s = []
        for g in groups:
            qs = jnp.concatenate(
                [q_ref[0, r0:r0 + BLOCK, (GQA_G * g + h) * HEAD_DIM:(GQA_G * g + h + 1) * HEAD_DIM]
                 for h in range(GQA_G)], axis=0)
            kw = kbuf[r0:r0 + 3 * BLOCK, g * HEAD_DIM:(g + 1) * HEAD_DIM]
            sg = lax.dot_general(qs, kw, (((1,), (1,)), ((), ())), preferred_element_type=F32)
            sg = sg + bias_ref[g]
            if j == 0:
                sg = sg + first_edge
            if j == n_sub - 1:
                sg = sg + last_edge
            s.append(sg)
        return s

    def softmax_numerators(s):
        sk = [sink_ref[g][:, 0:1] for g in groups]
        m = [jnp.maximum(jnp.max(s[g], axis=-1, keepdims=True), sk[g]) for g in groups]
        p = [jnp.exp2((s[g] - m[g]) * exp2_scale).astype(BF16) for g in groups]
        sink_p = [jnp.exp2((sk[g] - m[g]) * exp2_scale) for g in groups]
        return p, sink_p

    def outputs(j, p, sink_p):
        r0 = j * BLOCK
        for g in groups:
            vw = vbuf[r0:r0 + 3 * BLOCK, 2 * g * HEAD_DIM:(2 * g + 2) * HEAD_DIM]
            pv = jnp.dot(p[g], vw, preferred_element_type=F32)
            o = pv[:, :HEAD_DIM] / (pv[:, HEAD_DIM:] + sink_p[g])
            for h in range(GQA_G):
                c0 = (GQA_G * g + h) * HEAD_DIM
                o_ref[0, r0:r0 + BLOCK, c0:c0 + HEAD_DIM] = o[h * BLOCK:(h + 1) * BLOCK].astype(o_ref.dtype)

    s_q, p_q = {}, {}
    for step in range(n_sub + 2):
        if step >= 2:
            outputs(step - 2, *p_q.pop(step - 2))
        if 1 <= step <= n_sub:
            p_q[step - 1] = softmax_numerators(s_q.pop(step - 1))
        if step < n_sub:
            s_q[step] = scores(step)


def window_attention(qkv, sink, rel_bias):
    B, L, _ = qkv.shape
    nq = N_HEADS * HEAD_DIM
    sub = ATT_TQ // BLOCK
    nblk = L // BLOCK
    kcol = nq // KV_W
    vcol = kcol + 1

    buckets, band = _band_structure()
    bias = rel_bias[buckets].astype(F32)
    inv_scale = HEAD_DIM ** 0.5
    bias = jnp.where(band[:, :, None], bias * inv_scale, NEG)
    bias = jnp.transpose(bias, (2, 0, 1)).reshape(N_KV, GQA_G * BLOCK, 3 * BLOCK)
    sink_rows = jnp.broadcast_to((sink.astype(F32) * inv_scale).reshape(N_KV, GQA_G, 1, 1),
                                 (N_KV, GQA_G, BLOCK, V7X_LANES)).reshape(N_KV, GQA_G * BLOCK, V7X_LANES)

    return pl.pallas_call(
        _attn_body,
        grid=(B, L // ATT_TQ),
        in_specs=[
            pl.BlockSpec((1, ATT_TQ, nq), lambda b, i: (b, i, 0)),
            pl.BlockSpec((1, BLOCK, KV_W), lambda b, i: (b, jnp.maximum(sub * i - 1, 0), kcol)),
            pl.BlockSpec((1, ATT_TQ, KV_W), lambda b, i: (b, i, kcol)),
            pl.BlockSpec((1, BLOCK, KV_W), lambda b, i: (b, jnp.minimum(sub * i + sub, nblk - 1), kcol)),
            pl.BlockSpec((1, BLOCK, KV_W), lambda b, i: (b, jnp.maximum(sub * i - 1, 0), vcol)),
            pl.BlockSpec((1, ATT_TQ, KV_W), lambda b, i: (b, i, vcol)),
            pl.BlockSpec((1, BLOCK, KV_W), lambda b, i: (b, jnp.minimum(sub * i + sub, nblk - 1), vcol)),
            pl.BlockSpec((N_KV, GQA_G * BLOCK, 3 * BLOCK), lambda b, i: (0, 0, 0)),
            pl.BlockSpec((N_KV, GQA_G * BLOCK, V7X_LANES), lambda b, i: (0, 0, 0)),
        ],
        out_specs=pl.BlockSpec((1, ATT_TQ, nq), lambda b, i: (b, i, 0)),
        out_shape=jax.ShapeDtypeStruct((B, L, nq), BF16),
        scratch_shapes=[pltpu.VMEM((ATT_TQ + 2 * BLOCK, KV_W), BF16),
                        pltpu.VMEM((ATT_TQ + 2 * BLOCK, 2 * KV_W), BF16)],
        compiler_params=_cparams("parallel", "parallel"),
        name="window_attention",
    )(qkv, qkv, qkv, qkv, qkv, qkv, qkv, bias, sink_rows)


FEAT_PAD = V7X_LANES


def _filter_mlp_body(feat_ref, w1_ref, b1_ref, w2_ref, b2_ref, w3_ref, b3_ref, fr_ref, a_ref):
    fr = fr_ref[...]
    a = jnp.sin(fr * (jnp.dot(feat_ref[...], w1_ref[...], preferred_element_type=F32) + b1_ref[...]))
    a = jnp.sin(fr * (jnp.dot(a.astype(BF16), w2_ref[...], preferred_element_type=F32) + b2_ref[...]))
    a = jnp.sin(fr * (jnp.dot(a.astype(BF16), w3_ref[...], preferred_element_type=F32) + b3_ref[...]))
    a_ref[...] = a.astype(BF16)


def hyena_filter_mlp(L, f_w1, f_b1, f_w2, f_b2, f_w3, f_b3, f_freq, *, bt=1024):
    t = jnp.linspace(0.0, 1.0, L, dtype=F32)[:, None]
    w = 2.0 * math.pi * jnp.arange(L, dtype=F32) / L
    f = jnp.linspace(1e-4, HY_BANDS - 1, HY_BANDS, dtype=F32)
    ang = w[:, None] * f[None, :]
    feats = jnp.concatenate([t, jnp.cos(ang), -jnp.sin(ang), jnp.zeros((L, FEAT_PAD - HY_EMB), F32)], axis=-1)
    w1 = jnp.concatenate([f_w1.astype(F32), jnp.zeros((FEAT_PAD - HY_EMB, HY_FILTER_W), F32)], axis=0)
    W = HY_FILTER_W
    const = lambda r, c: pl.BlockSpec((r, c), lambda i: (0, 0))
    return pl.pallas_call(
        _filter_mlp_body,
        grid=(L // bt,),
        in_specs=[pl.BlockSpec((bt, FEAT_PAD), lambda i: (i, 0)), const(FEAT_PAD, W), const(1, W), const(W, W),
                  const(1, W), const(W, W), const(1, W), const(1, W)],
        out_specs=pl.BlockSpec((bt, W), lambda i: (i, 0)),
        out_shape=jax.ShapeDtypeStruct((L, W), BF16),
        compiler_params=_cparams("parallel"),
        name="hyena_filter_mlp",
    )(feats.astype(BF16), w1.astype(BF16), f_b1.reshape(1, W), f_w2.astype(BF16), f_b2.reshape(1, W),
      f_w3.astype(BF16), f_b3.reshape(1, W), f_freq.astype(F32).reshape(1, W))


def _dft_tables(L, paired):
    N = 2 * L
    P = FFT_P
    Q = N // P
    S = L // P
    b = np.arange(Q, dtype=np.float64)
    fq = np.exp(-2j * np.pi * np.outer(b, np.arange(S)) / Q)
    fp =np.exp(-2j * np.pi * np.outer(np.arange(P), np.arange(P)) / P)
    tw = np.exp(-2j * np.pi * np.outer(b, np.arange(P)) / N)
    ci = np.conj(fq).T / N

    def stack(c):
        return np.block([[c.real, -c.imag], [c.imag, c.real]])

    if paired:
        g1 = stack(fq)
        g4 = stack(ci)
    else:
        g1 = np.concatenate([fq.real, fq.imag], axis=0)
        g4 = np.concatenate([ci.real, -ci.imag], axis=1)
    fq_full = np.exp(-2j * np.pi * np.outer(b, np.arange(Q)) / Q)
    g1_full = np.concatenate([fq_full.real, fq_full.imag], axis=0)
    g2 =np.block([[fp.real, fp.imag], [-fp.imag, fp.real]])
    g2c = np.block([[fp.real, -fp.imag], [fp.imag, fp.real]])
    twr = np.tile(tw.real, (1, 2))
    twi = np.tile(tw.imag, (1, 2))
    f32 = lambda a: np.ascontiguousarray(a, dtype=np.float32)
    return dict(g1=f32(g1), g4=f32(g4), g1_full=f32(g1_full), g2=f32(g2), g2c=f32(g2c),
                twr=f32(twr), twi=f32(twi), Q=Q, S=S, N=N)


def _to_tiles(x, n_tiles):
    chunks = [x[:, s * FFT_P:(s + 1) * FFT_P] for s in range(n_tiles)]
    return jnp.swapaxes(jnp.stack(chunks, axis=0), 0, 1)


def _from_tiles(x):
    y = jnp.swapaxes(x, 0, 1)
    return jnp.concatenate([y[s] for s in range(y.shape[0])], axis=1)


def _fwd_fft(re_tiles, im_tiles, g1, twr, twi, g2):
    G = len(re_tiles)
    Q = twr.shape[0]
    P = FFT_P
    rows = []
    for c in range(0, G, 2):
        top = jnp.concatenate([re_tiles[c], re_tiles[c + 1]], axis=1)
        if im_tiles is None:
            rhs = top
        else:
            rhs = jnp.concatenate([top, jnp.concatenate([im_tiles[c], im_tiles[c + 1]], axis=1)], axis=0)
        y = jnp.dot(g1, rhs.astype(BF16), preferred_element_type=F32)
        y = y.astype(twr.dtype)
        yr, yi = y[:Q], y[Q:]
        zr = yr * twr - yi * twi
        zi = yr * twi + yi * twr
        rows.append(jnp.concatenate([zr[:, :P], zi[:, :P]], axis=1))
        rows.append(jnp.concatenate([zr[:, P:], zi[:, P:]], axis=1))
    lhs = jnp.concatenate(rows, axis=0).astype(BF16)
    return jnp.dot(lhs, g2, preferred_element_type=F32)


def _inv_fft(spec, g2c, twr, twi, g4, want_imag):
    Q = twr.shape[0]
    P = FFT_P
    G = spec.shape[0] // Q
    S = g4.shape[0] // 2 if want_imag else g4.shape[0]
    y = jnp.dot(spec.astype(BF16), g2c, preferred_element_type=F32)
    y = y.astype(twr.dtype)
    out_re, out_im = [], []
    for c in range(0, G, 2):
        ya = y[c * Q:(c + 1) * Q]
        yb = y[(c + 1) * Q:(c + 2) * Q]
        yr = jnp.concatenate([ya[:, :P], yb[:, :P]], axis=1)
        yi = jnp.concatenate([ya[:, P:], yb[:, P:]], axis=1)
        zr = yr * twr + yi * twi
        zi = yi * twr - yr * twi
        rhs = jnp.concatenate([zr, zi], axis=0).astype(BF16)
        o = jnp.dot(g4, rhs, preferred_element_type=F32)
        out_re += [o[:S, :P], o[:S, P:]]
        if want_imag:
            out_im += [o[S:, :P], o[S:, P:]]
    return out_re, out_im


def _hyena_body(v_ref, x1_ref, x2_ref, af_ref, ab_ref, wof_ref, wob_ref, delta_ref, tf_ref, tb_ref, par_ref,
                g1_ref, g1f_ref, twr_ref, twi_ref, g2_ref, g2c_ref, g4_ref, o_ref, taps_ref, kf_ref,
                *, S, Q, paired):
    g1, twr, twi = g1_ref[...], twr_ref[...], twi_ref[...]
    g2, g2c, g4 = g2_ref[...], g2c_ref[...], g4_ref[...]
    nb = 2 if paired else 1
    cb = v_ref.shape[1]
    n_groups = cb // CH_GROUP
    P = FFT_P
    G = CH_GROUP

    lane = lax.broadcasted_iota(jnp.int32, (G, S, P), 2)
    row = lax.broadcasted_iota(jnp.int32, (G, S, P), 1)

    @pl.when(pl.program_id(1) == 0)
    def _():
        g1f = g1f_ref[...]
        L = S * P
        delta = delta_ref[...]
        halves = ((wof_ref, af_ref, jnp.exp(-(delta * tf_ref[...]))),
                  (wob_ref, ab_ref, jnp.where(lax.broadcasted_iota(jnp.int32, (cb, L), 1) == 0, 0.0,
                                              jnp.exp(-(delta * tb_ref[...])))))
        for half, (wo_ref, a_ref, decay) in enumerate(halves):
            wo = wo_ref[...].reshape(HY_ORDER * cb, HY_FILTER_W)
            h = lax.dot_general(wo, a_ref[...], (((1,), (1,)), ((), ())), preferred_element_type=F32)
            for o in range(HY_ORDER):
                taps_ref[o, :, half * L:(half + 1) * L] = h[o * cb:(o + 1) * cb] * decay

        def filter_group(gi, carry):
            c0 = pl.multiple_of(gi * G, G)
            for o in range(HY_ORDER):
                k = taps_ref[o, pl.ds(c0, G), :]
                norm = jnp.sum(jnp.abs(k), axis=-1, keepdims=True)
                tiles = _to_tiles(k, Q)
                spec = _fwd_fft([tiles[c] for c in range(G)], None, g1f, twr, twi, g2)
                kf = spec.reshape(G, Q, 2 * P) * (1.0 / norm)[:, :, None]
                kf_ref[o, pl.ds(c0, G)] = kf.astype(kf_ref.dtype)
            return carry

        lax.fori_loop(0, n_groups, filter_group, 0)

    def short_conv(u, w0, w1, w2, bias):
        lr = pltpu.roll(u, 1, axis=2)
        prev = jnp.where(lane == 0, pltpu.roll(lr, 1, axis=1), lr)
        prev = jnp.where((lane == 0) & (row == 0), 0.0, prev)
        ll = pltpu.roll(u, P - 1, axis=2)
        nxt = jnp.where(lane == P - 1, pltpu.roll(ll, S - 1, axis=1), ll)
        nxt = jnp.where((lane == P - 1) & (row == S - 1), 0.0, nxt)
        return w0 * prev + w1 * u + w2 * nxt + bias

    def group(gi, carry):
        c0 = pl.multiple_of(gi * G, G)
        pw = lambda idx: par_ref[idx, pl.ds(c0, G)]

        def load(ref, b, base):
            u = _to_tiles(ref[b, pl.ds(c0, G), :], S)
            return short_conv(u, pw(base), pw(base + 1), pw(base + 2), pw(base + 3))

        z = [load(v_ref, b, 0) for b in range(nb)]
        gates = [[load(x1_ref, b, 4) for b in range(nb)], [load(x2_ref, b, 8) for b in range(nb)]]
        for o in range(HY_ORDER):
            re = [z[0][c] for c in range(G)]
            im = [z[1][c] for c in range(G)] if paired else None
            spec = _fwd_fft(re, im, g1, twr, twi, g2)
            kf = kf_ref[o, pl.ds(c0, G)].reshape(G * Q, 2 * P)
            spec = spec.astype(kf.dtype)
            xr, xi = spec[:, :P], spec[:, P:]
            kr, ki = kf[:, :P], kf[:, P:]
            prod = jnp.concatenate([xr * kr - xi * ki, xr * ki + xi * kr], axis=1)
            out_re, out_im = _inv_fft(prod, g2c, twr, twi, g4, paired)
            skip = pw(12 + o)
            conv = [jnp.stack(out_re, axis=0)] + ([jnp.stack(out_im, axis=0)] if paired else [])
            z = [gates[o][b] * (conv[b] + skip * z[b]) for b in range(nb)]
        for b in range(nb):
            o_ref[b, pl.ds(c0, G), :] = _from_tiles(z[b])
        return carry

    lax.fori_loop(0, n_groups, group, 0)


def hyena_operator(u_t, a, f_wout, conv_w, conv_b, skip, tabs, *, cb):
    B, C3, L = u_t.shape
    D = C3 // 3
    Q, S = tabs["Q"], tabs["S"]
    P = FFT_P
    W = HY_FILTER_W
    paired = B % 2 == 0
    nb = 2 if paired else 1
    wo = f_wout.T.reshape(HY_ORDER, 2, D, W).astype(BF16)
    wo_f, wo_b = wo[:, 0], wo[:, 1]
    deltas = np.abs(np.linspace(math.log(HY_TARGET) / HY_SLOW_PCT,
                                math.log(HY_TARGET) / HY_FAST_PCT, D)).astype(np.float32).reshape(D, 1)
    t_f = jnp.linspace(0.0, 1.0, L, dtype=F32).reshape(1, L)
    a_b = jnp.roll(jnp.flip(a, axis=0), 1, axis=0)
    t_b = jnp.roll(jnp.flip(t_f, axis=1), 1, axis=1)
    cols = []
    for part in range(3):
        sl = slice(part * D, (part + 1) * D)
        cols += [conv_w[0, sl], conv_w[1, sl], conv_w[2, sl], conv_b[sl]]
    cols += [skip[0], skip[1]]
    n_par = len(cols)
    par = jnp.broadcast_to(jnp.stack(cols, axis=0).astype(F32)[:, :, None, None], (n_par, D, 1, P))

    bf = lambda name: jnp.asarray(tabs[name]).astype(BF16)
    g1, g1f, g2, g2c, g4 = bf("g1"), bf("g1_full"), bf("g2"), bf("g2c"), bf("g4")
    twr, twi = bf("twr"), bf("twi")
    const = lambda arr: pl.BlockSpec(arr.shape, lambda c, p: (0,) * arr.ndim)
    ncb = D // cb
    return pl.pallas_call(
        functools.partial(_hyena_body, S=S, Q=Q, paired=paired),
        grid=(ncb, B // nb),
        in_specs=[
            pl.BlockSpec((nb, cb, L), lambda c, p: (p, c, 0)),
            pl.BlockSpec((nb, cb, L), lambda c, p: (p, c + ncb, 0)),
            pl.BlockSpec((nb, cb, L), lambda c, p: (p, c + 2 * ncb, 0)),
            const(a), const(a_b),
            pl.BlockSpec((HY_ORDER, cb, W), lambda c, p: (0, c, 0)),
            pl.BlockSpec((HY_ORDER, cb, W), lambda c, p: (0, c, 0)),
            pl.BlockSpec((cb, 1), lambda c, p: (c, 0)),
            const(t_f), const(t_b),
            pl.BlockSpec((n_par, cb, 1, P), lambda c, p: (0, c, 0, 0)),
            const(g1), const(g1f), const(twr), const(twi), const(g2), const(g2c), const(g4),
        ],
        out_specs=pl.BlockSpec((nb, cb, L), lambda c, p: (p, c, 0)),
        out_shape=jax.ShapeDtypeStruct((B, D, L), F32),
        scratch_shapes=[pltpu.VMEM((HY_ORDER, cb, 2 * L), F32), pltpu.VMEM((HY_ORDER, cb, Q, 2 * P), BF16)],
        compiler_params=_cparams("parallel", "arbitrary"),
        name="hyena_operator",
    )(u_t, u_t, u_t, a, a_b, wo_f, wo_b, jnp.asarray(deltas), t_f, t_b, par, g1, g1f, twr, twi, g2, g2c, g4)


def _trunk(x, p, cfg):
    B, L, D = x.shape
    tabs = _dft_tables(L, paired=(B % 2 == 0))
    for i in range(DEPTH):
        j = i // N_MIXERS
        if i % N_MIXERS == 0:
            a = hyena_filter_mlp(L, p["hy_f_w1"][j], p["hy_f_b1"][j], p["hy_f_w2"][j], p["hy_f_b2"][j],
                                 p["hy_f_w3"][j], p["hy_f_b3"][j], p["hy_f_freq"][j])
            u_t = norm_matmul_t(x, p["norm_mix_g"][i], p["hy_w_in_t"][j], p["hy_b_in"][j], bt=1024, bc=1024)
            z_t = hyena_operator(u_t, a, p["hy_f_wout"][j], p["hy_conv_w"][j], p["hy_conv_b"][j], p["hy_skip"][j],
                                 tabs, cb=cfg["hy_cb"])
            x = residual_matmul_t(x, z_t, p["hy_w_out"][j], p["hy_b_out"][j], bt=512)
        else:
            x2 = x.reshape(B * L, D)
            qkv = qkv_projection(x2, p["norm_mix_g"][i], p["at_w_qkv"][j], p["at_q_g"][j], p["at_k_g"][j], bm=1024)
            att = window_attention(qkv.reshape(B, L, -1), p["at_sink"][j], p["rel_bias"])
            x = residual_matmul(x2, att.reshape(B * L, -1), p["at_w_o"][j], bm=1024).reshape(B, L, D)
        x = ffn_block(x.reshape(B * L, D), p["norm_ffn_g"][i], p["ffn_w_gate_up"][i], p["ffn_w_down"][i],
                      bm=1024, bf=512).reshape(B, L, D)
    return x


def kernel(x_prompt, x_sample, norm_mix_g, norm_ffn_g, hy_w_in, hy_b_in, hy_conv_w, hy_conv_b, hy_f_w1, hy_f_b1,
           hy_f_w2, hy_f_b2, hy_f_w3, hy_f_b3, hy_f_wout, hy_f_freq, hy_skip, hy_w_out, hy_b_out, at_w_qkv, at_q_g,
           at_k_g, at_sink, at_w_o, rel_bias, ffn_w_gate_up, ffn_w_down):
    p = dict(
        norm_mix_g=norm_mix_g.astype(F32), norm_ffn_g=norm_ffn_g.astype(F32),
        hy_w_in_t=jnp.swapaxes(hy_w_in, 1, 2).astype(BF16), hy_b_in=hy_b_in,
        hy_conv_w=hy_conv_w, hy_conv_b=hy_conv_b,
        hy_f_w1=hy_f_w1, hy_f_b1=hy_f_b1, hy_f_w2=hy_f_w2, hy_f_b2=hy_f_b2, hy_f_w3=hy_f_w3, hy_f_b3=hy_f_b3,
        hy_f_wout=hy_f_wout, hy_f_freq=hy_f_freq, hy_skip=hy_skip,
        hy_w_out=hy_w_out.astype(BF16), hy_b_out=hy_b_out,
        at_w_qkv=at_w_qkv.astype(BF16), at_q_g=at_q_g, at_k_g=at_k_g, at_sink=at_sink,
        at_w_o=at_w_o.astype(BF16), rel_bias=rel_bias,
        ffn_w_gate_up=ffn_w_gate_up.astype(BF16), ffn_w_down=ffn_w_down.astype(BF16),
    )
    y_prompt = _trunk(x_prompt, p, dict(hy_cb=32))
    y_sample = _trunk(x_sample, p, dict(hy_cb=16))
    return (y_prompt, y_sample)
```

```python
import functools
import math

import jax
import jax.numpy as jnp
import numpy as np
from jax import lax
from jax.experimental import pallas as pl
from jax.experimental.pallas import tpu as pltpu

F32 = jnp.float32
BF16 = jnp.bfloat16

D_MODEL = 2048
DEPTH = 4
N_MIXERS = 2
HY_ORDER = 2
HY_EMB = 33
HY_BANDS = (HY_EMB - 1) // 2
HY_FILTER_W = 64
HY_FAST_PCT = 0.3
HY_SLOW_PCT = 1.5
HY_TARGET = 1e-2
N_HEADS = 16
HEAD_DIM = 128
N_KV = 4
GQA_G = N_HEADS // N_KV
WINDOW = 128
BLOCK = 128
N_BUCKETS = 32
MAX_DIST = 128
D_FF = -(-(8 * D_MODEL) // (3 * 256)) * 256
EPS = 1e-6
NEG = -1e30

V7X_LANES = 128
V7X_SUBLANES = 8
VMEM_LIMIT = 56 * 1024 * 1024

FFT_P = V7X_LANES
CH_GROUP = V7X_SUBLANES


def _cparams(*sem):
    return pltpu.CompilerParams(dimension_semantics=sem, vmem_limit_bytes=VMEM_LIMIT)


def _rms_bf16(x, g):
    ms = jnp.mean(x * x, axis=-1, keepdims=True)
    return (x * lax.rsqrt(ms + EPS) * g).astype(BF16)


def _qkv_proj_body(x_ref, g_ref, w_ref, qg_ref, kg_ref, o_ref):
    hn = _rms_bf16(x_ref[...], g_ref[...])
    nq, nk = N_HEADS * HEAD_DIM, N_KV * HEAD_DIM
    pair = 2 * HEAD_DIM
    for c0 in range(0, nq + nk, pair):
        acc = jnp.dot(hn, w_ref[:, c0:c0 + pair], preferred_element_type=F32)
        gain = qg_ref[...] if c0 < nq else kg_ref[...]
        for c in (0, HEAD_DIM):
            t = acc[:, c:c + HEAD_DIM]
            ms = jnp.mean(t * t, axis=-1, keepdims=True)
            o_ref[:, c0 + c:c0 + c + HEAD_DIM] = (t * lax.rsqrt(ms + EPS) * gain).astype(o_ref.dtype)
    o_ref[:, nq + nk:] = jnp.dot(hn, w_ref[:, nq + nk:], preferred_element_type=F32).astype(o_ref.dtype)


def qkv_projection(x, g, w, q_g, k_g, *, bm):
    T, K = x.shape
    N = w.shape[1]
    return pl.pallas_call(
        _qkv_proj_body,
        grid=(T // bm,),
        in_specs=[
            pl.BlockSpec((bm, K), lambda i: (i, 0)),
            pl.BlockSpec((1, K), lambda i: (0, 0)),
            pl.BlockSpec((K, N), lambda i: (0, 0), pipeline_mode=pl.Buffered(1)),
            pl.BlockSpec((1, HEAD_DIM), lambda i: (0, 0)),
            pl.BlockSpec((1, HEAD_DIM), lambda i: (0, 0)),
        ],
        out_specs=pl.BlockSpec((bm, N), lambda i: (i, 0)),
        out_shape=jax.ShapeDtypeStruct((T, N), BF16),
        compiler_params=_cparams("parallel"),
        name="qkv_projection",
    )(x, g.reshape(1, K), w, q_g.astype(F32).reshape(1, HEAD_DIM), k_g.astype(F32).reshape(1, HEAD_DIM))


IN_BT = V7X_SUBLANES * V7X_LANES


def _hyena_edge_body(x_ref, g_ref, w_ref, b_ref, o_ref):
    hn = _rms_bf16(x_ref[0], g_ref[...])
    o_ref[0] = lax.dot_general(w_ref[...], hn, (((1,), (1,)), ((), ())), preferred_element_type=F32) + b_ref[...]


def hyena_edge_projection(x_edge, g, w_t, bias, *, bc):
    B, n, K = x_edge.shape
    C = w_t.shape[0]
    return pl.pallas_call(
        _hyena_edge_body,
        grid=(B, C // bc),
        in_specs=[
            pl.BlockSpec((1, n, K), lambda b, c: (b, 0, 0)),
            pl.BlockSpec((1, K), lambda b, c: (0, 0)),
            pl.BlockSpec((bc, K), lambda b, c: (c, 0)),
            pl.BlockSpec((bc, 1), lambda b, c: (c, 0)),
        ],
        out_specs=pl.BlockSpec((1, bc, n), lambda b, c: (b, c, 0)),
        out_shape=jax.ShapeDtypeStruct((B, C, n), F32),
        compiler_params=_cparams("parallel", "parallel"),
        name="hyena_edge_projection",
    )(x_edge, g.reshape(1, K), w_t, bias.astype(F32).reshape(C, 1))


def _hyena_in_body(x_ref, g_ref, w_ref, b_ref, k0_ref, k1_ref, k2_ref, kb_ref, edge_ref, o_ref, hn_ref):
    t = pl.program_id(1)
    nt = pl.num_programs(1)

    @pl.when(pl.program_id(2) == 0)
    def _():
        hn_ref[...] = _rms_bf16(x_ref[0], g_ref[...])

    u = lax.dot_general(w_ref[...], hn_ref[...], (((1,), (1,)), ((), ())), preferred_element_type=F32) + b_ref[...]
    bc, bt = u.shape
    edge = edge_ref[0]
    eidx = lax.broadcasted_iota(jnp.int32, edge.shape, 1)
    pick = lambda k: jnp.sum(jnp.where(eidx == k, edge, 0.0), axis=1, keepdims=True)
    left = jnp.where(t == 0, 0.0, pick(t))
    right = jnp.where(t == nt - 1, 0.0, pick(nt + t))
    lane = lax.broadcasted_iota(jnp.int32, (bc, V7X_LANES), 1)
    rolled_p = pltpu.roll(u, 1, axis=1)
    first = jnp.where(lane == 0, left, rolled_p[:, :V7X_LANES])
    prev = jnp.concatenate([first, rolled_p[:, V7X_LANES:]], axis=1)
    rolled_n = pltpu.roll(u, bt - 1, axis=1)
    last = jnp.where(lane == V7X_LANES - 1, right, rolled_n[:, bt - V7X_LANES:])
    nxt = jnp.concatenate([rolled_n[:, :bt - V7X_LANES], last], axis=1)
    uc = k0_ref[...] * prev + k1_ref[...] * u + k2_ref[...] * nxt + kb_ref[...]
    for j in range(V7X_SUBLANES):
        o_ref[pl.ds(j, bc, stride=V7X_SUBLANES), :] = uc[:, j * V7X_LANES:(j + 1) * V7X_LANES]


def hyena_in_projection(x, g, w_t, bias, conv_w, conv_b, *, bc):
    B, L, K = x.shape
    C = w_t.shape[0]
    bt = IN_BT
    nt = L // bt
    xb = x.reshape(B, nt, bt, K)
    x_edge = jnp.concatenate([jnp.roll(xb[:, :, bt - 1], 1, axis=1), jnp.roll(xb[:, :, 0], -1, axis=1)], axis=1)
    edge = hyena_edge_projection(x_edge, g, w_t, bias, bc=bc)
    col = lambda v: v.astype(F32).reshape(C, 1)
    cspec = pl.BlockSpec((bc, 1), lambda b, t, c: (c, 0))
    return pl.pallas_call(
        _hyena_in_body,
        grid=(B, nt, C // bc),
        in_specs=[
            pl.BlockSpec((1, bt, K), lambda b, t, c: (b, t, 0)),
            pl.BlockSpec((1, K), lambda b, t, c: (0, 0)),
            pl.BlockSpec((bc, K), lambda b, t, c: (c, 0)),
            cspec, cspec, cspec, cspec, cspec,
            pl.BlockSpec((1, bc, 2 * nt), lambda b, t, c: (b, c, 0)),
        ],
        out_specs=pl.BlockSpec((None, None, bc * V7X_SUBLANES, V7X_LANES), lambda b, t, c: (b, t, c, 0)),
        out_shape=jax.ShapeDtypeStruct((B, nt, C * V7X_SUBLANES, V7X_LANES), F32),
        scratch_shapes=[pltpu.VMEM((bt, K), BF16)],
        compiler_params=_cparams("parallel", "parallel", "arbitrary"),
        name="hyena_in_projection",
    )(x, g.reshape(1, K), w_t, col(bias), col(conv_w[0]), col(conv_w[1]), col(conv_w[2]), col(conv_b), edge)


def _res_mm_body(a_ref, w_ref, x_ref, o_ref):
    o_ref[...] = x_ref[...] + jnp.dot(a_ref[...], w_ref[...], preferred_element_type=F32)


def residual_matmul(x, a, w, *, bm):
    T, K = a.shape
    N = w.shape[1]
    return pl.pallas_call(
        _res_mm_body,
        grid=(T // bm,),
        in_specs=[
            pl.BlockSpec((bm, K), lambda i: (i, 0)),
            pl.BlockSpec((K, N), lambda i: (0, 0), pipeline_mode=pl.Buffered(1)),
            pl.BlockSpec((bm, N), lambda i: (i, 0)),
        ],
        out_specs=pl.BlockSpec((bm, N), lambda i: (i, 0)),
        out_shape=jax.ShapeDtypeStruct((T, N), F32),
        compiler_params=_cparams("parallel"),
        name="residual_matmul",
    )(a, w, x)


def _res_mm_t_body(z_ref, w_ref, b_ref, x_ref, o_ref):
    z = z_ref[0].astype(BF16)
    y = lax.dot_general(z, w_ref[...], (((0,), (0,)), ((), ())), preferred_element_type=F32)
    o_ref[0] = x_ref[0] + y + b_ref[...]


def residual_matmul_t(x, z_t, w, bias, *, bt):
    B, K, L = z_t.shape
    N = w.shape[1]
    return pl.pallas_call(
        _res_mm_t_body,
        grid=(B, L // bt),
        in_specs=[
            pl.BlockSpec((1, K, bt), lambda b, t: (b, 0, t)),
            pl.BlockSpec((K, N), lambda b, t: (0, 0), pipeline_mode=pl.Buffered(1)),
            pl.BlockSpec((1, N), lambda b, t: (0, 0)),
            pl.BlockSpec((1, bt, N), lambda b, t: (b, t, 0)),
        ],
        out_specs=pl.BlockSpec((1, bt, N), lambda b, t: (b, t, 0)),
        out_shape=jax.ShapeDtypeStruct((B, L, N), F32),
        compiler_params=_cparams("parallel", "parallel"),
        name="residual_matmul_t",
    )(z_t, w, bias.reshape(1, N), x)


def _ffn_body(x_ref, g_ref, wg_ref, wu_ref, wd_ref, o_ref, hn_ref):
    @pl.when(pl.program_id(1) == 0)
    def _():
        x = x_ref[...]
        hn_ref[...] = _rms_bf16(x, g_ref[...])
        o_ref[...] = x

    h = hn_ref[...]
    gate = jnp.dot(h, wg_ref[...], preferred_element_type=F32)
    up = jnp.dot(h, wu_ref[...], preferred_element_type=F32)
    act = (gate * jax.nn.sigmoid(gate) * up).astype(BF16)
    o_ref[...] += jnp.dot(act, wd_ref[...], preferred_element_type=F32)


def ffn_block(x, g, w_gate_up, w_down, *, bm, bf):
    T, K = x.shape
    nf = D_FF // bf
    return pl.pallas_call(
        _ffn_body,
        grid=(T // bm, nf),
        in_specs=[
            pl.BlockSpec((bm, K), lambda i, f: (i, 0)),
            pl.BlockSpec((1, K), lambda i, f: (0, 0)),
            pl.BlockSpec((K, bf), lambda i, f: (0, f)),
            pl.BlockSpec((K, bf), lambda i, f: (0, f + nf)),
            pl.BlockSpec((bf, K), lambda i, f: (f, 0)),
        ],
        out_specs=pl.BlockSpec((bm, K), lambda i, f: (i, 0)),
        out_shape=jax.ShapeDtypeStruct((T, K), F32),
        scratch_shapes=[pltpu.VMEM((bm, K), BF16)],
        compiler_params=_cparams("parallel", "arbitrary"),
        name="ffn_block",
    )(x, g.reshape(1, K), w_gate_up, w_gate_up, w_down)


ATT_TQ = 4 * BLOCK
KV_W = N_KV * HEAD_DIM


def _band_structure():
    qi = np.arange(BLOCK)[:, None]
    ki = np.arange(3 * BLOCK)[None, :]
    rel = ki - BLOCK - qi
    nb = N_BUCKETS // 2
    max_exact = nb // 2
    n = np.abs(rel)
    large = max_exact + (np.log(np.maximum(n, 1) / max_exact) / math.log(MAX_DIST / max_exact)
                         * (nb - max_exact)).astype(np.int32)
    large = np.minimum(large, nb - 1)
    buckets = (rel > 0).astype(np.int32) * nb + np.where(n < max_exact, n, large).astype(np.int32)
    band = n <= WINDOW
    return buckets, band


def _attn_body(q_ref, kp_ref, kc_ref, kn_ref, vp_ref, vc_ref, vn_ref, bias_ref, sink_ref,
               o_ref, kbuf, vbuf):
    i = pl.program_id(1)
    last = pl.num_programs(1) - 1

    kbuf[0:BLOCK] = kp_ref[0]
    kbuf[BLOCK:BLOCK + ATT_TQ] = kc_ref[0]
    kbuf[BLOCK + ATT_TQ:] = kn_ref[0]
    ones = jnp.ones((ATT_TQ + 2 * BLOCK, HEAD_DIM), BF16)
    for g in range(N_KV):
        src = slice(g * HEAD_DIM, (g + 1) * HEAD_DIM)
        dst = slice(2 * g * HEAD_DIM, (2 * g + 1) * HEAD_DIM)
        vbuf[0:BLOCK, dst] = vp_ref[0, :, src]
        vbuf[BLOCK:BLOCK + ATT_TQ, dst] = vc_ref[0, :, src]
        vbuf[BLOCK + ATT_TQ:, dst] = vn_ref[0, :, src]
        vbuf[:, (2 * g + 1) * HEAD_DIM:(2 * g + 2) * HEAD_DIM] = ones

    lane = lax.broadcasted_iota(jnp.int32, (1, 3 * BLOCK), 1)
    first_edge = jnp.where((lane < BLOCK) & (i == 0), NEG, 0.0).astype(F32)
    last_edge = jnp.where((lane >= 2 * BLOCK) & (i == last), NEG, 0.0).astype(F32)
    exp2_scale = HEAD_DIM ** -0.5 * math.log2(math.e)

    n_sub = ATT_TQ // BLOCK
    groups = range(N_KV)

    def scores(j):
        r0 = j * BLOCK
        s = []
        for g in groups:
            qs = jnp.concatenate(
                [q_ref[0, r0:r0 + BLOCK, (GQA_G * g + h) * HEAD_DIM:(GQA_G * g + h + 1) * HEAD_DIM]
                 for h in range(GQA_G)], axis=0)
            kw = kbuf[r0:r0 + 3 * BLOCK, g * HEAD_DIM:(g + 1) * HEAD_DIM]
            sg = lax.dot_general(qs, kw, (((1,), (1,)), ((), ())), preferred_element_type=F32)
            sg = sg + bias_ref[g]
            if j == 0:
                sg = sg + first_edge
            if j == n_sub - 1:
                sg = sg + last_edge
            s.append(sg)
        return s

    def softmax_numerators(s):
        sk = [sink_ref[g][:, 0:1] for g in groups]
        m = [jnp.maximum(jnp.max(s[g], axis=-1, keepdims=True), sk[g]) for g in groups]
        p = [jnp.exp2((s[g] - m[g]) * exp2_scale).astype(BF16) for g in groups]
        sink_p = [jnp.exp2((sk[g] - m[g]) * exp2_scale) for g in groups]
        return p, sink_p

    def outputs(j, p, sink_p):
        r0 = j * BLOCK
        for g in groups:
            vw = vbuf[r0:r0 + 3 * BLOCK, 2 * g * HEAD_DIM:(2 * g + 2) * HEAD_DIM]
            pv = jnp.dot(p[g], vw, preferred_element_type=F32)
            o = pv[:, :HEAD_DIM] / (pv[:, HEAD_DIM:] + sink_p[g])
            for h in range(GQA_G):
                c0 = (GQA_G * g + h) * HEAD_DIM
                o_ref[0, r0:r0 + BLOCK, c0:c0 + HEAD_DIM] = o[h * BLOCK:(h + 1) * BLOCK].astype(o_ref.dtype)

    s_q, p_q = {}, {}
    for step in range(n_sub + 2):
        if step >= 2:
            outputs(step - 2, *p_q.pop(step - 2))
        if 1 <= step <= n_sub:
            p_q[step - 1] = softmax_numerators(s_q.pop(step - 1))
        if step < n_sub:
            s_q[step] = scores(step)


def window_attention(qkv, sink, rel_bias):
    B, L, _ = qkv.shape
    nq = N_HEADS * HEAD_DIM
    sub = ATT_TQ // BLOCK
    nblk = L // BLOCK
    kcol = nq // KV_W
    vcol = kcol + 1

    buckets, band = _band_structure()
    bias = rel_bias[buckets].astype(F32)
    inv_scale = HEAD_DIM ** 0.5
    bias = jnp.where(band[:, :, None], bias * inv_scale, NEG)
    bias = jnp.transpose(bias, (2, 0, 1)).reshape(N_KV, GQA_G * BLOCK, 3 * BLOCK)
    sink_rows = jnp.broadcast_to((sink.astype(F32) * inv_scale).reshape(N_KV, GQA_G, 1, 1),
                                 (N_KV, GQA_G, BLOCK, V7X_LANES)).reshape(N_KV, GQA_G * BLOCK, V7X_LANES)

    return pl.pallas_call(
        _attn_body,
        grid=(B, L // ATT_TQ),
        in_specs=[
            pl.BlockSpec((1, ATT_TQ, nq), lambda b, i: (b, i, 0)),
            pl.BlockSpec((1, BLOCK, KV_W), lambda b, i: (b, jnp.maximum(sub * i - 1, 0), kcol)),
            pl.BlockSpec((1, ATT_TQ, KV_W), lambda b, i: (b, i, kcol)),
            pl.BlockSpec((1, BLOCK, KV_W), lambda b, i: (b, jnp.minimum(sub * i + sub, nblk - 1), kcol)),
            pl.BlockSpec((1, BLOCK, KV_W), lambda b, i: (b, jnp.maximum(sub * i - 1, 0), vcol)),
            pl.BlockSpec((1, ATT_TQ, KV_W), lambda b, i: (b, i, vcol)),
            pl.BlockSpec((1, BLOCK, KV_W), lambda b, i: (b, jnp.minimum(sub * i + sub, nblk - 1), vcol)),
            pl.BlockSpec((N_KV, GQA_G * BLOCK, 3 * BLOCK), lambda b, i: (0, 0, 0)),
            pl.BlockSpec((N_KV, GQA_G * BLOCK, V7X_LANES), lambda b, i: (0, 0, 0)),
        ],
        out_specs=pl.BlockSpec((1, ATT_TQ, nq), lambda b, i: (b, i, 0)),
        out_shape=jax.ShapeDtypeStruct((B, L, nq), BF16),
        scratch_shapes=[pltpu.VMEM((ATT_TQ + 2 * BLOCK, KV_W), BF16),
                        pltpu.VMEM((ATT_TQ + 2 * BLOCK, 2 * KV_W), BF16)],
        compiler_params=_cparams("parallel", "parallel"),
        name="window_attention",
    )(qkv, qkv, qkv, qkv, qkv, qkv, qkv, bias, sink_rows)


FEAT_PAD = V7X_LANES


def _filter_mlp_body(feat_ref, w1_ref, b1_ref, w2_ref, b2_ref, w3_ref, b3_ref, fr_ref, a_ref):
    fr = fr_ref[...]
    a = jnp.sin(fr * (jnp.dot(feat_ref[...], w1_ref[...], preferred_element_type=F32) + b1_ref[...]))
    a = jnp.sin(fr * (jnp.dot(a.astype(BF16), w2_ref[...], preferred_element_type=F32) + b2_ref[...]))
    a = jnp.sin(fr * (jnp.dot(a.astype(BF16), w3_ref[...], preferred_element_type=F32) + b3_ref[...]))
    a_ref[...] = a.astype(BF16)


def hyena_filter_mlp(L, f_w1, f_b1, f_w2, f_b2, f_w3, f_b3, f_freq, *, bt=1024):
    t = jnp.linspace(0.0, 1.0, L, dtype=F32)[:, None]
    w = 2.0 * math.pi * jnp.arange(L, dtype=F32) / L
    f = jnp.linspace(1e-4, HY_BANDS - 1, HY_BANDS, dtype=F32)
    ang = w[:, None] * f[None, :]
    feats = jnp.concatenate([t, jnp.cos(ang), -jnp.sin(ang), jnp.zeros((L, FEAT_PAD - HY_EMB), F32)], axis=-1)
    w1 = jnp.concatenate([f_w1.astype(F32), jnp.zeros((FEAT_PAD - HY_EMB, HY_FILTER_W), F32)], axis=0)
    W = HY_FILTER_W
    const = lambda r, c: pl.BlockSpec((r, c), lambda i: (0, 0))
    return pl.pallas_call(
        _filter_mlp_body,
        grid=(L // bt,),
        in_specs=[pl.BlockSpec((bt, FEAT_PAD), lambda i: (i, 0)), const(FEAT_PAD, W), const(1, W), const(W, W),
                  const(1, W), const(W, W), const(1, W), const(1, W)],
        out_specs=pl.BlockSpec((bt, W), lambda i: (i, 0)),
        out_shape=jax.ShapeDtypeStruct((L, W), BF16),
        compiler_params=_cparams("parallel"),
        name="hyena_filter_mlp",
    )(feats.astype(BF16), w1.astype(BF16), f_b1.reshape(1, W), f_w2.astype(BF16), f_b2.reshape(1, W),
      f_w3.astype(BF16), f_b3.reshape(1, W), f_freq.astype(F32).reshape(1, W))


def _dft_tables(L, paired):
    N = 2 * L
    P = FFT_P
    Q = N // P
    S = L // P
    b = np.arange(Q, dtype=np.float64)
    fq = np.exp(-2j * np.pi * np.outer(b, np.arange(S)) / Q)
    fp =np.exp(-2j * np.pi * np.outer(np.arange(P), np.arange(P)) / P)
    tw = np.exp(-2j * np.pi * np.outer(b, np.arange(P)) / N)
    ci = np.conj(fq).T / N

    def stack(c):
        return np.block([[c.real, -c.imag], [c.imag, c.real]])

    if paired:
        g1 = stack(fq)
        g4 = stack(ci)
    else:
        g1 = np.concatenate([fq.real, fq.imag], axis=0)
        g4 = np.concatenate([ci.real, -ci.imag], axis=1)
    fq_full = np.exp(-2j * np.pi * np.outer(b, np.arange(Q)) / Q)
    g1_full = np.concatenate([fq_full.real, fq_full.imag], axis=0)
    g2 =np.block([[fp.real, fp.imag], [-fp.imag, fp.real]])
    g2c = np.block([[fp.real, -fp.imag], [fp.imag, fp.real]])
    twr = np.tile(tw.real, (1, 2))
    twi = np.tile(tw.imag, (1, 2))
    f32 = lambda a: np.ascontiguousarray(a, dtype=np.float32)
    return dict(g1=f32(g1), g4=f32(g4), g1_full=f32(g1_full), g2=f32(g2), g2c=f32(g2c),
                twr=f32(twr), twi=f32(twi), Q=Q, S=S, N=N)


def _to_tiles(x, n_tiles):
    chunks = [x[:, s * FFT_P:(s + 1) * FFT_P] for s in range(n_tiles)]
    return jnp.swapaxes(jnp.stack(chunks, axis=0), 0, 1)


def _from_tiles(x):
    y = jnp.swapaxes(x, 0, 1)
    return jnp.concatenate([y[s] for s in range(y.shape[0])], axis=1)


def _fwd_fft(re_tiles, im_tiles, g1, twr, twi, g2):
    G = len(re_tiles)
    Q = twr.shape[0]
    P = FFT_P
    rows = []
    for c in range(0, G, 2):
        top = jnp.concatenate([re_tiles[c], re_tiles[c + 1]], axis=1)
        if im_tiles is None:
            rhs = top
        else:
            rhs = jnp.concatenate([top, jnp.concatenate([im_tiles[c], im_tiles[c + 1]], axis=1)], axis=0)
        y = jnp.dot(g1, rhs.astype(BF16), preferred_element_type=F32)
        y = y.astype(twr.dtype)
        yr, yi = y[:Q], y[Q:]
        zr = yr * twr - yi * twi
        zi = yr * twi + yi * twr
        rows.append(jnp.concatenate([zr[:, :P], zi[:, :P]], axis=1))
        rows.append(jnp.concatenate([zr[:, P:], zi[:, P:]], axis=1))
    lhs = jnp.concatenate(rows, axis=0).astype(BF16)
    return jnp.dot(lhs, g2, preferred_element_type=F32)


def _inv_fft(spec, g2c, twr, twi, g4, want_imag):
    Q = twr.shape[0]
    P = FFT_P
    G = spec.shape[0] // Q
    S = g4.shape[0] // 2 if want_imag else g4.shape[0]
    y = jnp.dot(spec.astype(BF16), g2c, preferred_element_type=F32)
    y = y.astype(twr.dtype)
    out_re, out_im = [], []
    for c in range(0, G, 2):
        ya = y[c * Q:(c + 1) * Q]
        yb = y[(c + 1) * Q:(c + 2) * Q]
        yr = jnp.concatenate([ya[:, :P], yb[:, :P]], axis=1)
        yi = jnp.concatenate([ya[:, P:], yb[:, P:]], axis=1)
        zr = yr * twr + yi * twi
        zi = yi * twr - yr * twi
        rhs = jnp.concatenate([zr, zi], axis=0).astype(BF16)
        o = jnp.dot(g4, rhs, preferred_element_type=F32)
        out_re += [o[:S, :P], o[:S, P:]]
        if want_imag:
            out_im += [o[S:, :P], o[S:, P:]]
    return out_re, out_im


def _hyena_body(v_ref, x1_ref, x2_ref, af_ref, ab_ref, wof_ref, wob_ref, delta_ref, tf_ref, tb_ref, skip_ref,
                g1_ref, g1f_ref, twr_ref, twi_ref, g2_ref, g2c_ref, g4_ref, o_ref, taps_ref, kf_ref,
                *, S, Q, paired):
    g1, twr, twi = g1_ref[...], twr_ref[...], twi_ref[...]
    g2, g2c, g4 = g2_ref[...], g2c_ref[...], g4_ref[...]
    nb = 2 if paired else 1
    cb = kf_ref.shape[1]
    n_groups = cb // CH_GROUP
    P = FFT_P
    G = CH_GROUP

    @pl.when(pl.program_id(1) == 0)
    def _():
        g1f = g1f_ref[...]
        L = S * P
        delta = delta_ref[...]
        halves = ((wof_ref, af_ref, jnp.exp(-(delta * tf_ref[...]))),
                  (wob_ref, ab_ref, jnp.where(lax.broadcasted_iota(jnp.int32, (cb, L), 1) == 0, 0.0,
                                              jnp.exp(-(delta * tb_ref[...])))))
        for half, (wo_ref, a_ref, decay) in enumerate(halves):
            wo = wo_ref[...].reshape(HY_ORDER * cb, HY_FILTER_W)
            h = lax.dot_general(wo, a_ref[...], (((1,), (1,)), ((), ())), preferred_element_type=F32)
            for o in range(HY_ORDER):
                taps_ref[o, :, half * L:(half + 1) * L] = h[o * cb:(o + 1) * cb] * decay

        def filter_group(gi, carry):
            c0 = pl.multiple_of(gi * G, G)
            for o in range(HY_ORDER):
                k = taps_ref[o, pl.ds(c0, G), :]
                norm = jnp.sum(jnp.abs(k), axis=-1, keepdims=True)
                tiles = _to_tiles(k, Q)
                spec = _fwd_fft([tiles[c] for c in range(G)], None, g1f, twr, twi, g2)
                kf = spec.reshape(G, Q, 2 * P) * (1.0 / norm)[:, :, None]
                kf_ref[o, pl.ds(c0, G)] = kf.astype(kf_ref.dtype)
            return carry

        lax.fori_loop(0, n_groups, filter_group, 0)

    def group(gi, carry):
        c0 = pl.multiple_of(gi * G, G)
        r0 = pl.multiple_of(gi * (G * V7X_SUBLANES), G * V7X_SUBLANES)

        def tiles(ref, b):
            blk = ref[b, :, pl.ds(r0, G * V7X_SUBLANES), :]
            return [blk[:, c * V7X_SUBLANES:(c + 1) * V7X_SUBLANES, :].reshape(S, P) for c in range(G)]

        z = [tiles(v_ref, b) for b in range(nb)]
        gates = [[tiles(x1_ref, b) for b in range(nb)], [tiles(x2_ref, b) for b in range(nb)]]
        for o in range(HY_ORDER):
            re = z[0]
            im = z[1] if paired else None
            spec = _fwd_fft(re, im, g1, twr, twi, g2)
            kf = kf_ref[o, pl.ds(c0, G)].reshape(G * Q, 2 * P)
            spec = spec.astype(kf.dtype)
            xr, xi = spec[:, :P], spec[:, P:]
            kr, ki = kf[:, :P], kf[:, P:]
            prod = jnp.concatenate([xr * kr - xi * ki, xr * ki + xi * kr], axis=1)
            out_re, out_im = _inv_fft(prod, g2c, twr, twi, g4, paired)
            skip = skip_ref[o, pl.ds(c0, G)]
            conv = [out_re] + ([out_im] if paired else [])
            z = [[gates[o][b][c] * (conv[b][c] + skip[c] * z[b][c]) for c in range(G)] for b in range(nb)]
        for b in range(nb):
            o_ref[b, pl.ds(c0, G), :] = _from_tiles(jnp.stack(z[b], axis=0))
        return carry

    lax.fori_loop(0, n_groups, group, 0)


def hyena_operator(u_tiles, a, f_wout, skip, tabs, *, cb):
    B, nt, rows, P = u_tiles.shape
    D = rows // (3 * V7X_SUBLANES)
    L = nt * IN_BT
    S = L // P
    Q = tabs["Q"]
    W = HY_FILTER_W
    paired = B % 2 == 0
    nb = 2 if paired else 1
    wo = f_wout.T.reshape(HY_ORDER, 2, D, W).astype(BF16)
    wo_f, wo_b = wo[:, 0], wo[:, 1]
    deltas = np.abs(np.linspace(math.log(HY_TARGET) / HY_SLOW_PCT,
                                math.log(HY_TARGET) / HY_FAST_PCT, D)).astype(np.float32).reshape(D, 1)
    t_f = jnp.linspace(0.0, 1.0, L, dtype=F32).reshape(1, L)
    a_b = jnp.roll(jnp.flip(a, axis=0), 1, axis=0)
    t_b = jnp.roll(jnp.flip(t_f, axis=1), 1, axis=1)
    skip_rows = jnp.broadcast_to(skip.astype(F32)[:, :, None, None], (HY_ORDER, D, 1, P))

    bf = lambda name: jnp.asarray(tabs[name]).astype(BF16)
    g1, g1f, g2, g2c, g4 = bf("g1"), bf("g1_full"), bf("g2"), bf("g2c"), bf("g4")
    twr, twi = bf("twr"), bf("twi")
    const = lambda arr: pl.BlockSpec(arr.shape, lambda c, p: (0,) * arr.ndim)
    ncb = D // cb
    return pl.pallas_call(
        functools.partial(_hyena_body, S=S, Q=Q, paired=paired),
        grid=(ncb, B // nb),
        in_specs=[
            pl.BlockSpec((nb, nt, cb * V7X_SUBLANES, P), lambda c, p: (p, 0, c, 0)),
            pl.BlockSpec((nb, nt, cb * V7X_SUBLANES, P), lambda c, p: (p, 0, c + ncb, 0)),
            pl.BlockSpec((nb, nt, cb * V7X_SUBLANES, P), lambda c, p: (p, 0, c + 2 * ncb, 0)),
            const(a), const(a_b),
            pl.BlockSpec((HY_ORDER, cb, W), lambda c, p: (0, c, 0)),
            pl.BlockSpec((HY_ORDER, cb, W), lambda c, p: (0, c, 0)),
            pl.BlockSpec((cb, 1), lambda c, p: (c, 0)),
            const(t_f), const(t_b),
            pl.BlockSpec((HY_ORDER, cb, 1, P), lambda c, p: (0, c, 0, 0)),
            const(g1), const(g1f), const(twr), const(twi), const(g2), const(g2c), const(g4),
        ],
        out_specs=pl.BlockSpec((nb, cb, L), lambda c, p: (p, c, 0)),
        out_shape=jax.ShapeDtypeStruct((B, D, L), F32),
        scratch_shapes=[pltpu.VMEM((HY_ORDER, cb, 2 * L), F32), pltpu.VMEM((HY_ORDER, cb, Q, 2 * P), BF16)],
        compiler_params=_cparams("parallel", "arbitrary"),
        name="hyena_operator",
    )(u_tiles, u_tiles, u_tiles, a, a_b, wo_f, wo_b, jnp.asarray(deltas), t_f, t_b, skip_rows,
      g1, g1f, twr, twi, g2, g2c, g4)


def _trunk(x, p, cfg):
    B, L, D = x.shape
    tabs = _dft_tables(L, paired=(B % 2 == 0))
    for i in range(DEPTH):
        j = i // N_MIXERS
        if i % N_MIXERS == 0:
            a = hyena_filter_mlp(L, p["hy_f_w1"][j], p["hy_f_b1"][j], p["hy_f_w2"][j], p["hy_f_b2"][j],
                                 p["hy_f_w3"][j], p["hy_f_b3"][j], p["hy_f_freq"][j])
            u_tiles = hyena_in_projection(x, p["norm_mix_g"][i], p["hy_w_in_t"][j], p["hy_b_in"][j],
                                          p["hy_conv_w"][j], p["hy_conv_b"][j], bc=1024)
            z_t = hyena_operator(u_tiles, a, p["hy_f_wout"][j], p["hy_skip"][j], tabs, cb=cfg["hy_cb"])
            x = residual_matmul_t(x, z_t, p["hy_w_out"][j], p["hy_b_out"][j], bt=512)
        else:
            x2 = x.reshape(B * L, D)
            qkv = qkv_projection(x2, p["norm_mix_g"][i], p["at_w_qkv"][j], p["at_q_g"][j], p["at_k_g"][j], bm=1024)
            att = window_attention(qkv.reshape(B, L, -1), p["at_sink"][j], p["rel_bias"])
            x = residual_matmul(x2, att.reshape(B * L, -1), p["at_w_o"][j], bm=1024).reshape(B, L, D)
        x = ffn_block(x.reshape(B * L, D), p["norm_ffn_g"][i], p["ffn_w_gate_up"][i], p["ffn_w_down"][i],
                      bm=1024, bf=512).reshape(B, L, D)
    return x


def kernel(x_prompt, x_sample, norm_mix_g, norm_ffn_g, hy_w_in, hy_b_in, hy_conv_w, hy_conv_b, hy_f_w1, hy_f_b1,
           hy_f_w2, hy_f_b2, hy_f_w3, hy_f_b3, hy_f_wout, hy_f_freq, hy_skip, hy_w_out, hy_b_out, at_w_qkv, at_q_g,
           at_k_g, at_sink, at_w_o, rel_bias, ffn_w_gate_up, ffn_w_down):
    p = dict(
        norm_mix_g=norm_mix_g.astype(F32), norm_ffn_g=norm_ffn_g.astype(F32),
        hy_w_in_t=jnp.swapaxes(hy_w_in, 1, 2).astype(BF16), hy_b_in=hy_b_in,
        hy_conv_w=hy_conv_w, hy_conv_b=hy_conv_b,
        hy_f_w1=hy_f_w1, hy_f_b1=hy_f_b1, hy_f_w2=hy_f_w2, hy_f_b2=hy_f_b2, hy_f_w3=hy_f_w3, hy_f_b3=hy_f_b3,
        hy_f_wout=hy_f_wout, hy_f_freq=hy_f_freq, hy_skip=hy_skip,
        hy_w_out=hy_w_out.astype(BF16), hy_b_out=hy_b_out,
        at_w_qkv=at_w_qkv.astype(BF16), at_q_g=at_q_g, at_k_g=at_k_g, at_sink=at_sink,
        at_w_o=at_w_o.astype(BF16), rel_bias=rel_bias,
        ffn_w_gate_up=ffn_w_gate_up.astype(BF16), ffn_w_down=ffn_w_down.astype(BF16),
    )
    y_prompt = _trunk(x_prompt, p, dict(hy_cb=32))
    y_sample = _trunk(x_sample, p, dict(hy_cb=16))
    return (y_prompt, y_sample)
```

```python
import functools
import math

import jax
import jax.numpy as jnp
import numpy as np
from jax import lax
from jax.experimental import pallas as pl
from jax.experimental.pallas import tpu as pltpu

F32 = jnp.float32
BF16 = jnp.bfloat16

D_MODEL = 2048
DEPTH = 4
N_MIXERS = 2
HY_ORDER = 2
HY_EMB = 33
HY_BANDS = (HY_EMB - 1) // 2
HY_FILTER_W = 64
HY_FAST_PCT = 0.3
HY_SLOW_PCT = 1.5
HY_TARGET = 1e-2
N_HEADS = 16
HEAD_DIM = 128
N_KV = 4
GQA_G = N_HEADS // N_KV
WINDOW = 128
BLOCK = 128
N_BUCKETS = 32
MAX_DIST = 128
D_FF = -(-(8 * D_MODEL) // (3 * 256)) * 256
EPS = 1e-6
NEG = -1e30

V7X_LANES = 128
V7X_SUBLANES = 8
VMEM_LIMIT = 56 * 1024 * 1024

FFT_P = V7X_LANES
CH_GROUP = V7X_SUBLANES


def _cparams(*sem):
    return pltpu.CompilerParams(dimension_semantics=sem, vmem_limit_bytes=VMEM_LIMIT)


def _rms_bf16(x, g):
    ms = jnp.mean(x * x, axis=-1, keepdims=True)
    return (x * lax.rsqrt(ms + EPS) * g).astype(BF16)


def _qkv_proj_body(x_ref, g_ref, w_ref, qg_ref, kg_ref, o_ref):
    hn = _rms_bf16(x_ref[...], g_ref[...])
    nq, nk = N_HEADS * HEAD_DIM, N_KV * HEAD_DIM
    pair = 2 * HEAD_DIM
    for c0 in range(0, nq + nk, pair):
        acc = jnp.dot(hn, w_ref[:, c0:c0 + pair], preferred_element_type=F32)
        gain = qg_ref[...] if c0 < nq else kg_ref[...]
        for c in (0, HEAD_DIM):
            t = acc[:, c:c + HEAD_DIM]
            ms = jnp.mean(t * t, axis=-1, keepdims=True)
            o_ref[:, c0 + c:c0 + c + HEAD_DIM] = (t * lax.rsqrt(ms + EPS) * gain).astype(o_ref.dtype)
    o_ref[:, nq + nk:] = jnp.dot(hn, w_ref[:, nq + nk:], preferred_element_type=F32).astype(o_ref.dtype)


def qkv_projection(x, g, w, q_g, k_g, *, bm):
    T, K = x.shape
    N = w.shape[1]
    return pl.pallas_call(
        _qkv_proj_body,
        grid=(T // bm,),
        in_specs=[
            pl.BlockSpec((bm, K), lambda i: (i, 0)),
            pl.BlockSpec((1, K), lambda i: (0, 0)),
            pl.BlockSpec((K, N), lambda i: (0, 0), pipeline_mode=pl.Buffered(1)),
            pl.BlockSpec((1, HEAD_DIM), lambda i: (0, 0)),
            pl.BlockSpec((1, HEAD_DIM), lambda i: (0, 0)),
        ],
        out_specs=pl.BlockSpec((bm, N), lambda i: (i, 0)),
        out_shape=jax.ShapeDtypeStruct((T, N), BF16),
        compiler_params=_cparams("parallel"),
        name="qkv_projection",
    )(x, g.reshape(1, K), w, q_g.astype(F32).reshape(1, HEAD_DIM), k_g.astype(F32).reshape(1, HEAD_DIM))


IN_BT = V7X_SUBLANES * V7X_LANES
IN_ROW_SPLIT = 4


def _hyena_edge_body(x_ref, g_ref, w_ref, b_ref, o_ref):
    hn = _rms_bf16(x_ref[0], g_ref[...])
    o_ref[0] = lax.dot_general(w_ref[...], hn, (((1,), (1,)), ((), ())), preferred_element_type=F32) + b_ref[...]


def hyena_edge_projection(x_edge, g, w_t, bias, *, bc):
    B, n, K = x_edge.shape
    C = w_t.shape[0]
    return pl.pallas_call(
        _hyena_edge_body,
        grid=(B, C // bc),
        in_specs=[
            pl.BlockSpec((1, n, K), lambda b, c: (b, 0, 0)),
            pl.BlockSpec((1, K), lambda b, c: (0, 0)),
            pl.BlockSpec((bc, K), lambda b, c: (c, 0)),
            pl.BlockSpec((bc, 1), lambda b, c: (c, 0)),
        ],
        out_specs=pl.BlockSpec((1, bc, n), lambda b, c: (b, c, 0)),
        out_shape=jax.ShapeDtypeStruct((B, C, n), F32),
        compiler_params=_cparams("parallel", "parallel"),
        name="hyena_edge_projection",
    )(x_edge, g.reshape(1, K), w_t, bias.astype(F32).reshape(C, 1))


def _hyena_in_body(x_ref, g_ref, w_ref, b_ref, k0_ref, k1_ref, k2_ref, kb_ref, edge_ref, o_ref, hn_ref):
    t = pl.program_id(1)
    nt = pl.num_programs(1)

    @pl.when(pl.program_id(2) == 0)
    def _():
        hn_ref[...] = _rms_bf16(x_ref[0], g_ref[...])

    P = V7X_LANES
    bt = hn_ref.shape[0]
    n_chunks = bt // P
    rows = w_ref.shape[0] // IN_ROW_SPLIT
    for r in range(IN_ROW_SPLIT):
        rs = slice(r * rows, (r + 1) * rows)
        u = lax.dot_general(w_ref[rs, :], hn_ref[...], (((1,), (1,)), ((), ())), preferred_element_type=F32)
        edge = edge_ref[0, rs, :]
        eidx = lax.broadcasted_iota(jnp.int32, edge.shape, 1)
        pick = lambda k: jnp.sum(jnp.where(eidx == k, edge, 0.0), axis=1, keepdims=True)
        left = jnp.where(t == 0, 0.0, pick(t))
        right = jnp.where(t == nt - 1, 0.0, pick(nt + t))
        lane = lax.broadcasted_iota(jnp.int32, (rows, P), 1)
        bias, k0, k1, k2, kb = (ref[rs, :] for ref in (b_ref, k0_ref, k1_ref, k2_ref, kb_ref))
        chunks = [u[:, j * P:(j + 1) * P] + bias for j in range(n_chunks)]
        fwd = [pltpu.roll(c, 1, axis=1) for c in chunks]
        bwd = [pltpu.roll(c, P - 1, axis=1) for c in chunks]
        for j in range(n_chunks):
            prev = jnp.where(lane == 0, fwd[j - 1] if j > 0 else left, fwd[j])
            nxt = jnp.where(lane == P - 1, bwd[j + 1] if j + 1 < n_chunks else right, bwd[j])
            o_ref[pl.ds(r * rows * n_chunks + j, rows, stride=n_chunks), :] = (
                k0 * prev + k1 * chunks[j] + k2 * nxt + kb)


def hyena_in_projection(x, g, w_t, bias, conv_w, conv_b, *, bc):
    B, L, K = x.shape
    C = w_t.shape[0]
    bt = IN_BT
    nt = L // bt
    xb = x.reshape(B, nt, bt, K)
    x_edge = jnp.concatenate([jnp.roll(xb[:, :, bt - 1], 1, axis=1), jnp.roll(xb[:, :, 0], -1, axis=1)], axis=1)
    edge = hyena_edge_projection(x_edge, g, w_t, bias, bc=bc)
    col = lambda v: jnp.broadcast_to(v.astype(F32).reshape(C, 1), (C, V7X_LANES))
    cspec = pl.BlockSpec((bc, V7X_LANES), lambda b, t, c: (c, 0))
    return pl.pallas_call(
        _hyena_in_body,
        grid=(B, nt, C // bc),
        in_specs=[
            pl.BlockSpec((1, bt, K), lambda b, t, c: (b, t, 0)),
            pl.BlockSpec((1, K), lambda b, t, c: (0, 0)),
            pl.BlockSpec((bc, K), lambda b, t, c: (c, 0)),
            cspec, cspec, cspec, cspec, cspec,
            pl.BlockSpec((1, bc, 2 * nt), lambda b, t, c: (b, c, 0)),
        ],
        out_specs=pl.BlockSpec((None, None, bc * V7X_SUBLANES, V7X_LANES), lambda b, t, c: (b, t, c, 0)),
        out_shape=jax.ShapeDtypeStruct((B, nt, C * V7X_SUBLANES, V7X_LANES), F32),
        scratch_shapes=[pltpu.VMEM((bt, K), BF16)],
        compiler_params=_cparams("parallel", "parallel", "arbitrary"),
        name="hyena_in_projection",
    )(x, g.reshape(1, K), w_t, col(bias), col(conv_w[0]), col(conv_w[1]), col(conv_w[2]), col(conv_b), edge)


def _res_mm_body(a_ref, w_ref, x_ref, o_ref):
    o_ref[...] = x_ref[...] + jnp.dot(a_ref[...], w_ref[...], preferred_element_type=F32)


def residual_matmul(x, a, w, *, bm):
    T, K = a.shape
    N = w.shape[1]
    return pl.pallas_call(
        _res_mm_body,
        grid=(T // bm,),
        in_specs=[
            pl.BlockSpec((bm, K), lambda i: (i, 0)),
            pl.BlockSpec((K, N), lambda i: (0, 0), pipeline_mode=pl.Buffered(1)),
            pl.BlockSpec((bm, N), lambda i: (i, 0)),
        ],
        out_specs=pl.BlockSpec((bm, N), lambda i: (i, 0)),
        out_shape=jax.ShapeDtypeStruct((T, N), F32),
        compiler_params=_cparams("parallel"),
        name="residual_matmul",
    )(a, w, x)


def _res_mm_t_body(z_ref, w_ref, b_ref, x_ref, o_ref):
    z = z_ref[0].astype(BF16)
    y = lax.dot_general(z, w_ref[...], (((0,), (0,)), ((), ())), preferred_element_type=F32)
    o_ref[0] = x_ref[0] + y + b_ref[...]


def residual_matmul_t(x, z_t, w, bias, *, bt):
    B, K, L = z_t.shape
    N = w.shape[1]
    return pl.pallas_call(
        _res_mm_t_body,
        grid=(B, L // bt),
        in_specs=[
            pl.BlockSpec((1, K, bt), lambda b, t: (b, 0, t)),
            pl.BlockSpec((K, N), lambda b, t: (0, 0), pipeline_mode=pl.Buffered(1)),
            pl.BlockSpec((1, N), lambda b, t: (0, 0)),
            pl.BlockSpec((1, bt, N), lambda b, t: (b, t, 0)),
        ],
        out_specs=pl.BlockSpec((1, bt, N), lambda b, t: (b, t, 0)),
        out_shape=jax.ShapeDtypeStruct((B, L, N), F32),
        compiler_params=_cparams("parallel", "parallel"),
        name="residual_matmul_t",
    )(z_t, w, bias.reshape(1, N), x)


def _ffn_body(x_ref, g_ref, wg_ref, wu_ref, wd_ref, o_ref, hn_ref):
    @pl.when(pl.program_id(1) == 0)
    def _():
        x = x_ref[...]
        hn_ref[...] = _rms_bf16(x, g_ref[...])
        o_ref[...] = x

    h = hn_ref[...]
    gate = jnp.dot(h, wg_ref[...], preferred_element_type=F32)
    up = jnp.dot(h, wu_ref[...], preferred_element_type=F32)
    act = (gate * jax.nn.sigmoid(gate) * up).astype(BF16)
    o_ref[...] += jnp.dot(act, wd_ref[...], preferred_element_type=F32)


def ffn_block(x, g, w_gate_up, w_down, *, bm, bf):
    T, K = x.shape
    nf = D_FF // bf
    return pl.pallas_call(
        _ffn_body,
        grid=(T // bm, nf),
        in_specs=[
            pl.BlockSpec((bm, K), lambda i, f: (i, 0)),
            pl.BlockSpec((1, K), lambda i, f: (0, 0)),
            pl.BlockSpec((K, bf), lambda i, f: (0, f)),
            pl.BlockSpec((K, bf), lambda i, f: (0, f + nf)),
            pl.BlockSpec((bf, K), lambda i, f: (f, 0)),
        ],
        out_specs=pl.BlockSpec((bm, K), lambda i, f: (i, 0)),
        out_shape=jax.ShapeDtypeStruct((T, K), F32),
        scratch_shapes=[pltpu.VMEM((bm, K), BF16)],
        compiler_params=_cparams("parallel", "arbitrary"),
        name="ffn_block",
    )(x, g.reshape(1, K), w_gate_up, w_gate_up, w_down)


ATT_TQ = 4 * BLOCK
KV_W = N_KV * HEAD_DIM


def _band_structure():
    qi = np.arange(BLOCK)[:, None]
    ki = np.arange(3 * BLOCK)[None, :]
    rel = ki - BLOCK - qi
    nb = N_BUCKETS // 2
    max_exact = nb // 2
    n = np.abs(rel)
    large = max_exact + (np.log(np.maximum(n, 1) / max_exact) / math.log(MAX_DIST / max_exact)
                         * (nb - max_exact)).astype(np.int32)
    large = np.minimum(large, nb - 1)
    buckets = (rel > 0).astype(np.int32) * nb + np.where(n < max_exact, n, large).astype(np.int32)
    band = n <= WINDOW
    return buckets, band


def _attn_body(q_ref, kp_ref, kc_ref, kn_ref, vp_ref, vc_ref, vn_ref, bias_ref, sink_ref,
               o_ref, kbuf, vbuf):
    i = pl.program_id(1)
    last = pl.num_programs(1) - 1

    kbuf[0:BLOCK] = kp_ref[0]
    kbuf[BLOCK:BLOCK + ATT_TQ] = kc_ref[0]
    kbuf[BLOCK + ATT_TQ:] = kn_ref[0]
    ones = jnp.ones((ATT_TQ + 2 * BLOCK, HEAD_DIM), BF16)
    for g in range(N_KV):
        src = slice(g * HEAD_DIM, (g + 1) * HEAD_DIM)
        dst = slice(2 * g * HEAD_DIM, (2 * g + 1) * HEAD_DIM)
        vbuf[0:BLOCK, dst] = vp_ref[0, :, src]
        vbuf[BLOCK:BLOCK + ATT_TQ, dst] = vc_ref[0, :, src]
        vbuf[BLOCK + ATT_TQ:, dst] = vn_ref[0, :, src]
        vbuf[:, (2 * g + 1) * HEAD_DIM:(2 * g + 2) * HEAD_DIM] = ones

    lane = lax.broadcasted_iota(jnp.int32, (1, 3 * BLOCK), 1)
    first_edge = jnp.where((lane < BLOCK) & (i == 0), NEG, 0.0).astype(F32)
    last_edge = jnp.where((lane >= 2 * BLOCK) & (i == last), NEG, 0.0).astype(F32)
    exp2_scale = HEAD_DIM ** -0.5 * math.log2(math.e)

    n_sub = ATT_TQ // BLOCK
    groups = range(N_KV)

    def scores(j):
        r0 = j * BLOCK
        s = []
        for g in groups:
            qs = jnp.concatenate(
                [q_ref[0, r0:r0 + BLOCK, (GQA_G * g + h) * HEAD_DIM:(GQA_G * g + h + 1) * HEAD_DIM]
                 for h in range(GQA_G)], axis=0)
            kw = kbuf[r0:r0 + 3 * BLOCK, g * HEAD_DIM:(g + 1) * HEAD_DIM]
            sg = lax.dot_general(qs, kw, (((1,), (1,)), ((), ())), preferred_element_type=F32)
            sg = sg + bias_ref[g]
            if j == 0:
                sg = sg + first_edge
            if j == n_sub - 1:
                sg = sg + last_edge
            s.append(sg)
        return s

    def softmax_numerators(s):
        sk = [sink_ref[g][:, 0:1] for g in groups]
        m = [jnp.maximum(jnp.max(s[g], axis=-1, keepdims=True), sk[g]) for g in groups]
        p = [jnp.exp2((s[g] - m[g]) * exp2_scale).astype(BF16) for g in groups]
        sink_p = [jnp.exp2((sk[g] - m[g]) * exp2_scale) for g in groups]
        return p, sink_p

    def outputs(j, p, sink_p):
        r0 = j * BLOCK
        for g in groups:
            vw = vbuf[r0:r0 + 3 * BLOCK, 2 * g * HEAD_DIM:(2 * g + 2) * HEAD_DIM]
            pv = jnp.dot(p[g], vw, preferred_element_type=F32)
            o = pv[:, :HEAD_DIM] / (pv[:, HEAD_DIM:] + sink_p[g])
            for h in range(GQA_G):
                c0 = (GQA_G * g + h) * HEAD_DIM
                o_ref[0, r0:r0 + BLOCK, c0:c0 + HEAD_DIM] = o[h * BLOCK:(h + 1) * BLOCK].astype(o_ref.dtype)

    s_q, p_q = {}, {}
    for step in range(n_sub + 2):
        if step >= 2:
            outputs(step - 2, *p_q.pop(step - 2))
        if 1 <= step <= n_sub:
            p_q[step - 1] = softmax_numerators(s_q.pop(step - 1))
        if step < n_sub:
            s_q[step] = scores(step)


def window_attention(qkv, sink, rel_bias):
    B, L, _ = qkv.shape
    nq = N_HEADS * HEAD_DIM
    sub = ATT_TQ // BLOCK
    nblk = L // BLOCK
    kcol = nq // KV_W
    vcol = kcol + 1

    buckets, band = _band_structure()
    bias = rel_bias[buckets].astype(F32)
    inv_scale = HEAD_DIM ** 0.5
    bias = jnp.where(band[:, :, None], bias * inv_scale, NEG)
    bias = jnp.transpose(bias, (2, 0, 1)).reshape(N_KV, GQA_G * BLOCK, 3 * BLOCK)
    sink_rows = jnp.broadcast_to((sink.astype(F32) * inv_scale).reshape(N_KV, GQA_G, 1, 1),
                                 (N_KV, GQA_G, BLOCK, V7X_LANES)).reshape(N_KV, GQA_G * BLOCK, V7X_LANES)

    return pl.pallas_call(
        _attn_body,
        grid=(B, L // ATT_TQ),
        in_specs=[
            pl.BlockSpec((1, ATT_TQ, nq), lambda b, i: (b, i, 0)),
            pl.BlockSpec((1, BLOCK, KV_W), lambda b, i: (b, jnp.maximum(sub * i - 1, 0), kcol)),
            pl.BlockSpec((1, ATT_TQ, KV_W), lambda b, i: (b, i, kcol)),
            pl.BlockSpec((1, BLOCK, KV_W), lambda b, i: (b, jnp.minimum(sub * i + sub, nblk - 1), kcol)),
            pl.BlockSpec((1, BLOCK, KV_W), lambda b, i: (b, jnp.maximum(sub * i - 1, 0), vcol)),
            pl.BlockSpec((1, ATT_TQ, KV_W), lambda b, i: (b, i, vcol)),
            pl.BlockSpec((1, BLOCK, KV_W), lambda b, i: (b, jnp.minimum(sub * i + sub, nblk - 1), vcol)),
            pl.BlockSpec((N_KV, GQA_G * BLOCK, 3 * BLOCK), lambda b, i: (0, 0, 0)),
            pl.BlockSpec((N_KV, GQA_G * BLOCK, V7X_LANES), lambda b, i: (0, 0, 0)),
        ],
        out_specs=pl.BlockSpec((1, ATT_TQ, nq), lambda b, i: (b, i, 0)),
        out_shape=jax.ShapeDtypeStruct((B, L, nq), BF16),
        scratch_shapes=[pltpu.VMEM((ATT_TQ + 2 * BLOCK, KV_W), BF16),
                        pltpu.VMEM((ATT_TQ + 2 * BLOCK, 2 * KV_W), BF16)],
        compiler_params=_cparams("parallel", "parallel"),
        name="window_attention",
    )(qkv, qkv, qkv, qkv, qkv, qkv, qkv, bias, sink_rows)


FEAT_PAD = V7X_LANES


def _filter_mlp_body(feat_ref, w1_ref, b1_ref, w2_ref, b2_ref, w3_ref, b3_ref, fr_ref, a_ref):
    fr = fr_ref[...]
    a = jnp.sin(fr * (jnp.dot(feat_ref[...], w1_ref[...], preferred_element_type=F32) + b1_ref[...]))
    a = jnp.sin(fr * (jnp.dot(a.astype(BF16), w2_ref[...], preferred_element_type=F32) + b2_ref[...]))
    a = jnp.sin(fr * (jnp.dot(a.astype(BF16), w3_ref[...], preferred_element_type=F32) + b3_ref[...]))
    a_ref[...] = a.astype(BF16)


def hyena_filter_mlp(L, f_w1, f_b1, f_w2, f_b2, f_w3, f_b3, f_freq, *, bt=1024):
    t = jnp.linspace(0.0, 1.0, L, dtype=F32)[:, None]
    w = 2.0 * math.pi * jnp.arange(L, dtype=F32) / L
    f = jnp.linspace(1e-4, HY_BANDS - 1, HY_BANDS, dtype=F32)
    ang = w[:, None] * f[None, :]
    feats = jnp.concatenate([t, jnp.cos(ang), -jnp.sin(ang), jnp.zeros((L, FEAT_PAD - HY_EMB), F32)], axis=-1)
    w1 = jnp.concatenate([f_w1.astype(F32), jnp.zeros((FEAT_PAD - HY_EMB, HY_FILTER_W), F32)], axis=0)
    W = HY_FILTER_W
    const = lambda r, c: pl.BlockSpec((r, c), lambda i: (0, 0))
    return pl.pallas_call(
        _filter_mlp_body,
        grid=(L // bt,),
        in_specs=[pl.BlockSpec((bt, FEAT_PAD), lambda i: (i, 0)), const(FEAT_PAD, W), const(1, W), const(W, W),
                  const(1, W), const(W, W), const(1, W), const(1, W)],
        out_specs=pl.BlockSpec((bt, W), lambda i: (i, 0)),
        out_shape=jax.ShapeDtypeStruct((L, W), BF16),
        compiler_params=_cparams("parallel"),
        name="hyena_filter_mlp",
    )(feats.astype(BF16), w1.astype(BF16), f_b1.reshape(1, W), f_w2.astype(BF16), f_b2.reshape(1, W),
      f_w3.astype(BF16), f_b3.reshape(1, W), f_freq.astype(F32).reshape(1, W))


def _dft_tables(L, paired):
    N = 2 * L
    P = FFT_P
    Q = N // P
    S = L // P
    b = np.arange(Q, dtype=np.float64)
    fq = np.exp(-2j * np.pi * np.outer(b, np.arange(S)) / Q)
    fp =np.exp(-2j * np.pi * np.outer(np.arange(P), np.arange(P)) / P)
    tw = np.exp(-2j * np.pi * np.outer(b, np.arange(P)) / N)
    ci = np.conj(fq).T / N

    def stack(c):
        return np.block([[c.real, -c.imag], [c.imag, c.real]])

    if paired:
        g1 = stack(fq)
        g4 = stack(ci)
    else:
        g1 = np.concatenate([fq.real, fq.imag], axis=0)
        g4 = np.concatenate([ci.real, -ci.imag], axis=1)
    fq_full = np.exp(-2j * np.pi * np.outer(b, np.arange(Q)) / Q)
    g1_full = np.concatenate([fq_full.real, fq_full.imag], axis=0)
    g2 =np.block([[fp.real, fp.imag], [-fp.imag, fp.real]])
    g2c = np.block([[fp.real, -fp.imag], [fp.imag, fp.real]])
    twr = np.tile(tw.real, (1, 2))
    twi = np.tile(tw.imag, (1, 2))
    f32 = lambda a: np.ascontiguousarray(a, dtype=np.float32)
    return dict(g1=f32(g1), g4=f32(g4), g1_full=f32(g1_full), g2=f32(g2), g2c=f32(g2c),
                twr=f32(twr), twi=f32(twi), Q=Q, S=S, N=N)


def _to_tiles(x, n_tiles):
    chunks = [x[:, s * FFT_P:(s + 1) * FFT_P] for s in range(n_tiles)]
    return jnp.swapaxes(jnp.stack(chunks, axis=0), 0, 1)


def _from_tiles(x):
    y = jnp.swapaxes(x, 0, 1)
    return jnp.concatenate([y[s] for s in range(y.shape[0])], axis=1)


def _fwd_fft(re_tiles, im_tiles, g1, twr, twi, g2):
    G = len(re_tiles)
    Q = twr.shape[0]
    P = FFT_P
    rows = []
    for c in range(0, G, 2):
        top = jnp.concatenate([re_tiles[c], re_tiles[c + 1]], axis=1)
        if im_tiles is None:
            rhs = top
        else:
            rhs = jnp.concatenate([top, jnp.concatenate([im_tiles[c], im_tiles[c + 1]], axis=1)], axis=0)
        y = jnp.dot(g1, rhs.astype(BF16), preferred_element_type=F32)
        y = y.astype(twr.dtype)
        yr, yi = y[:Q], y[Q:]
        zr = yr * twr - yi * twi
        zi = yr * twi + yi * twr
        rows.append(jnp.concatenate([zr[:, :P], zi[:, :P]], axis=1))
        rows.append(jnp.concatenate([zr[:, P:], zi[:, P:]], axis=1))
    lhs = jnp.concatenate(rows, axis=0).astype(BF16)
    return jnp.dot(lhs, g2, preferred_element_type=F32)


def _inv_fft(spec, g2c, twr, twi, g4, want_imag):
    Q = twr.shape[0]
    P = FFT_P
    G = spec.shape[0] // Q
    S = g4.shape[0] // 2 if want_imag else g4.shape[0]
    y = jnp.dot(spec.astype(BF16), g2c, preferred_element_type=F32)
    y = y.astype(twr.dtype)
    out_re, out_im = [], []
    for c in range(0, G, 2):
        ya = y[c * Q:(c + 1) * Q]
        yb = y[(c + 1) * Q:(c + 2) * Q]
        yr = jnp.concatenate([ya[:, :P], yb[:, :P]], axis=1)
        yi = jnp.concatenate([ya[:, P:], yb[:, P:]], axis=1)
        zr = yr * twr + yi * twi
        zi = yi * twr - yr * twi
        rhs = jnp.concatenate([zr, zi], axis=0).astype(BF16)
        o = jnp.dot(g4, rhs, preferred_element_type=F32)
        out_re += [o[:S, :P], o[:S, P:]]
        if want_imag:
            out_im += [o[S:, :P], o[S:, P:]]
    return out_re, out_im


def _hyena_body(v_ref, x1_ref, x2_ref, af_ref, ab_ref, wof_ref, wob_ref, delta_ref, tf_ref, tb_ref, skip_ref,
                g1_ref, g1f_ref, twr_ref, twi_ref, g2_ref, g2c_ref, g4_ref, o_ref, taps_ref, kf_ref,
                *, S, Q, paired):
    g1, twr, twi = g1_ref[...], twr_ref[...], twi_ref[...]
    g2, g2c, g4 = g2_ref[...], g2c_ref[...], g4_ref[...]
    nb = 2 if paired else 1
    cb = kf_ref.shape[1]
    n_groups = cb // CH_GROUP
    P = FFT_P
    G = CH_GROUP

    @pl.when(pl.program_id(1) == 0)
    def _():
        g1f = g1f_ref[...]
        L = S * P
        delta = delta_ref[...]
        halves = ((wof_ref, af_ref, jnp.exp(-(delta * tf_ref[...]))),
                  (wob_ref, ab_ref, jnp.where(lax.broadcasted_iota(jnp.int32, (cb, L), 1) == 0, 0.0,
                                              jnp.exp(-(delta * tb_ref[...])))))
        for half, (wo_ref, a_ref, decay) in enumerate(halves):
            wo = wo_ref[...].reshape(HY_ORDER * cb, HY_FILTER_W)
            h = lax.dot_general(wo, a_ref[...], (((1,), (1,)), ((), ())), preferred_element_type=F32)
            for o in range(HY_ORDER):
                taps_ref[o, :, half * L:(half + 1) * L] = h[o * cb:(o + 1) * cb] * decay

        def filter_group(gi, carry):
            c0 = pl.multiple_of(gi * G, G)
            for o in range(HY_ORDER):
                k = taps_ref[o, pl.ds(c0, G), :]
                norm = jnp.sum(jnp.abs(k), axis=-1, keepdims=True)
                tiles = _to_tiles(k, Q)
                spec = _fwd_fft([tiles[c] for c in range(G)], None, g1f, twr, twi, g2)
                kf = spec.reshape(G, Q, 2 * P) * (1.0 / norm)[:, :, None]
                kf_ref[o, pl.ds(c0, G)] = kf.astype(kf_ref.dtype)
            return carry

        lax.fori_loop(0, n_groups, filter_group, 0)

    def group(gi, carry):
        c0 = pl.multiple_of(gi * G, G)
        r0 = pl.multiple_of(gi * (G * V7X_SUBLANES), G * V7X_SUBLANES)

        def tiles(ref, b):
            blk = ref[b, :, pl.ds(r0, G * V7X_SUBLANES), :]
            return [blk[:, c * V7X_SUBLANES:(c + 1) * V7X_SUBLANES, :].reshape(S, P) for c in range(G)]

        z = [tiles(v_ref, b) for b in range(nb)]
        gates = [[tiles(x1_ref, b) for b in range(nb)], [tiles(x2_ref, b) for b in range(nb)]]
        for o in range(HY_ORDER):
            re = z[0]
            im = z[1] if paired else None
            spec = _fwd_fft(re, im, g1, twr, twi, g2)
            kf = kf_ref[o, pl.ds(c0, G)].reshape(G * Q, 2 * P)
            spec = spec.astype(kf.dtype)
            xr, xi = spec[:, :P], spec[:, P:]
            kr, ki = kf[:, :P], kf[:, P:]
            prod = jnp.concatenate([xr * kr - xi * ki, xr * ki + xi * kr], axis=1)
            out_re, out_im = _inv_fft(prod, g2c, twr, twi, g4, paired)
            skip = skip_ref[o, pl.ds(c0, G)]
            conv = [out_re] + ([out_im] if paired else [])
            z = [[gates[o][b][c] * (conv[b][c] + skip[c] * z[b][c]) for c in range(G)] for b in range(nb)]
        for b in range(nb):
            o_ref[b, pl.ds(c0, G), :] = _from_tiles(jnp.stack(z[b], axis=0))
        return carry

    lax.fori_loop(0, n_groups, group, 0)


def hyena_operator(u_tiles, a, f_wout, skip, tabs, *, cb):
    B, nt, rows, P = u_tiles.shape
    D = rows // (3 * V7X_SUBLANES)
    L = nt * IN_BT
    S = L // P
    Q = tabs["Q"]
    W = HY_FILTER_W
    paired = B % 2 == 0
    nb = 2 if paired else 1
    wo = f_wout.T.reshape(HY_ORDER, 2, D, W).astype(BF16)
    wo_f, wo_b = wo[:, 0], wo[:, 1]
    deltas = np.abs(np.linspace(math.log(HY_TARGET) / HY_SLOW_PCT,
                                math.log(HY_TARGET) / HY_FAST_PCT, D)).astype(np.float32).reshape(D, 1)
    t_f = jnp.linspace(0.0, 1.0, L, dtype=F32).reshape(1, L)
    a_b = jnp.roll(jnp.flip(a, axis=0), 1, axis=0)
    t_b = jnp.roll(jnp.flip(t_f, axis=1), 1, axis=1)
    skip_rows = jnp.broadcast_to(skip.astype(F32)[:, :, None, None], (HY_ORDER, D, 1, P))

    bf = lambda name: jnp.asarray(tabs[name]).astype(BF16)
    g1, g1f, g2, g2c, g4 = bf("g1"), bf("g1_full"), bf("g2"), bf("g2c"), bf("g4")
    twr, twi = bf("twr"), bf("twi")
    const = lambda arr: pl.BlockSpec(arr.shape, lambda c, p: (0,) * arr.ndim)
    ncb = D // cb
    return pl.pallas_call(
        functools.partial(_hyena_body, S=S, Q=Q, paired=paired),
        grid=(ncb, B // nb),
        in_specs=[
            pl.BlockSpec((nb, nt, cb * V7X_SUBLANES, P), lambda c, p: (p, 0, c, 0)),
            pl.BlockSpec((nb, nt, cb * V7X_SUBLANES, P), lambda c, p: (p, 0, c + ncb, 0)),
            pl.BlockSpec((nb, nt, cb * V7X_SUBLANES, P), lambda c, p: (p, 0, c + 2 * ncb, 0)),
            const(a), const(a_b),
            pl.BlockSpec((HY_ORDER, cb, W), lambda c, p: (0, c, 0)),
            pl.BlockSpec((HY_ORDER, cb, W), lambda c, p: (0, c, 0)),
            pl.BlockSpec((cb, 1), lambda c, p: (c, 0)),
            const(t_f), const(t_b),
            pl.BlockSpec((HY_ORDER, cb, 1, P), lambda c, p: (0, c, 0, 0)),
            const(g1), const(g1f), const(twr), const(twi), const(g2), const(g2c), const(g4),
        ],
        out_specs=pl.BlockSpec((nb, cb, L), lambda c, p: (p, c, 0)),
        out_shape=jax.ShapeDtypeStruct((B, D, L), F32),
        scratch_shapes=[pltpu.VMEM((HY_ORDER, cb, 2 * L), F32), pltpu.VMEM((HY_ORDER, cb, Q, 2 * P), BF16)],
        compiler_params=_cparams("parallel", "arbitrary"),
        name="hyena_operator",
    )(u_tiles, u_tiles, u_tiles, a, a_b, wo_f, wo_b, jnp.asarray(deltas), t_f, t_b, skip_rows,
      g1, g1f, twr, twi, g2, g2c, g4)


def _trunk(x, p, cfg):
    B, L, D = x.shape
    tabs = _dft_tables(L, paired=(B % 2 == 0))
    for i in range(DEPTH):
        j = i // N_MIXERS
        if i % N_MIXERS == 0:
            a = hyena_filter_mlp(L, p["hy_f_w1"][j], p["hy_f_b1"][j], p["hy_f_w2"][j], p["hy_f_b2"][j],
                                 p["hy_f_w3"][j], p["hy_f_b3"][j], p["hy_f_freq"][j])
            u_tiles = hyena_in_projection(x, p["norm_mix_g"][i], p["hy_w_in_t"][j], p["hy_b_in"][j],
                                          p["hy_conv_w"][j], p["hy_conv_b"][j], bc=1024)
            z_t = hyena_operator(u_tiles, a, p["hy_f_wout"][j], p["hy_skip"][j], tabs, cb=cfg["hy_cb"])
            x = residual_matmul_t(x, z_t, p["hy_w_out"][j], p["hy_b_out"][j], bt=512)
        else:
            x2 = x.reshape(B * L, D)
            qkv = qkv_projection(x2, p["norm_mix_g"][i], p["at_w_qkv"][j], p["at_q_g"][j], p["at_k_g"][j], bm=1024)
            att = window_attention(qkv.reshape(B, L, -1), p["at_sink"][j], p["rel_bias"])
            x = residual_matmul(x2, att.reshape(B * L, -1), p["at_w_o"][j], bm=1024).reshape(B, L, D)
        x = ffn_block(x.reshape(B * L, D), p["norm_ffn_g"][i], p["ffn_w_gate_up"][i], p["ffn_w_down"][i],
                      bm=1024, bf=512).reshape(B, L, D)
    return x


def kernel(x_prompt, x_sample, norm_mix_g, norm_ffn_g, hy_w_in, hy_b_in, hy_conv_w, hy_conv_b, hy_f_w1, hy_f_b1,
           hy_f_w2, hy_f_b2, hy_f_w3, hy_f_b3, hy_f_wout, hy_f_freq, hy_skip, hy_w_out, hy_b_out, at_w_qkv, at_q_g,
           at_k_g, at_sink, at_w_o, rel_bias, ffn_w_gate_up, ffn_w_down):
    p = dict(
        norm_mix_g=norm_mix_g.astype(F32), norm_ffn_g=norm_ffn_g.astype(F32),
        hy_w_in_t=jnp.swapaxes(hy_w_in, 1, 2).astype(BF16), hy_b_in=hy_b_in,
        hy_conv_w=hy_conv_w, hy_conv_b=hy_conv_b,
        hy_f_w1=hy_f_w1, hy_f_b1=hy_f_b1, hy_f_w2=hy_f_w2, hy_f_b2=hy_f_b2, hy_f_w3=hy_f_w3, hy_f_b3=hy_f_b3,
        hy_f_wout=hy_f_wout, hy_f_freq=hy_f_freq, hy_skip=hy_skip,
        hy_w_out=hy_w_out.astype(BF16), hy_b_out=hy_b_out,
        at_w_qkv=at_w_qkv.astype(BF16), at_q_g=at_q_g, at_k_g=at_k_g, at_sink=at_sink,
        at_w_o=at_w_o.astype(BF16), rel_bias=rel_bias,
        ffn_w_gate_up=ffn_w_gate_up.astype(BF16), ffn_w_down=ffn_w_down.astype(BF16),
    )
    y_prompt = _trunk(x_prompt, p, dict(hy_cb=32))
    y_sample = _trunk(x_sample, p, dict(hy_cb=16))
    return (y_prompt, y_sample)
```

```python
import functools
import math

import jax
import jax.numpy as jnp
import numpy as np
from jax import lax
from jax.experimental import pallas as pl
from jax.experimental.pallas import tpu as pltpu

F32 = jnp.float32
BF16 = jnp.bfloat16

D_MODEL = 2048
DEPTH = 4
N_MIXERS = 2
HY_ORDER = 2
HY_EMB = 33
HY_BANDS = (HY_EMB - 1) // 2
HY_FILTER_W = 64
HY_FAST_PCT = 0.3
HY_SLOW_PCT = 1.5
HY_TARGET = 1e-2
N_HEADS = 16
HEAD_DIM = 128
N_KV = 4
GQA_G = N_HEADS // N_KV
WINDOW = 128
BLOCK = 128
N_BUCKETS = 32
MAX_DIST = 128
D_FF = -(-(8 * D_MODEL) // (3 * 256)) * 256
EPS = 1e-6
NEG = -1e30

V7X_LANES = 128
V7X_SUBLANES = 8
VMEM_LIMIT = 56 * 1024 * 1024

FFT_P = V7X_LANES
CH_GROUP = 4 * V7X_SUBLANES


def _cparams(*sem):
    return pltpu.CompilerParams(dimension_semantics=sem, vmem_limit_bytes=VMEM_LIMIT)


def _rms_bf16(x, g):
    ms = jnp.mean(x * x, axis=-1, keepdims=True)
    return (x * lax.rsqrt(ms + EPS) * g).astype(BF16)


def _qkv_proj_body(x_ref, g_ref, w_ref, qg_ref, kg_ref, o_ref):
    hn = _rms_bf16(x_ref[...], g_ref[...])
    nq, nk = N_HEADS * HEAD_DIM, N_KV * HEAD_DIM
    pair = 2 * HEAD_DIM
    for c0 in range(0, nq + nk, pair):
        acc = jnp.dot(hn, w_ref[:, c0:c0 + pair], preferred_element_type=F32)
        gain = qg_ref[...] if c0 < nq else kg_ref[...]
        for c in (0, HEAD_DIM):
            t = acc[:, c:c + HEAD_DIM]
            ms = jnp.mean(t * t, axis=-1, keepdims=True)
            o_ref[:, c0 + c:c0 + c + HEAD_DIM] = (t * lax.rsqrt(ms + EPS) * gain).astype(o_ref.dtype)
    o_ref[:, nq + nk:] = jnp.dot(hn, w_ref[:, nq + nk:], preferred_element_type=F32).astype(o_ref.dtype)


def qkv_projection(x, g, w, q_g, k_g, *, bm):
    T, K = x.shape
    N = w.shape[1]
    return pl.pallas_call(
        _qkv_proj_body,
        grid=(T // bm,),
        in_specs=[
            pl.BlockSpec((bm, K), lambda i: (i, 0)),
            pl.BlockSpec((1, K), lambda i: (0, 0)),
            pl.BlockSpec((K, N), lambda i: (0, 0), pipeline_mode=pl.Buffered(1)),
            pl.BlockSpec((1, HEAD_DIM), lambda i: (0, 0)),
            pl.BlockSpec((1, HEAD_DIM), lambda i: (0, 0)),
        ],
        out_specs=pl.BlockSpec((bm, N), lambda i: (i, 0)),
        out_shape=jax.ShapeDtypeStruct((T, N), BF16),
        compiler_params=_cparams("parallel"),
        name="qkv_projection",
    )(x, g.reshape(1, K), w, q_g.astype(F32).reshape(1, HEAD_DIM), k_g.astype(F32).reshape(1, HEAD_DIM))


IN_BT = V7X_SUBLANES * V7X_LANES
IN_ROW_SPLIT = 4


def _hyena_edge_body(x_ref, g_ref, w_ref, b_ref, o_ref):
    hn = _rms_bf16(x_ref[0], g_ref[...])
    o_ref[0] = lax.dot_general(w_ref[...], hn, (((1,), (1,)), ((), ())), preferred_element_type=F32) + b_ref[...]


def hyena_edge_projection(x_edge, g, w_t, bias, *, bc):
    B, n, K = x_edge.shape
    C = w_t.shape[0]
    return pl.pallas_call(
        _hyena_edge_body,
        grid=(B, C // bc),
        in_specs=[
            pl.BlockSpec((1, n, K), lambda b, c: (b, 0, 0)),
            pl.BlockSpec((1, K), lambda b, c: (0, 0)),
            pl.BlockSpec((bc, K), lambda b, c: (c, 0)),
            pl.BlockSpec((bc, 1), lambda b, c: (c, 0)),
        ],
        out_specs=pl.BlockSpec((1, bc, n), lambda b, c: (b, c, 0)),
        out_shape=jax.ShapeDtypeStruct((B, C, n), F32),
        compiler_params=_cparams("parallel", "parallel"),
        name="hyena_edge_projection",
    )(x_edge, g.reshape(1, K), w_t, bias.astype(F32).reshape(C, 1))


def _hyena_in_body(x_ref, g_ref, w_ref, b_ref, k0_ref, k1_ref, k2_ref, kb_ref, edge_ref, o_ref, hn_ref):
    t = pl.program_id(1)
    nt = pl.num_programs(1)

    @pl.when(pl.program_id(2) == 0)
    def _():
        hn_ref[...] = _rms_bf16(x_ref[0], g_ref[...])

    P = V7X_LANES
    bt = hn_ref.shape[0]
    n_chunks = bt // P
    rows = w_ref.shape[0] // IN_ROW_SPLIT
    for r in range(IN_ROW_SPLIT):
        rs = slice(r * rows, (r + 1) * rows)
        u = lax.dot_general(w_ref[rs, :], hn_ref[...], (((1,), (1,)), ((), ())), preferred_element_type=F32)
        edge = edge_ref[0, rs, :]
        eidx = lax.broadcasted_iota(jnp.int32, edge.shape, 1)
        pick = lambda k: jnp.sum(jnp.where(eidx == k, edge, 0.0), axis=1, keepdims=True)
        left = jnp.where(t == 0, 0.0, pick(t))
        right = jnp.where(t == nt - 1, 0.0, pick(nt + t))
        lane = lax.broadcasted_iota(jnp.int32, (rows, P), 1)
        bias, k0, k1, k2, kb = (ref[rs, :] for ref in (b_ref, k0_ref, k1_ref, k2_ref, kb_ref))
        chunks = [u[:, j * P:(j + 1) * P] + bias for j in range(n_chunks)]
        fwd = [pltpu.roll(c, 1, axis=1) for c in chunks]
        bwd = [pltpu.roll(c, P - 1, axis=1) for c in chunks]
        for j in range(n_chunks):
            prev = jnp.where(lane == 0, fwd[j - 1] if j > 0 else left, fwd[j])
            nxt = jnp.where(lane == P - 1, bwd[j + 1] if j + 1 < n_chunks else right, bwd[j])
            o_ref[pl.ds(r * rows * n_chunks + j, rows, stride=n_chunks), :] = (
                k0 * prev + k1 * chunks[j] + k2 * nxt + kb)


def hyena_in_projection(x, g, w_t, bias, conv_w, conv_b, *, bc):
    B, L, K = x.shape
    C = w_t.shape[0]
    bt = IN_BT
    nt = L // bt
    xb = x.reshape(B, nt, bt, K)
    x_edge = jnp.concatenate([jnp.roll(xb[:, :, bt - 1], 1, axis=1), jnp.roll(xb[:, :, 0], -1, axis=1)], axis=1)
    edge = hyena_edge_projection(x_edge, g, w_t, bias, bc=bc)
    col = lambda v: jnp.broadcast_to(v.astype(F32).reshape(C, 1), (C, V7X_LANES))
    cspec = pl.BlockSpec((bc, V7X_LANES), lambda b, t, c: (c, 0))
    return pl.pallas_call(
        _hyena_in_body,
        grid=(B, nt, C // bc),
        in_specs=[
            pl.BlockSpec((1, bt, K), lambda b, t, c: (b, t, 0)),
            pl.BlockSpec((1, K), lambda b, t, c: (0, 0)),
            pl.BlockSpec((bc, K), lambda b, t, c: (c, 0)),
            cspec, cspec, cspec, cspec, cspec,
            pl.BlockSpec((1, bc, 2 * nt), lambda b, t, c: (b, c, 0)),
        ],
        out_specs=pl.BlockSpec((None, None, bc * V7X_SUBLANES, V7X_LANES), lambda b, t, c: (b, t, c, 0)),
        out_shape=jax.ShapeDtypeStruct((B, nt, C * V7X_SUBLANES, V7X_LANES), F32),
        scratch_shapes=[pltpu.VMEM((bt, K), BF16)],
        compiler_params=_cparams("parallel", "parallel", "arbitrary"),
        name="hyena_in_projection",
    )(x, g.reshape(1, K), w_t, col(bias), col(conv_w[0]), col(conv_w[1]), col(conv_w[2]), col(conv_b), edge)


def _res_mm_body(a_ref, w_ref, x_ref, o_ref):
    o_ref[...] = x_ref[...] + jnp.dot(a_ref[...], w_ref[...], preferred_element_type=F32)


def residual_matmul(x, a, w, *, bm):
    T, K = a.shape
    N = w.shape[1]
    return pl.pallas_call(
        _res_mm_body,
        grid=(T // bm,),
        in_specs=[
            pl.BlockSpec((bm, K), lambda i: (i, 0)),
            pl.BlockSpec((K, N), lambda i: (0, 0), pipeline_mode=pl.Buffered(1)),
            pl.BlockSpec((bm, N), lambda i: (i, 0)),
        ],
        out_specs=pl.BlockSpec((bm, N), lambda i: (i, 0)),
        out_shape=jax.ShapeDtypeStruct((T, N), F32),
        compiler_params=_cparams("parallel"),
        name="residual_matmul",
    )(a, w, x)


def _res_mm_t_body(z_ref, w_ref, b_ref, x_ref, o_ref):
    z = z_ref[0].astype(BF16)
    y = lax.dot_general(z, w_ref[...], (((0,), (0,)), ((), ())), preferred_element_type=F32)
    o_ref[0] = x_ref[0] + y + b_ref[...]


def residual_matmul_t(x, z_t, w, bias, *, bt):
    B, K, L = z_t.shape
    N = w.shape[1]
    return pl.pallas_call(
        _res_mm_t_body,
        grid=(B, L // bt),
        in_specs=[
            pl.BlockSpec((1, K, bt), lambda b, t: (b, 0, t)),
            pl.BlockSpec((K, N), lambda b, t: (0, 0), pipeline_mode=pl.Buffered(1)),
            pl.BlockSpec((1, N), lambda b, t: (0, 0)),
            pl.BlockSpec((1, bt, N), lambda b, t: (b, t, 0)),
        ],
        out_specs=pl.BlockSpec((1, bt, N), lambda b, t: (b, t, 0)),
        out_shape=jax.ShapeDtypeStruct((B, L, N), F32),
        compiler_params=_cparams("parallel", "parallel"),
        name="residual_matmul_t",
    )(z_t, w, bias.reshape(1, N), x)


def _ffn_body(x_ref, g_ref, wg_ref, wu_ref, wd_ref, o_ref, hn_ref):
    @pl.when(pl.program_id(1) == 0)
    def _():
        x = x_ref[...]
        hn_ref[...] = _rms_bf16(x, g_ref[...])
        o_ref[...] = x

    h = hn_ref[...]
    gate = jnp.dot(h, wg_ref[...], preferred_element_type=F32)
    up = jnp.dot(h, wu_ref[...], preferred_element_type=F32)
    act = (gate * jax.nn.sigmoid(gate) * up).astype(BF16)
    o_ref[...] += jnp.dot(act, wd_ref[...], preferred_element_type=F32)


def ffn_block(x, g, w_gate_up, w_down, *, bm, bf):
    T, K = x.shape
    nf = D_FF // bf
    return pl.pallas_call(
        _ffn_body,
        grid=(T // bm, nf),
        in_specs=[
            pl.BlockSpec((bm, K), lambda i, f: (i, 0)),
            pl.BlockSpec((1, K), lambda i, f: (0, 0)),
            pl.BlockSpec((K, bf), lambda i, f: (0, f)),
            pl.BlockSpec((K, bf), lambda i, f: (0, f + nf)),
            pl.BlockSpec((bf, K), lambda i, f: (f, 0)),
        ],
        out_specs=pl.BlockSpec((bm, K), lambda i, f: (i, 0)),
        out_shape=jax.ShapeDtypeStruct((T, K), F32),
        scratch_shapes=[pltpu.VMEM((bm, K), BF16)],
        compiler_params=_cparams("parallel", "arbitrary"),
        name="ffn_block",
    )(x, g.reshape(1, K), w_gate_up, w_gate_up, w_down)


ATT_TQ = 4 * BLOCK
KV_W = N_KV * HEAD_DIM


def _band_structure():
    qi = np.arange(BLOCK)[:, None]
    ki = np.arange(3 * BLOCK)[None, :]
    rel = ki - BLOCK - qi
    nb = N_BUCKETS // 2
    max_exact = nb // 2
    n = np.abs(rel)
    large = max_exact + (np.log(np.maximum(n, 1) / max_exact) / math.log(MAX_DIST / max_exact)
                         * (nb - max_exact)).astype(np.int32)
    large = np.minimum(large, nb - 1)
    buckets = (rel > 0).astype(np.int32) * nb + np.where(n < max_exact, n, large).astype(np.int32)
    band = n <= WINDOW
    return buckets, band


def _attn_body(q_ref, kp_ref, kc_ref, kn_ref, vp_ref, vc_ref, vn_ref, bias_ref, sink_ref,
               o_ref, kbuf, vbuf):
    i = pl.program_id(1)
    last = pl.num_programs(1) - 1

    kbuf[0:BLOCK] = kp_ref[0]
    kbuf[BLOCK:BLOCK + ATT_TQ] = kc_ref[0]
    kbuf[BLOCK + ATT_TQ:] = kn_ref[0]
    ones = jnp.ones((ATT_TQ + 2 * BLOCK, HEAD_DIM), BF16)
    for g in range(N_KV):
        src = slice(g * HEAD_DIM, (g + 1) * HEAD_DIM)
        dst = slice(2 * g * HEAD_DIM, (2 * g + 1) * HEAD_DIM)
        vbuf[0:BLOCK, dst] = vp_ref[0, :, src]
        vbuf[BLOCK:BLOCK + ATT_TQ, dst] = vc_ref[0, :, src]
        vbuf[BLOCK + ATT_TQ:, dst] = vn_ref[0, :, src]
        vbuf[:, (2 * g + 1) * HEAD_DIM:(2 * g + 2) * HEAD_DIM] = ones

    lane = lax.broadcasted_iota(jnp.int32, (1, 3 * BLOCK), 1)
    first_edge = jnp.where((lane < BLOCK) & (i == 0), NEG, 0.0).astype(F32)
    last_edge = jnp.where((lane >= 2 * BLOCK) & (i == last), NEG, 0.0).astype(F32)
    exp2_scale = HEAD_DIM ** -0.5 * math.log2(math.e)

    n_sub = ATT_TQ // BLOCK
    groups = range(N_KV)

    def scores(j):
        r0 = j * BLOCK
        s = []
        for g in groups:
            qs = jnp.concatenate(
                [q_ref[0, r0:r0 + BLOCK, (GQA_G * g + h) * HEAD_DIM:(GQA_G * g + h + 1) * HEAD_DIM]
                 for h in range(GQA_G)], axis=0)
            kw = kbuf[r0:r0 + 3 * BLOCK, g * HEAD_DIM:(g + 1) * HEAD_DIM]
            sg = lax.dot_general(qs, kw, (((1,), (1,)), ((), ())), preferred_element_type=F32)
            sg = sg + bias_ref[g]
            if j == 0:
                sg = sg + first_edge
            if j == n_sub - 1:
                sg = sg + last_edge
            s.append(sg)
        return s

    def softmax_numerators(s):
        sk = [sink_ref[g][:, 0:1] for g in groups]
        m = [jnp.maximum(jnp.max(s[g], axis=-1, keepdims=True), sk[g]) for g in groups]
        p = [jnp.exp2((s[g] - m[g]) * exp2_scale).astype(BF16) for g in groups]
        sink_p = [jnp.exp2((sk[g] - m[g]) * exp2_scale) for g in groups]
        return p, sink_p

    def outputs(j, p, sink_p):
        r0 = j * BLOCK
        for g in groups:
            vw = vbuf[r0:r0 + 3 * BLOCK, 2 * g * HEAD_DIM:(2 * g + 2) * HEAD_DIM]
            pv = jnp.dot(p[g], vw, preferred_element_type=F32)
            o = pv[:, :HEAD_DIM] / (pv[:, HEAD_DIM:] + sink_p[g])
            for h in range(GQA_G):
                c0 = (GQA_G * g + h) * HEAD_DIM
                o_ref[0, r0:r0 + BLOCK, c0:c0 + HEAD_DIM] = o[h * BLOCK:(h + 1) * BLOCK].astype(o_ref.dtype)

    s_q, p_q = {}, {}
    for step in range(n_sub + 2):
        if step >= 2:
            outputs(step - 2, *p_q.pop(step - 2))
        if 1 <= step <= n_sub:
            p_q[step - 1] = softmax_numerators(s_q.pop(step - 1))
        if step < n_sub:
            s_q[step] = scores(step)


def window_attention(qkv, sink, rel_bias):
    B, L, _ = qkv.shape
    nq = N_HEADS * HEAD_DIM
    sub = ATT_TQ // BLOCK
    nblk = L // BLOCK
    kcol = nq // KV_W
    vcol = kcol + 1

    buckets, band = _band_structure()
    bias = rel_bias[buckets].astype(F32)
    inv_scale = HEAD_DIM ** 0.5
    bias = jnp.where(band[:, :, None], bias * inv_scale, NEG)
    bias = jnp.transpose(bias, (2, 0, 1)).reshape(N_KV, GQA_G * BLOCK, 3 * BLOCK)
    sink_rows = jnp.broadcast_to((sink.astype(F32) * inv_scale).reshape(N_KV, GQA_G, 1, 1),
                                 (N_KV, GQA_G, BLOCK, V7X_LANES)).reshape(N_KV, GQA_G * BLOCK, V7X_LANES)

    return pl.pallas_call(
        _attn_body,
        grid=(B, L // ATT_TQ),
        in_specs=[
            pl.BlockSpec((1, ATT_TQ, nq), lambda b, i: (b, i, 0)),
            pl.BlockSpec((1, BLOCK, KV_W), lambda b, i: (b, jnp.maximum(sub * i - 1, 0), kcol)),
            pl.BlockSpec((1, ATT_TQ, KV_W), lambda b, i: (b, i, kcol)),
            pl.BlockSpec((1, BLOCK, KV_W), lambda b, i: (b, jnp.minimum(sub * i + sub, nblk - 1), kcol)),
            pl.BlockSpec((1, BLOCK, KV_W), lambda b, i: (b, jnp.maximum(sub * i - 1, 0), vcol)),
            pl.BlockSpec((1, ATT_TQ, KV_W), lambda b, i: (b, i, vcol)),
            pl.BlockSpec((1, BLOCK, KV_W), lambda b, i: (b, jnp.minimum(sub * i + sub, nblk - 1), vcol)),
            pl.BlockSpec((N_KV, GQA_G * BLOCK, 3 * BLOCK), lambda b, i: (0, 0, 0)),
            pl.BlockSpec((N_KV, GQA_G * BLOCK, V7X_LANES), lambda b, i: (0, 0, 0)),
        ],
        out_specs=pl.BlockSpec((1, ATT_TQ, nq), lambda b, i: (b, i, 0)),
        out_shape=jax.ShapeDtypeStruct((B, L, nq), BF16),
        scratch_shapes=[pltpu.VMEM((ATT_TQ + 2 * BLOCK, KV_W), BF16),
                        pltpu.VMEM((ATT_TQ + 2 * BLOCK, 2 * KV_W), BF16)],
        compiler_params=_cparams("parallel", "parallel"),
        name="window_attention",
    )(qkv, qkv, qkv, qkv, qkv, qkv, qkv, bias, sink_rows)


FEAT_PAD = V7X_LANES


def _filter_mlp_body(feat_ref, w1_ref, b1_ref, w2_ref, b2_ref, w3_ref, b3_ref, fr_ref, a_ref):
    fr = fr_ref[...]
    a = jnp.sin(fr * (jnp.dot(feat_ref[...], w1_ref[...], preferred_element_type=F32) + b1_ref[...]))
    a = jnp.sin(fr * (jnp.dot(a.astype(BF16), w2_ref[...], preferred_element_type=F32) + b2_ref[...]))
    a = jnp.sin(fr * (jnp.dot(a.astype(BF16), w3_ref[...], preferred_element_type=F32) + b3_ref[...]))
    a_ref[...] = a.astype(BF16)


def hyena_filter_mlp(L, f_w1, f_b1, f_w2, f_b2, f_w3, f_b3, f_freq, *, bt=1024):
    t = jnp.linspace(0.0, 1.0, L, dtype=F32)[:, None]
    w = 2.0 * math.pi * jnp.arange(L, dtype=F32) / L
    f = jnp.linspace(1e-4, HY_BANDS - 1, HY_BANDS, dtype=F32)
    ang = w[:, None] * f[None, :]
    feats = jnp.concatenate([t, jnp.cos(ang), -jnp.sin(ang), jnp.zeros((L, FEAT_PAD - HY_EMB), F32)], axis=-1)
    w1 = jnp.concatenate([f_w1.astype(F32), jnp.zeros((FEAT_PAD - HY_EMB, HY_FILTER_W), F32)], axis=0)
    W = HY_FILTER_W
    const = lambda r, c: pl.BlockSpec((r, c), lambda i: (0, 0))
    return pl.pallas_call(
        _filter_mlp_body,
        grid=(L // bt,),
        in_specs=[pl.BlockSpec((bt, FEAT_PAD), lambda i: (i, 0)), const(FEAT_PAD, W), const(1, W), const(W, W),
                  const(1, W), const(W, W), const(1, W), const(1, W)],
        out_specs=pl.BlockSpec((bt, W), lambda i: (i, 0)),
        out_shape=jax.ShapeDtypeStruct((L, W), BF16),
        compiler_params=_cparams("parallel"),
        name="hyena_filter_mlp",
    )(feats.astype(BF16), w1.astype(BF16), f_b1.reshape(1, W), f_w2.astype(BF16), f_b2.reshape(1, W),
      f_w3.astype(BF16), f_b3.reshape(1, W), f_freq.astype(F32).reshape(1, W))


def _dft_tables(L, paired):
    N = 2 * L
    P = FFT_P
    Q = N // P
    S = L // P
    b = np.arange(Q, dtype=np.float64)
    fq = np.exp(-2j * np.pi * np.outer(b, np.arange(S)) / Q)
    fp =np.exp(-2j * np.pi * np.outer(np.arange(P), np.arange(P)) / P)
    tw = np.exp(-2j * np.pi * np.outer(b, np.arange(P)) / N)
    ci = np.conj(fq).T / N

    def stack(c):
        return np.block([[c.real, -c.imag], [c.imag, c.real]])

    if paired:
        g1 = stack(fq)
        g4 = stack(ci)
    else:
        g1 = np.concatenate([fq.real, fq.imag], axis=0)
        g4 = np.concatenate([ci.real, -ci.imag], axis=1)
    fq_full = np.exp(-2j * np.pi * np.outer(b, np.arange(Q)) / Q)
    g1_full = np.concatenate([fq_full.real, fq_full.imag], axis=0)
    g2 =np.block([[fp.real, fp.imag], [-fp.imag, fp.real]])
    g2c = np.block([[fp.real, -fp.imag], [fp.imag, fp.real]])
    twr = np.tile(tw.real, (1, 2))
    twi = np.tile(tw.imag, (1, 2))
    f32 = lambda a: np.ascontiguousarray(a, dtype=np.float32)
    return dict(g1=f32(g1), g4=f32(g4), g1_full=f32(g1_full), g2=f32(g2), g2c=f32(g2c),
                twr=f32(twr), twi=f32(twi), Q=Q, S=S, N=N)


def _to_tiles(x, n_tiles):
    chunks = [x[:, s * FFT_P:(s + 1) * FFT_P] for s in range(n_tiles)]
    return jnp.swapaxes(jnp.stack(chunks, axis=0), 0, 1)


def _from_tiles(x):
    y = jnp.swapaxes(x, 0, 1)
    return jnp.concatenate([y[s] for s in range(y.shape[0])], axis=1)


def _fwd_fft(re_tiles, im_tiles, g1, twr, twi, g2):
    G = len(re_tiles)
    Q = twr.shape[0]
    P = FFT_P
    rows = []
    for c in range(0, G, 2):
        top = jnp.concatenate([re_tiles[c], re_tiles[c + 1]], axis=1)
        if im_tiles is None:
            rhs = top
        else:
            rhs = jnp.concatenate([top, jnp.concatenate([im_tiles[c], im_tiles[c + 1]], axis=1)], axis=0)
        y = jnp.dot(g1, rhs.astype(BF16), preferred_element_type=F32)
        y = y.astype(twr.dtype)
        yr, yi = y[:Q], y[Q:]
        zr = yr * twr - yi * twi
        zi = yr * twi + yi * twr
        rows.append(jnp.concatenate([zr[:, :P], zi[:, :P]], axis=1))
        rows.append(jnp.concatenate([zr[:, P:], zi[:, P:]], axis=1))
    lhs = jnp.concatenate(rows, axis=0).astype(BF16)
    return jnp.dot(lhs, g2, preferred_element_type=F32)


def _inv_fft(spec, g2c, twr, twi, g4, want_imag):
    Q = twr.shape[0]
    P = FFT_P
    G = spec.shape[0] // Q
    S = g4.shape[0] // 2 if want_imag else g4.shape[0]
    y = jnp.dot(spec.astype(BF16), g2c, preferred_element_type=F32)
    y = y.astype(twr.dtype)
    out_re, out_im = [], []
    for c in range(0, G, 2):
        ya = y[c * Q:(c + 1) * Q]
        yb = y[(c + 1) * Q:(c + 2) * Q]
        yr = jnp.concatenate([ya[:, :P], yb[:, :P]], axis=1)
        yi = jnp.concatenate([ya[:, P:], yb[:, P:]], axis=1)
        zr = yr * twr + yi * twi
        zi = yi * twr - yr * twi
        rhs = jnp.concatenate([zr, zi], axis=0).astype(BF16)
        o = jnp.dot(g4, rhs, preferred_element_type=F32)
        out_re += [o[:S, :P], o[:S, P:]]
        if want_imag:
            out_im += [o[S:, :P], o[S:, P:]]
    return out_re, out_im


def _hyena_body(v_ref, x1_ref, x2_ref, af_ref, ab_ref, wof_ref, wob_ref, delta_ref, tf_ref, tb_ref, skip_ref,
                g1_ref, g1f_ref, twr_ref, twi_ref, g2_ref, g2c_ref, g4_ref, o_ref, taps_ref, kf_ref,
                *, S, Q, paired):
    g1, twr, twi = g1_ref[...], twr_ref[...], twi_ref[...]
    g2, g2c, g4 = g2_ref[...], g2c_ref[...], g4_ref[...]
    nb = 2 if paired else 1
    cb = kf_ref.shape[1]
    G = min(cb, CH_GROUP)
    n_groups = cb // G
    P = FFT_P

    @pl.when(pl.program_id(1) == 0)
    def _():
        g1f = g1f_ref[...]
        L = S * P
        delta = delta_ref[...]
        halves = ((wof_ref, af_ref, jnp.exp(-(delta * tf_ref[...]))),
                  (wob_ref, ab_ref, jnp.where(lax.broadcasted_iota(jnp.int32, (cb, L), 1) == 0, 0.0,
                                              jnp.exp(-(delta * tb_ref[...])))))
        for half, (wo_ref, a_ref, decay) in enumerate(halves):
            wo = wo_ref[...].reshape(HY_ORDER * cb, HY_FILTER_W)
            h = lax.dot_general(wo, a_ref[...], (((1,), (1,)), ((), ())), preferred_element_type=F32)
            for o in range(HY_ORDER):
                taps_ref[o, :, half * L:(half + 1) * L] = h[o * cb:(o + 1) * cb] * decay

        def filter_group(gi, carry):
            c0 = pl.multiple_of(gi * G, G)
            for o in range(HY_ORDER):
                k = taps_ref[o, pl.ds(c0, G), :]
                norm = jnp.sum(jnp.abs(k), axis=-1, keepdims=True)
                tiles = _to_tiles(k, Q)
                spec = _fwd_fft([tiles[c] for c in range(G)], None, g1f, twr, twi, g2)
                kf = spec.reshape(G, Q, 2 * P) * (1.0 / norm)[:, :, None]
                kf_ref[o, pl.ds(c0, G)] = kf.astype(kf_ref.dtype)
            return carry

        lax.fori_loop(0, n_groups, filter_group, 0)

    def group(gi, carry):
        c0 = pl.multiple_of(gi * G, G)
        r0 = pl.multiple_of(gi * (G * V7X_SUBLANES), G * V7X_SUBLANES)

        def tiles(ref, b):
            blk = ref[b, :, pl.ds(r0, G * V7X_SUBLANES), :]
            return [blk[:, c * V7X_SUBLANES:(c + 1) * V7X_SUBLANES, :].reshape(S, P) for c in range(G)]

        z = [tiles(v_ref, b) for b in range(nb)]
        gates = [[tiles(x1_ref, b) for b in range(nb)], [tiles(x2_ref, b) for b in range(nb)]]
        for o in range(HY_ORDER):
            re = z[0]
            im = z[1] if paired else None
            spec = _fwd_fft(re, im, g1, twr, twi, g2)
            kf = kf_ref[o, pl.ds(c0, G)].reshape(G * Q, 2 * P)
            spec = spec.astype(kf.dtype)
            xr, xi = spec[:, :P], spec[:, P:]
            kr, ki = kf[:, :P], kf[:, P:]
            prod = jnp.concatenate([xr * kr - xi * ki, xr * ki + xi * kr], axis=1)
            out_re, out_im = _inv_fft(prod, g2c, twr, twi, g4, paired)
            skip = skip_ref[o, pl.ds(c0, G)]
            conv = [out_re] + ([out_im] if paired else [])
            z = [[gates[o][b][c] * (conv[b][c] + skip[c] * z[b][c]) for c in range(G)] for b in range(nb)]
        for b in range(nb):
            o_ref[b, pl.ds(c0, G), :] = _from_tiles(jnp.stack(z[b], axis=0))
        return carry

    lax.fori_loop(0, n_groups, group, 0)


def hyena_operator(u_tiles, a, f_wout, skip, tabs, *, cb):
    B, nt, rows, P = u_tiles.shape
    D = rows // (3 * V7X_SUBLANES)
    L = nt * IN_BT
    S = L // P
    Q = tabs["Q"]
    W = HY_FILTER_W
    paired = B % 2 == 0
    nb = 2 if paired else 1
    wo = f_wout.T.reshape(HY_ORDER, 2, D, W).astype(BF16)
    wo_f, wo_b = wo[:, 0], wo[:, 1]
    deltas = np.abs(np.linspace(math.log(HY_TARGET) / HY_SLOW_PCT,
                                math.log(HY_TARGET) / HY_FAST_PCT, D)).astype(np.float32).reshape(D, 1)
    t_f = jnp.linspace(0.0, 1.0, L, dtype=F32).reshape(1, L)
    a_b = jnp.roll(jnp.flip(a, axis=0), 1, axis=0)
    t_b = jnp.roll(jnp.flip(t_f, axis=1), 1, axis=1)
    skip_rows = jnp.broadcast_to(skip.astype(F32)[:, :, None, None], (HY_ORDER, D, 1, P))

    bf = lambda name: jnp.asarray(tabs[name]).astype(BF16)
    g1, g1f, g2, g2c, g4 = bf("g1"), bf("g1_full"), bf("g2"), bf("g2c"), bf("g4")
    twr, twi = bf("twr"), bf("twi")
    const = lambda arr: pl.BlockSpec(arr.shape, lambda c, p: (0,) * arr.ndim, pipeline_mode=pl.Buffered(1))
    ncb = D // cb
    return pl.pallas_call(
        functools.partial(_hyena_body, S=S, Q=Q, paired=paired),
        grid=(ncb, B // nb),
        in_specs=[
            pl.BlockSpec((nb, nt, cb * V7X_SUBLANES, P), lambda c, p: (p, 0, c, 0)),
            pl.BlockSpec((nb, nt, cb * V7X_SUBLANES, P), lambda c, p: (p, 0, c + ncb, 0)),
            pl.BlockSpec((nb, nt, cb * V7X_SUBLANES, P), lambda c, p: (p, 0, c + 2 * ncb, 0)),
            const(a), const(a_b),
            pl.BlockSpec((HY_ORDER, cb, W), lambda c, p: (0, c, 0)),
            pl.BlockSpec((HY_ORDER, cb, W), lambda c, p: (0, c, 0)),
            pl.BlockSpec((cb, 1), lambda c, p: (c, 0)),
            const(t_f), const(t_b),
            pl.BlockSpec((HY_ORDER, cb, 1, P), lambda c, p: (0, c, 0, 0)),
            const(g1), const(g1f), const(twr), const(twi), const(g2), const(g2c), const(g4),
        ],
        out_specs=pl.BlockSpec((nb, cb, L), lambda c, p: (p, c, 0)),
        out_shape=jax.ShapeDtypeStruct((B, D, L), F32),
        scratch_shapes=[pltpu.VMEM((HY_ORDER, cb, 2 * L), F32), pltpu.VMEM((HY_ORDER, cb, Q, 2 * P), BF16)],
        compiler_params=_cparams("parallel", "arbitrary"),
        name="hyena_operator",
    )(u_tiles, u_tiles, u_tiles, a, a_b, wo_f, wo_b, jnp.asarray(deltas), t_f, t_b, skip_rows,
      g1, g1f, twr, twi, g2, g2c, g4)


def _trunk(x, p, cfg):
    B, L, D = x.shape
    tabs = _dft_tables(L, paired=(B % 2 == 0))
    for i in range(DEPTH):
        j = i // N_MIXERS
        if i % N_MIXERS == 0:
            a = hyena_filter_mlp(L, p["hy_f_w1"][j], p["hy_f_b1"][j], p["hy_f_w2"][j], p["hy_f_b2"][j],
                                 p["hy_f_w3"][j], p["hy_f_b3"][j], p["hy_f_freq"][j])
            u_tiles = hyena_in_projection(x, p["norm_mix_g"][i], p["hy_w_in_t"][j], p["hy_b_in"][j],
                                          p["hy_conv_w"][j], p["hy_conv_b"][j], bc=1024)
            z_t = hyena_operator(u_tiles, a, p["hy_f_wout"][j], p["hy_skip"][j], tabs, cb=cfg["hy_cb"])
            x = residual_matmul_t(x, z_t, p["hy_w_out"][j], p["hy_b_out"][j], bt=512)
        else:
            x2 = x.reshape(B * L, D)
            qkv = qkv_projection(x2, p["norm_mix_g"][i], p["at_w_qkv"][j], p["at_q_g"][j], p["at_k_g"][j], bm=1024)
            att = window_attention(qkv.reshape(B, L, -1), p["at_sink"][j], p["rel_bias"])
            x = residual_matmul(x2, att.reshape(B * L, -1), p["at_w_o"][j], bm=1024).reshape(B, L, D)
        x = ffn_block(x.reshape(B * L, D), p["norm_ffn_g"][i], p["ffn_w_gate_up"][i], p["ffn_w_down"][i],
                      bm=1024, bf=512).reshape(B, L, D)
    return x


def kernel(x_prompt, x_sample, norm_mix_g, norm_ffn_g, hy_w_in, hy_b_in, hy_conv_w, hy_conv_b, hy_f_w1, hy_f_b1,
           hy_f_w2, hy_f_b2, hy_f_w3, hy_f_b3, hy_f_wout, hy_f_freq, hy_skip, hy_w_out, hy_b_out, at_w_qkv, at_q_g,
           at_k_g, at_sink, at_w_o, rel_bias, ffn_w_gate_up, ffn_w_down):
    p = dict(
        norm_mix_g=norm_mix_g.astype(F32), norm_ffn_g=norm_ffn_g.astype(F32),
        hy_w_in_t=jnp.swapaxes(hy_w_in, 1, 2).astype(BF16), hy_b_in=hy_b_in,
        hy_conv_w=hy_conv_w, hy_conv_b=hy_conv_b,
        hy_f_w1=hy_f_w1, hy_f_b1=hy_f_b1, hy_f_w2=hy_f_w2, hy_f_b2=hy_f_b2, hy_f_w3=hy_f_w3, hy_f_b3=hy_f_b3,
        hy_f_wout=hy_f_wout, hy_f_freq=hy_f_freq, hy_skip=hy_skip,
        hy_w_out=hy_w_out.astype(BF16), hy_b_out=hy_b_out,
        at_w_qkv=at_w_qkv.astype(BF16), at_q_g=at_q_g, at_k_g=at_k_g, at_sink=at_sink,
        at_w_o=at_w_o.astype(BF16), rel_bias=rel_bias,
        ffn_w_gate_up=ffn_w_gate_up.astype(BF16), ffn_w_down=ffn_w_down.astype(BF16),
    )
    y_prompt = _trunk(x_prompt, p, dict(hy_cb=32))
    y_sample = _trunk(x_sample, p, dict(hy_cb=16))
    return (y_prompt, y_sample)
```

```python
import functools
import math

import jax
import jax.numpy as jnp
import numpy as np
from jax import lax
from jax.experimental import pallas as pl
from jax.experimental.pallas import tpu as pltpu

F32 = jnp.float32
BF16 = jnp.bfloat16

D_MODEL = 2048
DEPTH = 4
N_MIXERS = 2
HY_ORDER = 2
HY_EMB = 33
HY_BANDS = (HY_EMB - 1) // 2
HY_FILTER_W = 64
HY_FAST_PCT = 0.3
HY_SLOW_PCT = 1.5
HY_TARGET = 1e-2
N_HEADS = 16
HEAD_DIM = 128
N_KV = 4
GQA_G = N_HEADS // N_KV
WINDOW = 128
BLOCK = 128
N_BUCKETS = 32
MAX_DIST = 128
D_FF = -(-(8 * D_MODEL) // (3 * 256)) * 256
EPS = 1e-6
NEG = -1e30

V7X_LANES = 128
V7X_SUBLANES = 8
VMEM_LIMIT = 56 * 1024 * 1024

FFT_P = V7X_LANES
CH_GROUP = 4 * V7X_SUBLANES


def _cparams(*sem):
    return pltpu.CompilerParams(dimension_semantics=sem, vmem_limit_bytes=VMEM_LIMIT)


def _rms_bf16(x, g):
    ms = jnp.mean(x * x, axis=-1, keepdims=True)
    return (x * lax.rsqrt(ms + EPS) * g).astype(BF16)


def _qkv_proj_body(x_ref, g_ref, w_ref, qg_ref, kg_ref, o_ref):
    hn = _rms_bf16(x_ref[...], g_ref[...])
    nq, nk = N_HEADS * HEAD_DIM, N_KV * HEAD_DIM
    pair = 2 * HEAD_DIM
    for c0 in range(0, nq + nk, pair):
        acc = jnp.dot(hn, w_ref[:, c0:c0 + pair], preferred_element_type=F32)
        gain = qg_ref[...] if c0 < nq else kg_ref[...]
        for c in (0, HEAD_DIM):
            t = acc[:, c:c + HEAD_DIM]
            ms = jnp.mean(t * t, axis=-1, keepdims=True)
            o_ref[:, c0 + c:c0 + c + HEAD_DIM] = (t * lax.rsqrt(ms + EPS) * gain).astype(o_ref.dtype)
    o_ref[:, nq + nk:] = jnp.dot(hn, w_ref[:, nq + nk:], preferred_element_type=F32).astype(o_ref.dtype)


def qkv_projection(x, g, w, q_g, k_g, *, bm):
    T, K = x.shape
    N = w.shape[1]
    return pl.pallas_call(
        _qkv_proj_body,
        grid=(T // bm,),
        in_specs=[
            pl.BlockSpec((bm, K), lambda i: (i, 0)),
            pl.BlockSpec((1, K), lambda i: (0, 0)),
            pl.BlockSpec((K, N), lambda i: (0, 0), pipeline_mode=pl.Buffered(1)),
            pl.BlockSpec((1, HEAD_DIM), lambda i: (0, 0)),
            pl.BlockSpec((1, HEAD_DIM), lambda i: (0, 0)),
        ],
        out_specs=pl.BlockSpec((bm, N), lambda i: (i, 0)),
        out_shape=jax.ShapeDtypeStruct((T, N), BF16),
        compiler_params=_cparams("parallel"),
        name="qkv_projection",
    )(x, g.reshape(1, K), w, q_g.astype(F32).reshape(1, HEAD_DIM), k_g.astype(F32).reshape(1, HEAD_DIM))


IN_BT = V7X_SUBLANES * V7X_LANES
IN_ROW_SPLIT = 4


def _hyena_edge_body(x_ref, g_ref, w_ref, b_ref, o_ref):
    hn = _rms_bf16(x_ref[0], g_ref[...])
    o_ref[0] = lax.dot_general(w_ref[...], hn, (((1,), (1,)), ((), ())), preferred_element_type=F32) + b_ref[...]


def hyena_edge_projection(x_edge, g, w_t, bias, *, bc):
    B, n, K = x_edge.shape
    C = w_t.shape[0]
    return pl.pallas_call(
        _hyena_edge_body,
        grid=(B, C // bc),
        in_specs=[
            pl.BlockSpec((1, n, K), lambda b, c: (b, 0, 0)),
            pl.BlockSpec((1, K), lambda b, c: (0, 0)),
            pl.BlockSpec((bc, K), lambda b, c: (c, 0)),
            pl.BlockSpec((bc, 1), lambda b, c: (c, 0)),
        ],
        out_specs=pl.BlockSpec((1, bc, n), lambda b, c: (b, c, 0)),
        out_shape=jax.ShapeDtypeStruct((B, C, n), F32),
        compiler_params=_cparams("parallel", "parallel"),
        name="hyena_edge_projection",
    )(x_edge, g.reshape(1, K), w_t, bias.astype(F32).reshape(C, 1))


def _hyena_in_body(x_ref, g_ref, w_ref, b_ref, k0_ref, k1_ref, k2_ref, kb_ref, edge_ref, o_ref, hn_ref):
    t = pl.program_id(1)
    nt = pl.num_programs(1)

    @pl.when(pl.program_id(2) == 0)
    def _():
        hn_ref[...] = _rms_bf16(x_ref[0], g_ref[...])

    P = V7X_LANES
    bt = hn_ref.shape[0]
    n_chunks = bt // P
    rows = w_ref.shape[0] // IN_ROW_SPLIT
    for r in range(IN_ROW_SPLIT):
        rs = slice(r * rows, (r + 1) * rows)
        u = lax.dot_general(w_ref[rs, :], hn_ref[...], (((1,), (1,)), ((), ())), preferred_element_type=F32)
        edge = edge_ref[0, rs, :]
        eidx = lax.broadcasted_iota(jnp.int32, edge.shape, 1)
        pick = lambda k: jnp.sum(jnp.where(eidx == k, edge, 0.0), axis=1, keepdims=True)
        left = jnp.where(t == 0, 0.0, pick(t))
        right = jnp.where(t == nt - 1, 0.0, pick(nt + t))
        lane = lax.broadcasted_iota(jnp.int32, (rows, P), 1)
        bias, k0, k1, k2, kb = (ref[rs, :] for ref in (b_ref, k0_ref, k1_ref, k2_ref, kb_ref))
        chunks = [u[:, j * P:(j + 1) * P] + bias for j in range(n_chunks)]
        fwd = [pltpu.roll(c, 1, axis=1) for c in chunks]
        bwd = [pltpu.roll(c, P - 1, axis=1) for c in chunks]
        for j in range(n_chunks):
            prev = jnp.where(lane == 0, fwd[j - 1] if j > 0 else left, fwd[j])
            nxt = jnp.where(lane == P - 1, bwd[j + 1] if j + 1 < n_chunks else right, bwd[j])
            o_ref[pl.ds(r * rows * n_chunks + j, rows, stride=n_chunks), :] = (
                k0 * prev + k1 * chunks[j] + k2 * nxt + kb)


def hyena_in_projection(x, g, w_t, bias, conv_w, conv_b, *, bc):
    B, L, K = x.shape
    C = w_t.shape[0]
    bt = IN_BT
    nt = L // bt
    xb = x.reshape(B, nt, bt, K)
    x_edge = jnp.concatenate([jnp.roll(xb[:, :, bt - 1], 1, axis=1), jnp.roll(xb[:, :, 0], -1, axis=1)], axis=1)
    edge = hyena_edge_projection(x_edge, g, w_t, bias, bc=bc)
    col = lambda v: jnp.broadcast_to(v.astype(F32).reshape(C, 1), (C, V7X_LANES))
    cspec = pl.BlockSpec((bc, V7X_LANES), lambda b, t, c: (c, 0))
    return pl.pallas_call(
        _hyena_in_body,
        grid=(B, nt, C // bc),
        in_specs=[
            pl.BlockSpec((1, bt, K), lambda b, t, c: (b, t, 0)),
            pl.BlockSpec((1, K), lambda b, t, c: (0, 0)),
            pl.BlockSpec((bc, K), lambda b, t, c: (c, 0)),
            cspec, cspec, cspec, cspec, cspec,
            pl.BlockSpec((1, bc, 2 * nt), lambda b, t, c: (b, c, 0)),
        ],
        out_specs=pl.BlockSpec((None, None, bc * V7X_SUBLANES, V7X_LANES), lambda b, t, c: (b, t, c, 0)),
        out_shape=jax.ShapeDtypeStruct((B, nt, C * V7X_SUBLANES, V7X_LANES), F32),
        scratch_shapes=[pltpu.VMEM((bt, K), BF16)],
        compiler_params=_cparams("parallel", "parallel", "arbitrary"),
        name="hyena_in_projection",
    )(x, g.reshape(1, K), w_t, col(bias), col(conv_w[0]), col(conv_w[1]), col(conv_w[2]), col(conv_b), edge)


def _res_mm_body(a_ref, w_ref, x_ref, o_ref):
    o_ref[...] = x_ref[...] + jnp.dot(a_ref[...], w_ref[...], preferred_element_type=F32)


def residual_matmul(x, a, w, *, bm):
    T, K = a.shape
    N = w.shape[1]
    return pl.pallas_call(
        _res_mm_body,
        grid=(T // bm,),
        in_specs=[
            pl.BlockSpec((bm, K), lambda i: (i, 0)),
            pl.BlockSpec((K, N), lambda i: (0, 0), pipeline_mode=pl.Buffered(1)),
            pl.BlockSpec((bm, N), lambda i: (i, 0)),
        ],
        out_specs=pl.BlockSpec((bm, N), lambda i: (i, 0)),
        out_shape=jax.ShapeDtypeStruct((T, N), F32),
        compiler_params=_cparams("parallel"),
        name="residual_matmul",
    )(a, w, x)


def _res_mm_t_body(z_ref, w_ref, b_ref, x_ref, o_ref):
    z = z_ref[0].astype(BF16)
    y = lax.dot_general(z, w_ref[...], (((0,), (0,)), ((), ())), preferred_element_type=F32)
    o_ref[0] = x_ref[0] + y + b_ref[...]


def residual_matmul_t(x, z_t, w, bias, *, bt):
    B, K, L = z_t.shape
    N = w.shape[1]
    return pl.pallas_call(
        _res_mm_t_body,
        grid=(B, L // bt),
        in_specs=[
            pl.BlockSpec((1, K, bt), lambda b, t: (b, 0, t)),
            pl.BlockSpec((K, N), lambda b, t: (0, 0), pipeline_mode=pl.Buffered(1)),
            pl.BlockSpec((1, N), lambda b, t: (0, 0)),
            pl.BlockSpec((1, bt, N), lambda b, t: (b, t, 0)),
        ],
        out_specs=pl.BlockSpec((1, bt, N), lambda b, t: (b, t, 0)),
        out_shape=jax.ShapeDtypeStruct((B, L, N), F32),
        compiler_params=_cparams("parallel", "parallel"),
        name="residual_matmul_t",
    )(z_t, w, bias.reshape(1, N), x)


def _ffn_body(x_ref, g_ref, wg_ref, wu_ref, wd_ref, o_ref, hn_ref):
    @pl.when(pl.program_id(1) == 0)
    def _():
        x = x_ref[...]
        hn_ref[...] = _rms_bf16(x, g_ref[...])
        o_ref[...] = x

    h = hn_ref[...]
    gate = jnp.dot(h, wg_ref[...], preferred_element_type=F32)
    up = jnp.dot(h, wu_ref[...], preferred_element_type=F32)
    act = (gate * jax.nn.sigmoid(gate) * up).astype(BF16)
    o_ref[...] += jnp.dot(act, wd_ref[...], preferred_element_type=F32)


def ffn_block(x, g, w_gate_up, w_down, *, bm, bf):
    T, K = x.shape
    nf = D_FF // bf
    return pl.pallas_call(
        _ffn_body,
        grid=(T // bm, nf),
        in_specs=[
            pl.BlockSpec((bm, K), lambda i, f: (i, 0)),
            pl.BlockSpec((1, K), lambda i, f: (0, 0)),
            pl.BlockSpec((K, bf), lambda i, f: (0, f)),
            pl.BlockSpec((K, bf), lambda i, f: (0, f + nf)),
            pl.BlockSpec((bf, K), lambda i, f: (f, 0)),
        ],
        out_specs=pl.BlockSpec((bm, K), lambda i, f: (i, 0)),
        out_shape=jax.ShapeDtypeStruct((T, K), F32),
        scratch_shapes=[pltpu.VMEM((bm, K), BF16)],
        compiler_params=_cparams("parallel", "arbitrary"),
        name="ffn_block",
    )(x, g.reshape(1, K), w_gate_up, w_gate_up, w_down)


ATT_TQ = 4 * BLOCK
KV_W = N_KV * HEAD_DIM


def _band_structure():
    qi = np.arange(BLOCK)[:, None]
    ki = np.arange(3 * BLOCK)[None, :]
    rel = ki - BLOCK - qi
    nb = N_BUCKETS // 2
    max_exact = nb // 2
    n = np.abs(rel)
    large = max_exact + (np.log(np.maximum(n, 1) / max_exact) / math.log(MAX_DIST / max_exact)
                         * (nb - max_exact)).astype(np.int32)
    large = np.minimum(large, nb - 1)
    buckets = (rel > 0).astype(np.int32) * nb + np.where(n < max_exact, n, large).astype(np.int32)
    band = n <= WINDOW
    return buckets, band


def _attn_body(q_ref, kp_ref, kc_ref, kn_ref, vp_ref, vc_ref, vn_ref, bias_ref, sink_ref,
               o_ref, kbuf, vbuf):
    i = pl.program_id(1)
    last = pl.num_programs(1) - 1

    kbuf[0:BLOCK] = kp_ref[0]
    kbuf[BLOCK:BLOCK + ATT_TQ] = kc_ref[0]
    kbuf[BLOCK + ATT_TQ:] = kn_ref[0]
    ones = jnp.ones((ATT_TQ + 2 * BLOCK, HEAD_DIM), BF16)
    for g in range(N_KV):
        src = slice(g * HEAD_DIM, (g + 1) * HEAD_DIM)
        dst = slice(2 * g * HEAD_DIM, (2 * g + 1) * HEAD_DIM)
        vbuf[0:BLOCK, dst] = vp_ref[0, :, src]
        vbuf[BLOCK:BLOCK + ATT_TQ, dst] = vc_ref[0, :, src]
        vbuf[BLOCK + ATT_TQ:, dst] = vn_ref[0, :, src]
        vbuf[:, (2 * g + 1) * HEAD_DIM:(2 * g + 2) * HEAD_DIM] = ones

    lane = lax.broadcasted_iota(jnp.int32, (1, 3 * BLOCK), 1)
    first_edge = jnp.where((lane < BLOCK) & (i == 0), NEG, 0.0).astype(F32)
    last_edge = jnp.where((lane >= 2 * BLOCK) & (i == last), NEG, 0.0).astype(F32)
    exp2_scale = HEAD_DIM ** -0.5 * math.log2(math.e)

    n_sub = ATT_TQ // BLOCK
    groups = range(N_KV)

    def scores(j):
        r0 = j * BLOCK
        s = []
        for g in groups:
            qs = jnp.concatenate(
                [q_ref[0, r0:r0 + BLOCK, (GQA_G * g + h) * HEAD_DIM:(GQA_G * g + h + 1) * HEAD_DIM]
                 for h in range(GQA_G)], axis=0)
            kw = kbuf[r0:r0 + 3 * BLOCK, g * HEAD_DIM:(g + 1) * HEAD_DIM]
            sg = lax.dot_general(qs, kw, (((1,), (1,)), ((), ())), preferred_element_type=F32)
            sg = sg + bias_ref[g]
            if j == 0:
                sg = sg + first_edge
            if j == n_sub - 1:
                sg = sg + last_edge
            s.append(sg)
        return s

    def softmax_numerators(s):
        sk = [sink_ref[g][:, 0:1] for g in groups]
        m = [jnp.maximum(jnp.max(s[g], axis=-1, keepdims=True), sk[g]) for g in groups]
        p = [jnp.exp2((s[g] - m[g]) * exp2_scale).astype(BF16) for g in groups]
        sink_p = [jnp.exp2((sk[g] - m[g]) * exp2_scale) for g in groups]
        return p, sink_p

    def outputs(j, p, sink_p):
        r0 = j * BLOCK
        for g in groups:
            vw = vbuf[r0:r0 + 3 * BLOCK, 2 * g * HEAD_DIM:(2 * g + 2) * HEAD_DIM]
            pv = jnp.dot(p[g], vw, preferred_element_type=F32)
            o = pv[:, :HEAD_DIM] / (pv[:, HEAD_DIM:] + sink_p[g])
            for h in range(GQA_G):
                c0 = (GQA_G * g + h) * HEAD_DIM
                o_ref[0, r0:r0 + BLOCK, c0:c0 + HEAD_DIM] = o[h * BLOCK:(h + 1) * BLOCK].astype(o_ref.dtype)

    s_q, p_q = {}, {}
    for step in range(n_sub + 2):
        if step >= 2:
            outputs(step - 2, *p_q.pop(step - 2))
        if 1 <= step <= n_sub:
            p_q[step - 1] = softmax_numerators(s_q.pop(step - 1))
        if step < n_sub:
            s_q[step] = scores(step)


def window_attention(qkv, sink, rel_bias):
    B, L, _ = qkv.shape
    nq = N_HEADS * HEAD_DIM
    sub = ATT_TQ // BLOCK
    nblk = L // BLOCK
    kcol = nq // KV_W
    vcol = kcol + 1

    buckets, band = _band_structure()
    onehot = (jnp.asarray(buckets)[:, :, None] == jnp.arange(N_BUCKETS)).astype(F32)
    bias = jnp.einsum("qkn,nh->qkh", onehot, rel_bias.astype(F32), precision=lax.Precision.HIGHEST)
    inv_scale = HEAD_DIM ** 0.5
    bias = jnp.where(band[:, :, None], bias * inv_scale, NEG)
    bias = jnp.transpose(bias, (2, 0, 1)).reshape(N_KV, GQA_G * BLOCK, 3 * BLOCK)
    sink_rows = jnp.broadcast_to((sink.astype(F32) * inv_scale).reshape(N_KV, GQA_G, 1, 1),
                                 (N_KV, GQA_G, BLOCK, V7X_LANES)).reshape(N_KV, GQA_G * BLOCK, V7X_LANES)

    return pl.pallas_call(
        _attn_body,
        grid=(B, L // ATT_TQ),
        in_specs=[
            pl.BlockSpec((1, ATT_TQ, nq), lambda b, i: (b, i, 0)),
            pl.BlockSpec((1, BLOCK, KV_W), lambda b, i: (b, jnp.maximum(sub * i - 1, 0), kcol)),
            pl.BlockSpec((1, ATT_TQ, KV_W), lambda b, i: (b, i, kcol)),
            pl.BlockSpec((1, BLOCK, KV_W), lambda b, i: (b, jnp.minimum(sub * i + sub, nblk - 1), kcol)),
            pl.BlockSpec((1, BLOCK, KV_W), lambda b, i: (b, jnp.maximum(sub * i - 1, 0), vcol)),
            pl.BlockSpec((1, ATT_TQ, KV_W), lambda b, i: (b, i, vcol)),
            pl.BlockSpec((1, BLOCK, KV_W), lambda b, i: (b, jnp.minimum(sub * i + sub, nblk - 1), vcol)),
            pl.BlockSpec((N_KV, GQA_G * BLOCK, 3 * BLOCK), lambda b, i: (0, 0, 0)),
            pl.BlockSpec((N_KV, GQA_G * BLOCK, V7X_LANES), lambda b, i: (0, 0, 0)),
        ],
        out_specs=pl.BlockSpec((1, ATT_TQ, nq), lambda b, i: (b, i, 0)),
        out_shape=jax.ShapeDtypeStruct((B, L, nq), BF16),
        scratch_shapes=[pltpu.VMEM((ATT_TQ + 2 * BLOCK, KV_W), BF16),
                        pltpu.VMEM((ATT_TQ + 2 * BLOCK, 2 * KV_W), BF16)],
        compiler_params=_cparams("parallel", "parallel"),
        name="window_attention",
    )(qkv, qkv, qkv, qkv, qkv, qkv, qkv, bias, sink_rows)


FEAT_PAD = V7X_LANES


def _filter_mlp_body(feat_ref, w1_ref, b1_ref, w2_ref, b2_ref, w3_ref, b3_ref, fr_ref, a_ref):
    fr = fr_ref[...]
    a = jnp.sin(fr * (jnp.dot(feat_ref[...], w1_ref[...], preferred_element_type=F32) + b1_ref[...]))
    a = jnp.sin(fr * (jnp.dot(a.astype(BF16), w2_ref[...], preferred_element_type=F32) + b2_ref[...]))
    a = jnp.sin(fr * (jnp.dot(a.astype(BF16), w3_ref[...], preferred_element_type=F32) + b3_ref[...]))
    a_ref[...] = a.astype(BF16)


def hyena_filter_mlp(L, f_w1, f_b1, f_w2, f_b2, f_w3, f_b3, f_freq, *, bt=1024):
    t = jnp.linspace(0.0, 1.0, L, dtype=F32)[:, None]
    w = 2.0 * math.pi * jnp.arange(L, dtype=F32) / L
    f = jnp.linspace(1e-4, HY_BANDS - 1, HY_BANDS, dtype=F32)
    ang = w[:, None] * f[None, :]
    feats = jnp.concatenate([t, jnp.cos(ang), -jnp.sin(ang), jnp.zeros((L, FEAT_PAD - HY_EMB), F32)], axis=-1)
    w1 = jnp.concatenate([f_w1.astype(F32), jnp.zeros((FEAT_PAD - HY_EMB, HY_FILTER_W), F32)], axis=0)
    W = HY_FILTER_W
    const = lambda r, c: pl.BlockSpec((r, c), lambda i: (0, 0))
    return pl.pallas_call(
        _filter_mlp_body,
        grid=(L // bt,),
        in_specs=[pl.BlockSpec((bt, FEAT_PAD), lambda i: (i, 0)), const(FEAT_PAD, W), const(1, W), const(W, W),
                  const(1, W), const(W, W), const(1, W), const(1, W)],
        out_specs=pl.BlockSpec((bt, W), lambda i: (i, 0)),
        out_shape=jax.ShapeDtypeStruct((L, W), BF16),
        compiler_params=_cparams("parallel"),
        name="hyena_filter_mlp",
    )(feats.astype(BF16), w1.astype(BF16), f_b1.reshape(1, W), f_w2.astype(BF16), f_b2.reshape(1, W),
      f_w3.astype(BF16), f_b3.reshape(1, W), f_freq.astype(F32).reshape(1, W))


def _dft_tables(L, paired):
    N = 2 * L
    P = FFT_P
    Q = N // P
    S = L // P
    b = np.arange(Q, dtype=np.float64)
    fq = np.exp(-2j * np.pi * np.outer(b, np.arange(S)) / Q)
    fp =np.exp(-2j * np.pi * np.outer(np.arange(P), np.arange(P)) / P)
    tw = np.exp(-2j * np.pi * np.outer(b, np.arange(P)) / N)
    ci = np.conj(fq).T / N

    def stack(c):
        return np.block([[c.real, -c.imag], [c.imag, c.real]])

    if paired:
        g1 = stack(fq)
        g4 = stack(ci)
    else:
        g1 = np.concatenate([fq.real, fq.imag], axis=0)
        g4 = np.concatenate([ci.real, -ci.imag], axis=1)
    fq_full = np.exp(-2j * np.pi * np.outer(b, np.arange(Q)) / Q)
    g1_full = np.concatenate([fq_full.real, fq_full.imag], axis=0)
    g2 =np.block([[fp.real, fp.imag], [-fp.imag, fp.real]])
    g2c = np.block([[fp.real, -fp.imag], [fp.imag, fp.real]])
    twr = np.tile(tw.real, (1, 2))
    twi = np.tile(tw.imag, (1, 2))
    f32 = lambda a: np.ascontiguousarray(a, dtype=np.float32)
    return dict(g1=f32(g1), g4=f32(g4), g1_full=f32(g1_full), g2=f32(g2), g2c=f32(g2c),
                twr=f32(twr), twi=f32(twi), Q=Q, S=S, N=N)


def _to_tiles(x, n_tiles):
    chunks = [x[:, s * FFT_P:(s + 1) * FFT_P] for s in range(n_tiles)]
    return jnp.swapaxes(jnp.stack(chunks, axis=0), 0, 1)


def _from_tiles(x):
    y = jnp.swapaxes(x, 0, 1)
    return jnp.concatenate([y[s] for s in range(y.shape[0])], axis=1)


def _fwd_fft(re_tiles, im_tiles, g1, twr, twi, g2):
    G = len(re_tiles)
    Q = twr.shape[0]
    P = FFT_P
    rows = []
    for c in range(0, G, 2):
        top = jnp.concatenate([re_tiles[c], re_tiles[c + 1]], axis=1)
        if im_tiles is None:
            rhs = top
        else:
            rhs = jnp.concatenate([top, jnp.concatenate([im_tiles[c], im_tiles[c + 1]], axis=1)], axis=0)
        y = jnp.dot(g1, rhs.astype(BF16), preferred_element_type=F32)
        y = y.astype(twr.dtype)
        yr, yi = y[:Q], y[Q:]
        zr = yr * twr - yi * twi
        zi = yr * twi + yi * twr
        rows.append(jnp.concatenate([zr[:, :P], zi[:, :P]], axis=1))
        rows.append(jnp.concatenate([zr[:, P:], zi[:, P:]], axis=1))
    lhs = jnp.concatenate(rows, axis=0).astype(BF16)
    return jnp.dot(lhs, g2, preferred_element_type=F32)


def _inv_fft(spec, g2c, twr, twi, g4, want_imag):
    Q = twr.shape[0]
    P = FFT_P
    G = spec.shape[0] // Q
    S = g4.shape[0] // 2 if want_imag else g4.shape[0]
    y = jnp.dot(spec.astype(BF16), g2c, preferred_element_type=F32)
    y = y.astype(twr.dtype)
    out_re, out_im = [], []
    for c in range(0, G, 2):
        ya = y[c * Q:(c + 1) * Q]
        yb = y[(c + 1) * Q:(c + 2) * Q]
        yr = jnp.concatenate([ya[:, :P], yb[:, :P]], axis=1)
        yi = jnp.concatenate([ya[:, P:], yb[:, P:]], axis=1)
        zr = yr * twr + yi * twi
        zi = yi * twr - yr * twi
        rhs = jnp.concatenate([zr, zi], axis=0).astype(BF16)
        o = jnp.dot(g4, rhs, preferred_element_type=F32)
        out_re += [o[:S, :P], o[:S, P:]]
        if want_imag:
            out_im += [o[S:, :P], o[S:, P:]]
    return out_re, out_im


def _hyena_body(v_ref, x1_ref, x2_ref, af_ref, ab_ref, wof_ref, wob_ref, delta_ref, tf_ref, tb_ref, skip_ref,
                g1_ref, g1f_ref, twr_ref, twi_ref, g2_ref, g2c_ref, g4_ref, o_ref, taps_ref, kf_ref,
                *, S, Q, paired):
    g1, twr, twi = g1_ref[...], twr_ref[...], twi_ref[...]
    g2, g2c, g4 = g2_ref[...], g2c_ref[...], g4_ref[...]
    nb = 2 if paired else 1
    cb = kf_ref.shape[1]
    G = min(cb, CH_GROUP)
    n_groups = cb // G
    P = FFT_P

    @pl.when(pl.program_id(1) == 0)
    def _():
        g1f = g1f_ref[...]
        L = S * P
        delta = delta_ref[...]
        halves = ((wof_ref, af_ref, jnp.exp(-(delta * tf_ref[...]))),
                  (wob_ref, ab_ref, jnp.where(lax.broadcasted_iota(jnp.int32, (cb, L), 1) == 0, 0.0,
                                              jnp.exp(-(delta * tb_ref[...])))))
        for half, (wo_ref, a_ref, decay) in enumerate(halves):
            wo = wo_ref[...].reshape(HY_ORDER * cb, HY_FILTER_W)
            h = lax.dot_general(wo, a_ref[...], (((1,), (1,)), ((), ())), preferred_element_type=F32)
            for o in range(HY_ORDER):
                taps_ref[o, :, half * L:(half + 1) * L] = h[o * cb:(o + 1) * cb] * decay

        def filter_group(gi, carry):
            c0 = pl.multiple_of(gi * G, G)
            for o in range(HY_ORDER):
                k = taps_ref[o, pl.ds(c0, G), :]
                norm = jnp.sum(jnp.abs(k), axis=-1, keepdims=True)
                tiles = _to_tiles(k, Q)
                spec = _fwd_fft([tiles[c] for c in range(G)], None, g1f, twr, twi, g2)
                kf = spec.reshape(G, Q, 2 * P) * (1.0 / norm)[:, :, None]
                kf_ref[o, pl.ds(c0, G)] = kf.astype(kf_ref.dtype)
            return carry

        lax.fori_loop(0, n_groups, filter_group, 0)

    def group(gi, carry):
        c0 = pl.multiple_of(gi * G, G)
        r0 = pl.multiple_of(gi * (G * V7X_SUBLANES), G * V7X_SUBLANES)

        def tiles(ref, b):
            blk = ref[b, :, pl.ds(r0, G * V7X_SUBLANES), :]
            return [blk[:, c * V7X_SUBLANES:(c + 1) * V7X_SUBLANES, :].reshape(S, P) for c in range(G)]

        z = [tiles(v_ref, b) for b in range(nb)]
        gates = [[tiles(x1_ref, b) for b in range(nb)], [tiles(x2_ref, b) for b in range(nb)]]
        for o in range(HY_ORDER):
            re = z[0]
            im = z[1] if paired else None
            spec = _fwd_fft(re, im, g1, twr, twi, g2)
            kf = kf_ref[o, pl.ds(c0, G)].reshape(G * Q, 2 * P)
            spec = spec.astype(kf.dtype)
            xr, xi = spec[:, :P], spec[:, P:]
            kr, ki = kf[:, :P], kf[:, P:]
            prod = jnp.concatenate([xr * kr - xi * ki, xr * ki + xi * kr], axis=1)
            out_re, out_im = _inv_fft(prod, g2c, twr, twi, g4, paired)
            skip = skip_ref[o, pl.ds(c0, G)]
            conv = [out_re] + ([out_im] if paired else [])
            z = [[gates[o][b][c] * (conv[b][c] + skip[c] * z[b][c]) for c in range(G)] for b in range(nb)]
        for b in range(nb):
            o_ref[b, pl.ds(c0, G), :] = _from_tiles(jnp.stack(z[b], axis=0))
        return carry

    lax.fori_loop(0, n_groups, group, 0)


def hyena_operator(u_tiles, a, f_wout, skip, tabs, *, cb):
    B, nt, rows, P = u_tiles.shape
    D = rows // (3 * V7X_SUBLANES)
    L = nt * IN_BT
    S = L // P
    Q = tabs["Q"]
    W = HY_FILTER_W
    paired = B % 2 == 0
    nb = 2 if paired else 1
    wo = f_wout.T.reshape(HY_ORDER, 2, D, W).astype(BF16)
    wo_f, wo_b = wo[:, 0], wo[:, 1]
    deltas = np.abs(np.linspace(math.log(HY_TARGET) / HY_SLOW_PCT,
                                math.log(HY_TARGET) / HY_FAST_PCT, D)).astype(np.float32).reshape(D, 1)
    t_f = jnp.linspace(0.0, 1.0, L, dtype=F32).reshape(1, L)
    a_b = jnp.roll(jnp.flip(a, axis=0), 1, axis=0)
    t_b = ((L - jnp.arange(L, dtype=F32)) / (L - 1)).reshape(1, L)
    skip_rows = jnp.broadcast_to(skip.astype(F32)[:, :, None, None], (HY_ORDER, D, 1, P))

    bf = lambda name: jnp.asarray(tabs[name]).astype(BF16)
    g1, g1f, g2, g2c, g4 = bf("g1"), bf("g1_full"), bf("g2"), bf("g2c"), bf("g4")
    twr, twi = bf("twr"), bf("twi")
    const = lambda arr: pl.BlockSpec(arr.shape, lambda c, p: (0,) * arr.ndim, pipeline_mode=pl.Buffered(1))
    ncb = D // cb
    return pl.pallas_call(
        functools.partial(_hyena_body, S=S, Q=Q, paired=paired),
        grid=(ncb, B // nb),
        in_specs=[
            pl.BlockSpec((nb, nt, cb * V7X_SUBLANES, P), lambda c, p: (p, 0, c, 0)),
            pl.BlockSpec((nb, nt, cb * V7X_SUBLANES, P), lambda c, p: (p, 0, c + ncb, 0)),
            pl.BlockSpec((nb, nt, cb * V7X_SUBLANES, P), lambda c, p: (p, 0, c + 2 * ncb, 0)),
            const(a), const(a_b),
            pl.BlockSpec((HY_ORDER, cb, W), lambda c, p: (0, c, 0)),
            pl.BlockSpec((HY_ORDER, cb, W), lambda c, p: (0, c, 0)),
            pl.BlockSpec((cb, 1), lambda c, p: (c, 0)),
            const(t_f), const(t_b),
            pl.BlockSpec((HY_ORDER, cb, 1, P), lambda c, p: (0, c, 0, 0)),
            const(g1), const(g1f), const(twr), const(twi), const(g2), const(g2c), const(g4),
        ],
        out_specs=pl.BlockSpec((nb, cb, L), lambda c, p: (p, c, 0)),
        out_shape=jax.ShapeDtypeStruct((B, D, L), F32),
        scratch_shapes=[pltpu.VMEM((HY_ORDER, cb, 2 * L), F32), pltpu.VMEM((HY_ORDER, cb, Q, 2 * P), BF16)],
        compiler_params=_cparams("parallel", "arbitrary"),
        name="hyena_operator",
    )(u_tiles, u_tiles, u_tiles, a, a_b, wo_f, wo_b, jnp.asarray(deltas), t_f, t_b, skip_rows,
      g1, g1f, twr, twi, g2, g2c, g4)


def _trunk(x, p, cfg):
    B, L, D = x.shape
    tabs = _dft_tables(L, paired=(B % 2 == 0))
    for i in range(DEPTH):
        j = i // N_MIXERS
        if i % N_MIXERS == 0:
            a = hyena_filter_mlp(L, p["hy_f_w1"][j], p["hy_f_b1"][j], p["hy_f_w2"][j], p["hy_f_b2"][j],
                                 p["hy_f_w3"][j], p["hy_f_b3"][j], p["hy_f_freq"][j])
            u_tiles = hyena_in_projection(x, p["norm_mix_g"][i], p["hy_w_in_t"][j], p["hy_b_in"][j],
                                          p["hy_conv_w"][j], p["hy_conv_b"][j], bc=1024)
            z_t = hyena_operator(u_tiles, a, p["hy_f_wout"][j], p["hy_skip"][j], tabs, cb=cfg["hy_cb"])
            x = residual_matmul_t(x, z_t, p["hy_w_out"][j], p["hy_b_out"][j], bt=512)
        else:
            x2 = x.reshape(B * L, D)
            qkv = qkv_projection(x2, p["norm_mix_g"][i], p["at_w_qkv"][j], p["at_q_g"][j], p["at_k_g"][j], bm=1024)
            att = window_attention(qkv.reshape(B, L, -1), p["at_sink"][j], p["rel_bias"])
            x = residual_matmul(x2, att.reshape(B * L, -1), p["at_w_o"][j], bm=1024).reshape(B, L, D)
        x = ffn_block(x.reshape(B * L, D), p["norm_ffn_g"][i], p["ffn_w_gate_up"][i], p["ffn_w_down"][i],
                      bm=1024, bf=512).reshape(B, L, D)
    return x


def kernel(x_prompt, x_sample, norm_mix_g, norm_ffn_g, hy_w_in, hy_b_in, hy_conv_w, hy_conv_b, hy_f_w1, hy_f_b1,
           hy_f_w2, hy_f_b2, hy_f_w3, hy_f_b3, hy_f_wout, hy_f_freq, hy_skip, hy_w_out, hy_b_out, at_w_qkv, at_q_g,
           at_k_g, at_sink, at_w_o, rel_bias, ffn_w_gate_up, ffn_w_down):
    per_layer = lambda w: [w[i].astype(BF16) for i in range(w.shape[0])]
    p = dict(
        norm_mix_g=norm_mix_g.astype(F32), norm_ffn_g=norm_ffn_g.astype(F32),
        hy_w_in_t=[hy_w_in[j].T.astype(BF16) for j in range(hy_w_in.shape[0])], hy_b_in=hy_b_in,
        hy_conv_w=hy_conv_w, hy_conv_b=hy_conv_b,
        hy_f_w1=hy_f_w1, hy_f_b1=hy_f_b1, hy_f_w2=hy_f_w2, hy_f_b2=hy_f_b2, hy_f_w3=hy_f_w3, hy_f_b3=hy_f_b3,
        hy_f_wout=hy_f_wout, hy_f_freq=hy_f_freq, hy_skip=hy_skip,
        hy_w_out=per_layer(hy_w_out), hy_b_out=hy_b_out,
        at_w_qkv=per_layer(at_w_qkv), at_q_g=at_q_g, at_k_g=at_k_g, at_sink=at_sink,
        at_w_o=per_layer(at_w_o), rel_bias=rel_bias,
        ffn_w_gate_up=per_layer(ffn_w_gate_up), ffn_w_down=per_layer(ffn_w_down),
    )
    y_prompt = _trunk(x_prompt, p, dict(hy_cb=32))
    y_sample = _trunk(x_sample, p, dict(hy_cb=16))
    return (y_prompt, y_sample)
```

```python
import functools
import math

import jax
import jax.numpy as jnp
import numpy as np
from jax import lax
from jax.experimental import pallas as pl
from jax.experimental.pallas import tpu as pltpu

F32 = jnp.float32
BF16 = jnp.bfloat16

D_MODEL = 2048
DEPTH = 4
N_MIXERS = 2
HY_ORDER = 2
HY_EMB = 33
HY_BANDS = (HY_EMB - 1) // 2
HY_FILTER_W = 64
HY_FAST_PCT = 0.3
HY_SLOW_PCT = 1.5
HY_TARGET = 1e-2
N_HEADS = 16
HEAD_DIM = 128
N_KV = 4
GQA_G = N_HEADS // N_KV
WINDOW = 128
BLOCK = 128
N_BUCKETS = 32
MAX_DIST = 128
D_FF = -(-(8 * D_MODEL) // (3 * 256)) * 256
EPS = 1e-6
NEG = -1e30

V7X_LANES = 128
V7X_SUBLANES = 8
VMEM_LIMIT = 56 * 1024 * 1024

FFT_P = V7X_LANES
CH_GROUP = 4 * V7X_SUBLANES


def _cparams(*sem):
    return pltpu.CompilerParams(dimension_semantics=sem, vmem_limit_bytes=VMEM_LIMIT)


def _rms_bf16(x, g):
    ms = jnp.mean(x * x, axis=-1, keepdims=True)
    return (x * lax.rsqrt(ms + EPS) * g).astype(BF16)


def _qkv_proj_body(x_ref, g_ref, w_ref, qg_ref, kg_ref, o_ref):
    hn = _rms_bf16(x_ref[...], g_ref[...])
    nq, nk = N_HEADS * HEAD_DIM, N_KV * HEAD_DIM
    pair = 2 * HEAD_DIM
    for c0 in range(0, nq + nk, pair):
        acc = jnp.dot(hn, w_ref[:, c0:c0 + pair], preferred_element_type=F32)
        gain = qg_ref[...] if c0 < nq else kg_ref[...]
        for c in (0, HEAD_DIM):
            t = acc[:, c:c + HEAD_DIM]
            ms = jnp.mean(t * t, axis=-1, keepdims=True)
            o_ref[:, c0 + c:c0 + c + HEAD_DIM] = (t * lax.rsqrt(ms + EPS) * gain).astype(o_ref.dtype)
    o_ref[:, nq + nk:] = jnp.dot(hn, w_ref[:, nq + nk:], preferred_element_type=F32).astype(o_ref.dtype)


def qkv_projection(x, g, w, q_g, k_g, *, bm):
    T, K = x.shape
    N = w.shape[1]
    return pl.pallas_call(
        _qkv_proj_body,
        grid=(T // bm,),
        in_specs=[
            pl.BlockSpec((bm, K), lambda i: (i, 0)),
            pl.BlockSpec((1, K), lambda i: (0, 0)),
            pl.BlockSpec((K, N), lambda i: (0, 0), pipeline_mode=pl.Buffered(1)),
            pl.BlockSpec((1, HEAD_DIM), lambda i: (0, 0)),
            pl.BlockSpec((1, HEAD_DIM), lambda i: (0, 0)),
        ],
        out_specs=pl.BlockSpec((bm, N), lambda i: (i, 0)),
        out_shape=jax.ShapeDtypeStruct((T, N), BF16),
        compiler_params=_cparams("parallel"),
        name="qkv_projection",
    )(x, g.reshape(1, K), w, q_g.astype(F32).reshape(1, HEAD_DIM), k_g.astype(F32).reshape(1, HEAD_DIM))


IN_BT = V7X_SUBLANES * V7X_LANES
IN_ROW_SPLIT = 4


def _hyena_edge_body(x_ref, g_ref, w_ref, b_ref, o_ref):
    hn = _rms_bf16(x_ref[0], g_ref[...])
    o_ref[0] = lax.dot_general(hn, w_ref[...], (((1,), (1,)), ((), ())), preferred_element_type=F32) + b_ref[...]


def hyena_edge_projection(x_edge, g, w_t, bias, *, bc):
    B, n, K = x_edge.shape
    C = w_t.shape[0]
    return pl.pallas_call(
        _hyena_edge_body,
        grid=(B, C // bc),
        in_specs=[
            pl.BlockSpec((1, n, K), lambda b, c: (b, 0, 0)),
            pl.BlockSpec((1, K), lambda b, c: (0, 0)),
            pl.BlockSpec((bc, K), lambda b, c: (c, 0)),
            pl.BlockSpec((1, bc), lambda b, c: (0, c)),
        ],
        out_specs=pl.BlockSpec((1, n, bc), lambda b, c: (b, 0, c)),
        out_shape=jax.ShapeDtypeStruct((B, n, C), F32),
        compiler_params=_cparams("parallel", "parallel"),
        name="hyena_edge_projection",
    )(x_edge, g.reshape(1, K), w_t, bias.astype(F32).reshape(1, C))


def _hyena_in_body(x_ref, g_ref, w_ref, b_ref, k0_ref, k1_ref, k2_ref, kb_ref, edge_ref, o_ref, hn_ref):
    t = pl.program_id(1)
    nt = pl.num_programs(1)

    @pl.when(pl.program_id(2) == 0)
    def _():
        hn_ref[...] = _rms_bf16(x_ref[0], g_ref[...])

    P = V7X_LANES
    bt = hn_ref.shape[0]
    n_chunks = bt // P
    rows = w_ref.shape[0] // IN_ROW_SPLIT
    for r in range(IN_ROW_SPLIT):
        rs = slice(r * rows, (r + 1) * rows)
        u = lax.dot_general(w_ref[rs, :], hn_ref[...], (((1,), (1,)), ((), ())), preferred_element_type=F32)
        def border(k):
            parts = [jnp.transpose(jnp.broadcast_to(edge_ref[0, k, 0:1, c0:c0 + P], (P, P)))
                     for c0 in range(r * rows, (r + 1) * rows, P)]
            return jnp.concatenate(parts, axis=0)

        left = jnp.where(t == 0, 0.0, border(t))
        right = jnp.where(t == nt - 1, 0.0, border(nt + t))
        lane = lax.broadcasted_iota(jnp.int32, (rows, P), 1)
        bias, k0, k1, k2, kb = (ref[rs, :] for ref in (b_ref, k0_ref, k1_ref, k2_ref, kb_ref))
        chunks = [u[:, j * P:(j + 1) * P] + bias for j in range(n_chunks)]
        fwd = [pltpu.roll(c, 1, axis=1) for c in chunks]
        bwd = [pltpu.roll(c, P - 1, axis=1) for c in chunks]
        for j in range(n_chunks):
            prev = jnp.where(lane == 0, fwd[j - 1] if j > 0 else left, fwd[j])
            nxt = jnp.where(lane == P - 1, bwd[j + 1] if j + 1 < n_chunks else right, bwd[j])
            o_ref[pl.ds(r * rows * n_chunks + j, rows, stride=n_chunks), :] = (
                k0 * prev + k1 * chunks[j] + k2 * nxt + kb)


def hyena_in_projection(x, g, w_t, bias, conv_w, conv_b, *, bc):
    B, L, K = x.shape
    C = w_t.shape[0]
    bt = IN_BT
    nt = L // bt
    xb = x.reshape(B, nt, bt, K)
    x_edge = jnp.concatenate([jnp.roll(xb[:, :, bt - 1], 1, axis=1), jnp.roll(xb[:, :, 0], -1, axis=1)], axis=1)
    edge = hyena_edge_projection(x_edge, g, w_t, bias, bc=bc)
    edge = jnp.broadcast_to(edge[:, :, None, :], (B, 2 * nt, V7X_SUBLANES, C))
    col = lambda v: jnp.broadcast_to(v.astype(F32).reshape(C, 1), (C, V7X_LANES))
    cspec = pl.BlockSpec((bc, V7X_LANES), lambda b, t, c: (c, 0))
    return pl.pallas_call(
        _hyena_in_body,
        grid=(B, nt, C // bc),
        in_specs=[
            pl.BlockSpec((1, bt, K), lambda b, t, c: (b, t, 0)),
            pl.BlockSpec((1, K), lambda b, t, c: (0, 0)),
            pl.BlockSpec((bc, K), lambda b, t, c: (c, 0)),
            cspec, cspec, cspec, cspec, cspec,
            pl.BlockSpec((1, 2 * nt, V7X_SUBLANES, bc), lambda b, t, c: (b, 0, 0, c)),
        ],
        out_specs=pl.BlockSpec((None, None, bc * V7X_SUBLANES, V7X_LANES), lambda b, t, c: (b, t, c, 0)),
        out_shape=jax.ShapeDtypeStruct((B, nt, C * V7X_SUBLANES, V7X_LANES), F32),
        scratch_shapes=[pltpu.VMEM((bt, K), BF16)],
        compiler_params=_cparams("parallel", "parallel", "arbitrary"),
        name="hyena_in_projection",
    )(x, g.reshape(1, K), w_t, col(bias), col(conv_w[0]), col(conv_w[1]), col(conv_w[2]), col(conv_b), edge)


def _res_mm_body(a_ref, w_ref, x_ref, o_ref):
    o_ref[...] = x_ref[...] + jnp.dot(a_ref[...], w_ref[...], preferred_element_type=F32)


def residual_matmul(x, a, w, *, bm):
    T, K = a.shape
    N = w.shape[1]
    return pl.pallas_call(
        _res_mm_body,
        grid=(T // bm,),
        in_specs=[
            pl.BlockSpec((bm, K), lambda i: (i, 0)),
            pl.BlockSpec((K, N), lambda i: (0, 0), pipeline_mode=pl.Buffered(1)),
            pl.BlockSpec((bm, N), lambda i: (i, 0)),
        ],
        out_specs=pl.BlockSpec((bm, N), lambda i: (i, 0)),
        out_shape=jax.ShapeDtypeStruct((T, N), F32),
        compiler_params=_cparams("parallel"),
        name="residual_matmul",
    )(a, w, x)


def _res_mm_t_body(z_ref, w_ref, b_ref, x_ref, o_ref):
    z = z_ref[0].astype(BF16)
    y = lax.dot_general(z, w_ref[...], (((0,), (0,)), ((), ())), preferred_element_type=F32)
    o_ref[0] = x_ref[0] + y + b_ref[...]


def residual_matmul_t(x, z_t, w, bias, *, bt):
    B, K, L = z_t.shape
    N = w.shape[1]
    return pl.pallas_call(
        _res_mm_t_body,
        grid=(B, L // bt),
        in_specs=[
            pl.BlockSpec((1, K, bt), lambda b, t: (b, 0, t)),
            pl.BlockSpec((K, N), lambda b, t: (0, 0), pipeline_mode=pl.Buffered(1)),
            pl.BlockSpec((1, N), lambda b, t: (0, 0)),
            pl.BlockSpec((1, bt, N), lambda b, t: (b, t, 0)),
        ],
        out_specs=pl.BlockSpec((1, bt, N), lambda b, t: (b, t, 0)),
        out_shape=jax.ShapeDtypeStruct((B, L, N), F32),
        compiler_params=_cparams("parallel", "parallel"),
        name="residual_matmul_t",
    )(z_t, w, bias.reshape(1, N), x)


def _ffn_body(x_ref, g_ref, wg_ref, wu_ref, wd_ref, o_ref, hn_ref):
    @pl.when(pl.program_id(1) == 0)
    def _():
        x = x_ref[...]
        hn_ref[...] = _rms_bf16(x, g_ref[...])
        o_ref[...] = x

    h = hn_ref[...]
    gate = jnp.dot(h, wg_ref[...], preferred_element_type=F32)
    up = jnp.dot(h, wu_ref[...], preferred_element_type=F32)
    act = (gate * jax.nn.sigmoid(gate) * up).astype(BF16)
    o_ref[...] += jnp.dot(act, wd_ref[...], preferred_element_type=F32)


def ffn_block(x, g, w_gate_up, w_down, *, bm, bf):
    T, K = x.shape
    nf = D_FF // bf
    return pl.pallas_call(
        _ffn_body,
        grid=(T // bm, nf),
        in_specs=[
            pl.BlockSpec((bm, K), lambda i, f: (i, 0)),
            pl.BlockSpec((1, K), lambda i, f: (0, 0)),
            pl.BlockSpec((K, bf), lambda i, f: (0, f)),
            pl.BlockSpec((K, bf), lambda i, f: (0, f + nf)),
            pl.BlockSpec((bf, K), lambda i, f: (f, 0)),
        ],
        out_specs=pl.BlockSpec((bm, K), lambda i, f: (i, 0)),
        out_shape=jax.ShapeDtypeStruct((T, K), F32),
        scratch_shapes=[pltpu.VMEM((bm, K), BF16)],
        compiler_params=_cparams("parallel", "arbitrary"),
        name="ffn_block",
    )(x, g.reshape(1, K), w_gate_up, w_gate_up, w_down)


ATT_TQ = 4 * BLOCK
KV_W = N_KV * HEAD_DIM


def _band_structure():
    qi = np.arange(BLOCK)[:, None]
    ki = np.arange(3 * BLOCK)[None, :]
    rel = ki - BLOCK - qi
    nb = N_BUCKETS // 2
    max_exact = nb // 2
    n = np.abs(rel)
    large = max_exact + (np.log(np.maximum(n, 1) / max_exact) / math.log(MAX_DIST / max_exact)
                         * (nb - max_exact)).astype(np.int32)
    large = np.minimum(large, nb - 1)
    buckets = (rel > 0).astype(np.int32) * nb + np.where(n < max_exact, n, large).astype(np.int32)
    band = n <= WINDOW
    return buckets, band


def _attn_body(q_ref, kp_ref, kc_ref, kn_ref, vp_ref, vc_ref, vn_ref, bias_ref, sink_ref,
               o_ref, kbuf, vbuf):
    i = pl.program_id(1)
    last = pl.num_programs(1) - 1

    kbuf[0:BLOCK] = kp_ref[0]
    kbuf[BLOCK:BLOCK + ATT_TQ] = kc_ref[0]
    kbuf[BLOCK + ATT_TQ:] = kn_ref[0]
    ones = jnp.ones((ATT_TQ + 2 * BLOCK, HEAD_DIM), BF16)
    for g in range(N_KV):
        src = slice(g * HEAD_DIM, (g + 1) * HEAD_DIM)
        dst = slice(2 * g * HEAD_DIM, (2 * g + 1) * HEAD_DIM)
        vbuf[0:BLOCK, dst] = vp_ref[0, :, src]
        vbuf[BLOCK:BLOCK + ATT_TQ, dst] = vc_ref[0, :, src]
        vbuf[BLOCK + ATT_TQ:, dst] = vn_ref[0, :, src]
        vbuf[:, (2 * g + 1) * HEAD_DIM:(2 * g + 2) * HEAD_DIM] = ones

    lane = lax.broadcasted_iota(jnp.int32, (1, 3 * BLOCK), 1)
    first_edge = jnp.where((lane < BLOCK) & (i == 0), NEG, 0.0).astype(F32)
    last_edge = jnp.where((lane >= 2 * BLOCK) & (i == last), NEG, 0.0).astype(F32)
    exp2_scale = HEAD_DIM ** -0.5 * math.log2(math.e)

    n_sub = ATT_TQ // BLOCK
    groups = range(N_KV)

    def scores(j):
        r0 = j * BLOCK
        s = []
        for g in groups:
            qs = jnp.concatenate(
                [q_ref[0, r0:r0 + BLOCK, (GQA_G * g + h) * HEAD_DIM:(GQA_G * g + h + 1) * HEAD_DIM]
                 for h in range(GQA_G)], axis=0)
            kw = kbuf[r0:r0 + 3 * BLOCK, g * HEAD_DIM:(g + 1) * HEAD_DIM]
            sg = lax.dot_general(qs, kw, (((1,), (1,)), ((), ())), preferred_element_type=F32)
            sg = sg + bias_ref[g]
            if j == 0:
                sg = sg + first_edge
            if j == n_sub - 1:
                sg = sg + last_edge
            s.append(sg)
        return s

    def softmax_numerators(s):
        sk = [sink_ref[g][:, 0:1] for g in groups]
        m = [jnp.maximum(jnp.max(s[g], axis=-1, keepdims=True), sk[g]) for g in groups]
        p = [jnp.exp2((s[g] - m[g]) * exp2_scale).astype(BF16) for g in groups]
        sink_p = [jnp.exp2((sk[g] - m[g]) * exp2_scale) for g in groups]
        return p, sink_p

    def outputs(j, p, sink_p):
        r0 = j * BLOCK
        for g in groups:
            vw = vbuf[r0:r0 + 3 * BLOCK, 2 * g * HEAD_DIM:(2 * g + 2) * HEAD_DIM]
            pv = jnp.dot(p[g], vw, preferred_element_type=F32)
            o = pv[:, :HEAD_DIM] / (pv[:, HEAD_DIM:] + sink_p[g])
            for h in range(GQA_G):
                c0 = (GQA_G * g + h) * HEAD_DIM
                o_ref[0, r0:r0 + BLOCK, c0:c0 + HEAD_DIM] = o[h * BLOCK:(h + 1) * BLOCK].astype(o_ref.dtype)

    s_q, p_q = {}, {}
    for step in range(n_sub + 2):
        if step >= 2:
            outputs(step - 2, *p_q.pop(step - 2))
        if 1 <= step <= n_sub:
            p_q[step - 1] = softmax_numerators(s_q.pop(step - 1))
        if step < n_sub:
            s_q[step] = scores(step)


def window_attention(qkv, sink, rel_bias):
    B, L, _ = qkv.shape
    nq = N_HEADS * HEAD_DIM
    sub = ATT_TQ // BLOCK
    nblk = L // BLOCK
    kcol = nq // KV_W
    vcol = kcol + 1

    buckets, band = _band_structure()
    onehot = (jnp.asarray(buckets)[:, :, None] == jnp.arange(N_BUCKETS)).astype(F32)
    bias = jnp.einsum("qkn,nh->qkh", onehot, rel_bias.astype(F32), precision=lax.Precision.HIGHEST)
    inv_scale = HEAD_DIM ** 0.5
    bias = jnp.where(band[:, :, None], bias * inv_scale, NEG)
    bias = jnp.transpose(bias, (2, 0, 1)).reshape(N_KV, GQA_G * BLOCK, 3 * BLOCK)
    sink_rows = jnp.broadcast_to((sink.astype(F32) * inv_scale).reshape(N_KV, GQA_G, 1, 1),
                                 (N_KV, GQA_G, BLOCK, V7X_LANES)).reshape(N_KV, GQA_G * BLOCK, V7X_LANES)

    return pl.pallas_call(
        _attn_body,
        grid=(B, L // ATT_TQ),
        in_specs=[
            pl.BlockSpec((1, ATT_TQ, nq), lambda b, i: (b, i, 0)),
            pl.BlockSpec((1, BLOCK, KV_W), lambda b, i: (b, jnp.maximum(sub * i - 1, 0), kcol)),
            pl.BlockSpec((1, ATT_TQ, KV_W), lambda b, i: (b, i, kcol)),
            pl.BlockSpec((1, BLOCK, KV_W), lambda b, i: (b, jnp.minimum(sub * i + sub, nblk - 1), kcol)),
            pl.BlockSpec((1, BLOCK, KV_W), lambda b, i: (b, jnp.maximum(sub * i - 1, 0), vcol)),
            pl.BlockSpec((1, ATT_TQ, KV_W), lambda b, i: (b, i, vcol)),
            pl.BlockSpec((1, BLOCK, KV_W), lambda b, i: (b, jnp.minimum(sub * i + sub, nblk - 1), vcol)),
            pl.BlockSpec((N_KV, GQA_G * BLOCK, 3 * BLOCK), lambda b, i: (0, 0, 0)),
            pl.BlockSpec((N_KV, GQA_G * BLOCK, V7X_LANES), lambda b, i: (0, 0, 0)),
        ],
        out_specs=pl.BlockSpec((1, ATT_TQ, nq), lambda b, i: (b, i, 0)),
        out_shape=jax.ShapeDtypeStruct((B, L, nq), BF16),
        scratch_shapes=[pltpu.VMEM((ATT_TQ + 2 * BLOCK, KV_W), BF16),
                        pltpu.VMEM((ATT_TQ + 2 * BLOCK, 2 * KV_W), BF16)],
        compiler_params=_cparams("parallel", "parallel"),
        name="window_attention",
    )(qkv, qkv, qkv, qkv, qkv, qkv, qkv, bias, sink_rows)


FEAT_PAD = V7X_LANES


def _filter_mlp_body(feat_ref, w1_ref, b1_ref, w2_ref, b2_ref, w3_ref, b3_ref, fr_ref, a_ref):
    fr = fr_ref[...]
    a = jnp.sin(fr * (jnp.dot(feat_ref[...], w1_ref[...], preferred_element_type=F32) + b1_ref[...]))
    a = jnp.sin(fr * (jnp.dot(a.astype(BF16), w2_ref[...], preferred_element_type=F32) + b2_ref[...]))
    a = jnp.sin(fr * (jnp.dot(a.astype(BF16), w3_ref[...], preferred_element_type=F32) + b3_ref[...]))
    a_ref[...] = a.astype(BF16)


def hyena_filter_mlp(L, f_w1, f_b1, f_w2, f_b2, f_w3, f_b3, f_freq, *, bt=1024):
    t = jnp.linspace(0.0, 1.0, L, dtype=F32)[:, None]
    w = 2.0 * math.pi * jnp.arange(L, dtype=F32) / L
    f = jnp.linspace(1e-4, HY_BANDS - 1, HY_BANDS, dtype=F32)
    ang = w[:, None] * f[None, :]
    feats = jnp.concatenate([t, jnp.cos(ang), -jnp.sin(ang), jnp.zeros((L, FEAT_PAD - HY_EMB), F32)], axis=-1)
    w1 = jnp.concatenate([f_w1.astype(F32), jnp.zeros((FEAT_PAD - HY_EMB, HY_FILTER_W), F32)], axis=0)
    W = HY_FILTER_W
    const = lambda r, c: pl.BlockSpec((r, c), lambda i: (0, 0))
    return pl.pallas_call(
        _filter_mlp_body,
        grid=(L // bt,),
        in_specs=[pl.BlockSpec((bt, FEAT_PAD), lambda i: (i, 0)), const(FEAT_PAD, W), const(1, W), const(W, W),
                  const(1, W), const(W, W), const(1, W), const(1, W)],
        out_specs=pl.BlockSpec((bt, W), lambda i: (i, 0)),
        out_shape=jax.ShapeDtypeStruct((L, W), BF16),
        compiler_params=_cparams("parallel"),
        name="hyena_filter_mlp",
    )(feats.astype(BF16), w1.astype(BF16), f_b1.reshape(1, W), f_w2.astype(BF16), f_b2.reshape(1, W),
      f_w3.astype(BF16), f_b3.reshape(1, W), f_freq.astype(F32).reshape(1, W))


def _dft_tables(L, paired):
    N = 2 * L
    P = FFT_P
    Q = N // P
    S = L // P
    b = np.arange(Q, dtype=np.float64)
    fq = np.exp(-2j * np.pi * np.outer(b, np.arange(S)) / Q)
    fp =np.exp(-2j * np.pi * np.outer(np.arange(P), np.arange(P)) / P)
    tw = np.exp(-2j * np.pi * np.outer(b, np.arange(P)) / N)
    ci = np.conj(fq).T / N

    def stack(c):
        return np.block([[c.real, -c.imag], [c.imag, c.real]])

    if paired:
        g1 = stack(fq)
        g4 = stack(ci)
    else:
        g1 = np.concatenate([fq.real, fq.imag], axis=0)
        g4 = np.concatenate([ci.real, -ci.imag], axis=1)
    fq_full = np.exp(-2j * np.pi * np.outer(b, np.arange(Q)) / Q)
    g1_full = np.concatenate([fq_full.real, fq_full.imag], axis=0)
    g2 =np.block([[fp.real, fp.imag], [-fp.imag, fp.real]])
    g2c = np.block([[fp.real, -fp.imag], [fp.imag, fp.real]])
    twr = np.tile(tw.real, (1, 2))
    twi = np.tile(tw.imag, (1, 2))
    f32 = lambda a: np.ascontiguousarray(a, dtype=np.float32)
    return dict(g1=f32(g1), g4=f32(g4), g1_full=f32(g1_full), g2=f32(g2), g2c=f32(g2c),
                twr=f32(twr), twi=f32(twi), Q=Q, S=S, N=N)


def _to_tiles(x, n_tiles):
    chunks = [x[:, s * FFT_P:(s + 1) * FFT_P] for s in range(n_tiles)]
    return jnp.swapaxes(jnp.stack(chunks, axis=0), 0, 1)


def _from_tiles(x):
    y = jnp.swapaxes(x, 0, 1)
    return jnp.concatenate([y[s] for s in range(y.shape[0])], axis=1)


def _fwd_fft(re_tiles, im_tiles, g1, twr, twi, g2):
    G = len(re_tiles)
    Q = twr.shape[0]
    P = FFT_P
    rows = []
    for c in range(0, G, 2):
        top = jnp.concatenate([re_tiles[c], re_tiles[c + 1]], axis=1)
        if im_tiles is None:
            rhs = top
        else:
            rhs = jnp.concatenate([top, jnp.concatenate([im_tiles[c], im_tiles[c + 1]], axis=1)], axis=0)
        y = jnp.dot(g1, rhs.astype(BF16), preferred_element_type=F32)
        y = y.astype(twr.dtype)
        yr, yi = y[:Q], y[Q:]
        zr = yr * twr - yi * twi
        zi = yr * twi + yi * twr
        rows.append(jnp.concatenate([zr[:, :P], zi[:, :P]], axis=1))
        rows.append(jnp.concatenate([zr[:, P:], zi[:, P:]], axis=1))
    lhs = jnp.concatenate(rows, axis=0).astype(BF16)
    return jnp.dot(lhs, g2, preferred_element_type=F32)


def _inv_fft(spec, g2c, twr, twi, g4, want_imag):
    Q = twr.shape[0]
    P = FFT_P
    G = spec.shape[0] // Q
    S = g4.shape[0] // 2 if want_imag else g4.shape[0]
    y = jnp.dot(spec.astype(BF16), g2c, preferred_element_type=F32)
    y = y.astype(twr.dtype)
    out_re, out_im = [], []
    for c in range(0, G, 2):
        ya = y[c * Q:(c + 1) * Q]
        yb = y[(c + 1) * Q:(c + 2) * Q]
        yr = jnp.concatenate([ya[:, :P], yb[:, :P]], axis=1)
        yi = jnp.concatenate([ya[:, P:], yb[:, P:]], axis=1)
        zr = yr * twr + yi * twi
        zi = yi * twr - yr * twi
        rhs = jnp.concatenate([zr, zi], axis=0).astype(BF16)
        o = jnp.dot(g4, rhs, preferred_element_type=F32)
        out_re += [o[:S, :P], o[:S, P:]]
        if want_imag:
            out_im += [o[S:, :P], o[S:, P:]]
    return out_re, out_im


def _hyena_body(v_ref, x1_ref, x2_ref, af_ref, ab_ref, wof_ref, wob_ref, delta_ref, tf_ref, tb_ref, skip_ref,
                g1_ref, g1f_ref, twr_ref, twi_ref, g2_ref, g2c_ref, g4_ref, o_ref, taps_ref, kf_ref,
                *, S, Q, paired):
    g1, twr, twi = g1_ref[...], twr_ref[...], twi_ref[...]
    g2, g2c, g4 = g2_ref[...], g2c_ref[...], g4_ref[...]
    nb = 2 if paired else 1
    cb = kf_ref.shape[1]
    G = min(cb, CH_GROUP)
    n_groups = cb // G
    P = FFT_P

    @pl.when(pl.program_id(1) == 0)
    def _():
        g1f = g1f_ref[...]
        L = S * P
        delta = delta_ref[...]
        halves = ((wof_ref, af_ref, jnp.exp(-(delta * tf_ref[...]))),
                  (wob_ref, ab_ref, jnp.where(lax.broadcasted_iota(jnp.int32, (cb, L), 1) == 0, 0.0,
                                              jnp.exp(-(delta * tb_ref[...])))))
        for half, (wo_ref, a_ref, decay) in enumerate(halves):
            wo = wo_ref[...].reshape(HY_ORDER * cb, HY_FILTER_W)
            h = lax.dot_general(wo, a_ref[...], (((1,), (1,)), ((), ())), preferred_element_type=F32)
            for o in range(HY_ORDER):
                taps_ref[o, :, half * L:(half + 1) * L] = h[o * cb:(o + 1) * cb] * decay

        def filter_group(gi, carry):
            c0 = pl.multiple_of(gi * G, G)
            for o in range(HY_ORDER):
                k = taps_ref[o, pl.ds(c0, G), :]
                norm = jnp.sum(jnp.abs(k), axis=-1, keepdims=True)
                tiles = _to_tiles(k, Q)
                spec = _fwd_fft([tiles[c] for c in range(G)], None, g1f, twr, twi, g2)
                kf = spec.reshape(G, Q, 2 * P) * (1.0 / norm)[:, :, None]
                kf_ref[o, pl.ds(c0, G)] = kf.astype(kf_ref.dtype)
            return carry

        lax.fori_loop(0, n_groups, filter_group, 0)

    def group(gi, carry):
        c0 = pl.multiple_of(gi * G, G)
        r0 = pl.multiple_of(gi * (G * V7X_SUBLANES), G * V7X_SUBLANES)

        def tiles(ref, b):
            blk = ref[b, :, pl.ds(r0, G * V7X_SUBLANES), :]
            return [blk[:, c * V7X_SUBLANES:(c + 1) * V7X_SUBLANES, :].reshape(S, P) for c in range(G)]

        z = [tiles(v_ref, b) for b in range(nb)]
        gates = [[tiles(x1_ref, b) for b in range(nb)], [tiles(x2_ref, b) for b in range(nb)]]
        for o in range(HY_ORDER):
            re = z[0]
            im = z[1] if paired else None
            spec = _fwd_fft(re, im, g1, twr, twi, g2)
            kf = kf_ref[o, pl.ds(c0, G)].reshape(G * Q, 2 * P)
            spec = spec.astype(kf.dtype)
            xr, xi = spec[:, :P], spec[:, P:]
            kr, ki = kf[:, :P], kf[:, P:]
            prod = jnp.concatenate([xr * kr - xi * ki, xr * ki + xi * kr], axis=1)
            out_re, out_im = _inv_fft(prod, g2c, twr, twi, g4, paired)
            skip = skip_ref[o, pl.ds(c0, G)]
            conv = [out_re] + ([out_im] if paired else [])
            z = [[gates[o][b][c] * (conv[b][c] + skip[c] * z[b][c]) for c in range(G)] for b in range(nb)]
        for b in range(nb):
            o_ref[b, pl.ds(c0, G), :] = _from_tiles(jnp.stack(z[b], axis=0))
        return carry

    lax.fori_loop(0, n_groups, group, 0)


def hyena_operator(u_tiles, a, f_wout, skip, tabs, *, cb):
    B, nt, rows, P = u_tiles.shape
    D = rows // (3 * V7X_SUBLANES)
    L = nt * IN_BT
    S = L // P
    Q = tabs["Q"]
    W = HY_FILTER_W
    paired = B % 2 == 0
    nb = 2 if paired else 1
    wo = f_wout.T.reshape(HY_ORDER, 2, D, W).astype(BF16)
    wo_f, wo_b = wo[:, 0], wo[:, 1]
    deltas = np.abs(np.linspace(math.log(HY_TARGET) / HY_SLOW_PCT,
                                math.log(HY_TARGET) / HY_FAST_PCT, D)).astype(np.float32).reshape(D, 1)
    t_f = jnp.linspace(0.0, 1.0, L, dtype=F32).reshape(1, L)
    a_b = jnp.roll(jnp.flip(a, axis=0), 1, axis=0)
    t_b = ((L - jnp.arange(L, dtype=F32)) / (L - 1)).reshape(1, L)
    skip_rows = jnp.broadcast_to(skip.astype(F32)[:, :, None, None], (HY_ORDER, D, 1, P))

    bf = lambda name: jnp.asarray(tabs[name]).astype(BF16)
    g1, g1f, g2, g2c, g4 = bf("g1"), bf("g1_full"), bf("g2"), bf("g2c"), bf("g4")
    twr, twi = bf("twr"), bf("twi")
    const = lambda arr: pl.BlockSpec(arr.shape, lambda c, p: (0,) * arr.ndim, pipeline_mode=pl.Buffered(1))
    ncb = D // cb
    return pl.pallas_call(
        functools.partial(_hyena_body, S=S, Q=Q, paired=paired),
        grid=(ncb, B // nb),
        in_specs=[
            pl.BlockSpec((nb, nt, cb * V7X_SUBLANES, P), lambda c, p: (p, 0, c, 0)),
            pl.BlockSpec((nb, nt, cb * V7X_SUBLANES, P), lambda c, p: (p, 0, c + ncb, 0)),
            pl.BlockSpec((nb, nt, cb * V7X_SUBLANES, P), lambda c, p: (p, 0, c + 2 * ncb, 0)),
            const(a), const(a_b),
            pl.BlockSpec((HY_ORDER, cb, W), lambda c, p: (0, c, 0)),
            pl.BlockSpec((HY_ORDER, cb, W), lambda c, p: (0, c, 0)),
            pl.BlockSpec((cb, 1), lambda c, p: (c, 0)),
            const(t_f), const(t_b),
            pl.BlockSpec((HY_ORDER, cb, 1, P), lambda c, p: (0, c, 0, 0)),
            const(g1), const(g1f), const(twr), const(twi), const(g2), const(g2c), const(g4),
        ],
        out_specs=pl.BlockSpec((nb, cb, L), lambda c, p: (p, c, 0)),
        out_shape=jax.ShapeDtypeStruct((B, D, L), F32),
        scratch_shapes=[pltpu.VMEM((HY_ORDER, cb, 2 * L), F32), pltpu.VMEM((HY_ORDER, cb, Q, 2 * P), BF16)],
        compiler_params=_cparams("parallel", "arbitrary"),
        name="hyena_operator",
    )(u_tiles, u_tiles, u_tiles, a, a_b, wo_f, wo_b, jnp.asarray(deltas), t_f, t_b, skip_rows,
      g1, g1f, twr, twi, g2, g2c, g4)


def _trunk(x, p, cfg):
    B, L, D = x.shape
    tabs = _dft_tables(L, paired=(B % 2 == 0))
    for i in range(DEPTH):
        j = i // N_MIXERS
        if i % N_MIXERS == 0:
            a = hyena_filter_mlp(L, p["hy_f_w1"][j], p["hy_f_b1"][j], p["hy_f_w2"][j], p["hy_f_b2"][j],
                                 p["hy_f_w3"][j], p["hy_f_b3"][j], p["hy_f_freq"][j])
            u_tiles = hyena_in_projection(x, p["norm_mix_g"][i], p["hy_w_in_t"][j], p["hy_b_in"][j],
                                          p["hy_conv_w"][j], p["hy_conv_b"][j], bc=1024)
            z_t = hyena_operator(u_tiles, a, p["hy_f_wout"][j], p["hy_skip"][j], tabs, cb=cfg["hy_cb"])
            x = residual_matmul_t(x, z_t, p["hy_w_out"][j], p["hy_b_out"][j], bt=512)
        else:
            x2 = x.reshape(B * L, D)
            qkv = qkv_projection(x2, p["norm_mix_g"][i], p["at_w_qkv"][j], p["at_q_g"][j], p["at_k_g"][j], bm=1024)
            att = window_attention(qkv.reshape(B, L, -1), p["at_sink"][j], p["rel_bias"])
            x = residual_matmul(x2, att.reshape(B * L, -1), p["at_w_o"][j], bm=1024).reshape(B, L, D)
        x = ffn_block(x.reshape(B * L, D), p["norm_ffn_g"][i], p["ffn_w_gate_up"][i], p["ffn_w_down"][i],
                      bm=1024, bf=512).reshape(B, L, D)
    return x


def kernel(x_prompt, x_sample, norm_mix_g, norm_ffn_g, hy_w_in, hy_b_in, hy_conv_w, hy_conv_b, hy_f_w1, hy_f_b1,
           hy_f_w2, hy_f_b2, hy_f_w3, hy_f_b3, hy_f_wout, hy_f_freq, hy_skip, hy_w_out, hy_b_out, at_w_qkv, at_q_g,
           at_k_g, at_sink, at_w_o, rel_bias, ffn_w_gate_up, ffn_w_down):
    per_layer = lambda w: [w[i].astype(BF16) for i in range(w.shape[0])]
    p = dict(
        norm_mix_g=norm_mix_g.astype(F32), norm_ffn_g=norm_ffn_g.astype(F32),
        hy_w_in_t=[hy_w_in[j].T.astype(BF16) for j in range(hy_w_in.shape[0])], hy_b_in=hy_b_in,
        hy_conv_w=hy_conv_w, hy_conv_b=hy_conv_b,
        hy_f_w1=hy_f_w1, hy_f_b1=hy_f_b1, hy_f_w2=hy_f_w2, hy_f_b2=hy_f_b2, hy_f_w3=hy_f_w3, hy_f_b3=hy_f_b3,
        hy_f_wout=hy_f_wout, hy_f_freq=hy_f_freq, hy_skip=hy_skip,
        hy_w_out=per_layer(hy_w_out), hy_b_out=hy_b_out,
        at_w_qkv=per_layer(at_w_qkv), at_q_g=at_q_g, at_k_g=at_k_g, at_sink=at_sink,
        at_w_o=per_layer(at_w_o), rel_bias=rel_bias,
        ffn_w_gate_up=per_layer(ffn_w_gate_up), ffn_w_down=per_layer(ffn_w_down),
    )
    y_prompt = _trunk(x_prompt, p, dict(hy_cb=32))
    y_sample = _trunk(x_sample, p, dict(hy_cb=32))
    return (y_prompt, y_sample)
```

```python
import functools
import math

import jax
import jax.numpy as jnp
import numpy as np
from jax import lax
from jax.experimental import pallas as pl
from jax.experimental.pallas import tpu as pltpu

F32 = jnp.float32
BF16 = jnp.bfloat16

D_MODEL = 2048
DEPTH = 4
N_MIXERS = 2
HY_ORDER = 2
HY_EMB = 33
HY_BANDS = (HY_EMB - 1) // 2
HY_FILTER_W = 64
HY_FAST_PCT = 0.3
HY_SLOW_PCT = 1.5
HY_TARGET = 1e-2
N_HEADS = 16
HEAD_DIM = 128
N_KV = 4
GQA_G = N_HEADS // N_KV
WINDOW = 128
BLOCK = 128
N_BUCKETS = 32
MAX_DIST = 128
D_FF = -(-(8 * D_MODEL) // (3 * 256)) * 256
EPS = 1e-6
NEG = -1e30

V7X_LANES = 128
V7X_SUBLANES = 8
VMEM_LIMIT = 56 * 1024 * 1024

FFT_P = V7X_LANES
CH_GROUP = 4 * V7X_SUBLANES


def _cparams(*sem):
    return pltpu.CompilerParams(dimension_semantics=sem, vmem_limit_bytes=VMEM_LIMIT)


def _rms_bf16(x, g):
    ms = jnp.mean(x * x, axis=-1, keepdims=True)
    return (x * lax.rsqrt(ms + EPS) * g).astype(BF16)


def _qkv_proj_body(x_ref, g_ref, w_ref, qg_ref, kg_ref, o_ref):
    hn = _rms_bf16(x_ref[...], g_ref[...])
    nq, nk = N_HEADS * HEAD_DIM, N_KV * HEAD_DIM
    pair = 2 * HEAD_DIM
    for c0 in range(0, nq + nk, pair):
        acc = jnp.dot(hn, w_ref[:, c0:c0 + pair], preferred_element_type=F32)
        gain = qg_ref[...] if c0 < nq else kg_ref[...]
        for c in (0, HEAD_DIM):
            t = acc[:, c:c + HEAD_DIM]
            ms = jnp.mean(t * t, axis=-1, keepdims=True)
            o_ref[:, c0 + c:c0 + c + HEAD_DIM] = (t * lax.rsqrt(ms + EPS) * gain).astype(o_ref.dtype)
    o_ref[:, nq + nk:] = jnp.dot(hn, w_ref[:, nq + nk:], preferred_element_type=F32).astype(o_ref.dtype)


def qkv_projection(x, g, w, q_g, k_g, *, bm):
    T, K = x.shape
    N = w.shape[1]
    return pl.pallas_call(
        _qkv_proj_body,
        grid=(T // bm,),
        in_specs=[
            pl.BlockSpec((bm, K), lambda i: (i, 0)),
            pl.BlockSpec((1, K), lambda i: (0, 0)),
            pl.BlockSpec((K, N), lambda i: (0, 0), pipeline_mode=pl.Buffered(1)),
            pl.BlockSpec((1, HEAD_DIM), lambda i: (0, 0)),
            pl.BlockSpec((1, HEAD_DIM), lambda i: (0, 0)),
        ],
        out_specs=pl.BlockSpec((bm, N), lambda i: (i, 0)),
        out_shape=jax.ShapeDtypeStruct((T, N), BF16),
        compiler_params=_cparams("parallel"),
        name="qkv_projection",
    )(x, g.reshape(1, K), w, q_g.astype(F32).reshape(1, HEAD_DIM), k_g.astype(F32).reshape(1, HEAD_DIM))


IN_BT = V7X_SUBLANES * V7X_LANES
IN_ROW_SPLIT = 2


def _hyena_edge_body(x_ref, g_ref, w_ref, b_ref, o_ref):
    hn = _rms_bf16(x_ref[0], g_ref[...])
    o_ref[0] = lax.dot_general(hn, w_ref[...], (((1,), (1,)), ((), ())), preferred_element_type=F32) + b_ref[...]


def hyena_edge_projection(x_edge, g, w_t, bias, *, bc):
    B, n, K = x_edge.shape
    C = w_t.shape[0]
    return pl.pallas_call(
        _hyena_edge_body,
        grid=(B, C // bc),
        in_specs=[
            pl.BlockSpec((1, n, K), lambda b, c: (b, 0, 0)),
            pl.BlockSpec((1, K), lambda b, c: (0, 0)),
            pl.BlockSpec((bc, K), lambda b, c: (c, 0)),
            pl.BlockSpec((1, bc), lambda b, c: (0, c)),
        ],
        out_specs=pl.BlockSpec((1, n, bc), lambda b, c: (b, 0, c)),
        out_shape=jax.ShapeDtypeStruct((B, n, C), F32),
        compiler_params=_cparams("parallel", "parallel"),
        name="hyena_edge_projection",
    )(x_edge, g.reshape(1, K), w_t, bias.astype(F32).reshape(1, C))


def _hyena_in_body(x_ref, g_ref, w_ref, b_ref, k0_ref, k1_ref, k2_ref, kb_ref, edge_ref, o_ref, hn_ref):
    t = pl.program_id(1)
    nt = pl.num_programs(1)

    @pl.when(pl.program_id(2) == 0)
    def _():
        hn_ref[...] = _rms_bf16(x_ref[0], g_ref[...])

    P = V7X_LANES
    bt = hn_ref.shape[0]
    n_chunks = bt // P
    rows = w_ref.shape[0] // IN_ROW_SPLIT
    for r in range(IN_ROW_SPLIT):
        rs = slice(r * rows, (r + 1) * rows)
        u = lax.dot_general(w_ref[rs, :], hn_ref[...], (((1,), (1,)), ((), ())), preferred_element_type=F32)
        def border(k):
            parts = [jnp.transpose(jnp.broadcast_to(edge_ref[0, k, 0:1, c0:c0 + P], (P, P)))
                     for c0 in range(r * rows, (r + 1) * rows, P)]
            return jnp.concatenate(parts, axis=0)

        left = jnp.where(t == 0, 0.0, border(t))
        right = jnp.where(t == nt - 1, 0.0, border(nt + t))
        lane = lax.broadcasted_iota(jnp.int32, (rows, P), 1)
        bias, k0, k1, k2, kb = (ref[rs, :] for ref in (b_ref, k0_ref, k1_ref, k2_ref, kb_ref))
        chunks = [u[:, j * P:(j + 1) * P] + bias for j in range(n_chunks)]
        fwd = [pltpu.roll(c, 1, axis=1) for c in chunks]
        bwd = [pltpu.roll(c, P - 1, axis=1) for c in chunks]
        for j in range(n_chunks):
            prev = jnp.where(lane == 0, fwd[j - 1] if j > 0 else left, fwd[j])
            nxt = jnp.where(lane == P - 1, bwd[j + 1] if j + 1 < n_chunks else right, bwd[j])
            o_ref[pl.ds(r * rows * n_chunks + j, rows, stride=n_chunks), :] = (
                k0 * prev + k1 * chunks[j] + k2 * nxt + kb)


def hyena_in_projection(x, g, w_t, bias, conv_w, conv_b, *, bc):
    B, L, K = x.shape
    C = w_t.shape[0]
    bt = IN_BT
    nt = L // bt
    xb = x.reshape(B, nt, bt, K)
    x_edge = jnp.concatenate([jnp.roll(xb[:, :, bt - 1], 1, axis=1), jnp.roll(xb[:, :, 0], -1, axis=1)], axis=1)
    edge = hyena_edge_projection(x_edge, g, w_t, bias, bc=bc)
    edge = jnp.broadcast_to(edge[:, :, None, :], (B, 2 * nt, V7X_SUBLANES, C))
    col = lambda v: jnp.broadcast_to(v.astype(F32).reshape(C, 1), (C, V7X_LANES))
    cspec = pl.BlockSpec((bc, V7X_LANES), lambda b, t, c: (c, 0))
    return pl.pallas_call(
        _hyena_in_body,
        grid=(B, nt, C // bc),
        in_specs=[
            pl.BlockSpec((1, bt, K), lambda b, t, c: (b, t, 0)),
            pl.BlockSpec((1, K), lambda b, t, c: (0, 0)),
            pl.BlockSpec((bc, K), lambda b, t, c: (c, 0)),
            cspec, cspec, cspec, cspec, cspec,
            pl.BlockSpec((1, 2 * nt, V7X_SUBLANES, bc), lambda b, t, c: (b, 0, 0, c)),
        ],
        out_specs=pl.BlockSpec((None, None, bc * V7X_SUBLANES, V7X_LANES), lambda b, t, c: (b, t, c, 0)),
        out_shape=jax.ShapeDtypeStruct((B, nt, C * V7X_SUBLANES, V7X_LANES), F32),
        scratch_shapes=[pltpu.VMEM((bt, K), BF16)],
        compiler_params=_cparams("parallel", "parallel", "arbitrary"),
        name="hyena_in_projection",
    )(x, g.reshape(1, K), w_t, col(bias), col(conv_w[0]), col(conv_w[1]), col(conv_w[2]), col(conv_b), edge)


def _res_mm_body(a_ref, w_ref, x_ref, o_ref):
    o_ref[...] = x_ref[...] + jnp.dot(a_ref[...], w_ref[...], preferred_element_type=F32)


def residual_matmul(x, a, w, *, bm):
    T, K = a.shape
    N = w.shape[1]
    return pl.pallas_call(
        _res_mm_body,
        grid=(T // bm,),
        in_specs=[
            pl.BlockSpec((bm, K), lambda i: (i, 0)),
            pl.BlockSpec((K, N), lambda i: (0, 0), pipeline_mode=pl.Buffered(1)),
            pl.BlockSpec((bm, N), lambda i: (i, 0)),
        ],
        out_specs=pl.BlockSpec((bm, N), lambda i: (i, 0)),
        out_shape=jax.ShapeDtypeStruct((T, N), F32),
        compiler_params=_cparams("parallel"),
        name="residual_matmul",
    )(a, w, x)


def _res_mm_t_body(z_ref, w_ref, b_ref, x_ref, o_ref):
    z = z_ref[0].astype(BF16)
    y = lax.dot_general(z, w_ref[...], (((0,), (0,)), ((), ())), preferred_element_type=F32)
    o_ref[0] = x_ref[0] + y + b_ref[...]


def residual_matmul_t(x, z_t, w, bias, *, bt):
    B, K, L = z_t.shape
    N = w.shape[1]
    return pl.pallas_call(
        _res_mm_t_body,
        grid=(B, L // bt),
        in_specs=[
            pl.BlockSpec((1, K, bt), lambda b, t: (b, 0, t)),
            pl.BlockSpec((K, N), lambda b, t: (0, 0), pipeline_mode=pl.Buffered(1)),
            pl.BlockSpec((1, N), lambda b, t: (0, 0)),
            pl.BlockSpec((1, bt, N), lambda b, t: (b, t, 0)),
        ],
        out_specs=pl.BlockSpec((1, bt, N), lambda b, t: (b, t, 0)),
        out_shape=jax.ShapeDtypeStruct((B, L, N), F32),
        compiler_params=_cparams("parallel", "parallel"),
        name="residual_matmul_t",
    )(z_t, w, bias.reshape(1, N), x)


def _ffn_body(x_ref, g_ref, wg_ref, wu_ref, wd_ref, o_ref, hn_ref):
    @pl.when(pl.program_id(1) == 0)
    def _():
        x = x_ref[...]
        hn_ref[...] = _rms_bf16(x, g_ref[...])
        o_ref[...] = x

    h = hn_ref[...]
    gate = jnp.dot(h, wg_ref[...], preferred_element_type=F32)
    up = jnp.dot(h, wu_ref[...], preferred_element_type=F32)
    act = (gate * jax.nn.sigmoid(gate) * up).astype(BF16)
    o_ref[...] += jnp.dot(act, wd_ref[...], preferred_element_type=F32)


def ffn_block(x, g, w_gate_up, w_down, *, bm, bf):
    T, K = x.shape
    nf = D_FF // bf
    return pl.pallas_call(
        _ffn_body,
        grid=(T // bm, nf),
        in_specs=[
            pl.BlockSpec((bm, K), lambda i, f: (i, 0)),
            pl.BlockSpec((1, K), lambda i, f: (0, 0)),
            pl.BlockSpec((K, bf), lambda i, f: (0, f)),
            pl.BlockSpec((K, bf), lambda i, f: (0, f + nf)),
            pl.BlockSpec((bf, K), lambda i, f: (f, 0)),
        ],
        out_specs=pl.BlockSpec((bm, K), lambda i, f: (i, 0)),
        out_shape=jax.ShapeDtypeStruct((T, K), F32),
        scratch_shapes=[pltpu.VMEM((bm, K), BF16)],
        compiler_params=_cparams("parallel", "arbitrary"),
        name="ffn_block",
    )(x, g.reshape(1, K), w_gate_up, w_gate_up, w_down)


ATT_TQ = 4 * BLOCK
KV_W = N_KV * HEAD_DIM


def _band_structure():
    qi = np.arange(BLOCK)[:, None]
    ki = np.arange(3 * BLOCK)[None, :]
    rel = ki - BLOCK - qi
    nb = N_BUCKETS // 2
    max_exact = nb // 2
    n = np.abs(rel)
    large = max_exact + (np.log(np.maximum(n, 1) / max_exact) / math.log(MAX_DIST / max_exact)
                         * (nb - max_exact)).astype(np.int32)
    large = np.minimum(large, nb - 1)
    buckets = (rel > 0).astype(np.int32) * nb + np.where(n < max_exact, n, large).astype(np.int32)
    band = n <= WINDOW
    return buckets, band


def _attn_body(q_ref, kp_ref, kc_ref, kn_ref, vp_ref, vc_ref, vn_ref, bias_ref, sink_ref,
               o_ref, kbuf, vbuf):
    i = pl.program_id(1)
    last = pl.num_programs(1) - 1

    kbuf[0:BLOCK] = kp_ref[0]
    kbuf[BLOCK:BLOCK + ATT_TQ] = kc_ref[0]
    kbuf[BLOCK + ATT_TQ:] = kn_ref[0]
    ones = jnp.ones((ATT_TQ + 2 * BLOCK, HEAD_DIM), BF16)
    for g in range(N_KV):
        src = slice(g * HEAD_DIM, (g + 1) * HEAD_DIM)
        dst = slice(2 * g * HEAD_DIM, (2 * g + 1) * HEAD_DIM)
        vbuf[0:BLOCK, dst] = vp_ref[0, :, src]
        vbuf[BLOCK:BLOCK + ATT_TQ, dst] = vc_ref[0, :, src]
        vbuf[BLOCK + ATT_TQ:, dst] = vn_ref[0, :, src]
        vbuf[:, (2 * g + 1) * HEAD_DIM:(2 * g + 2) * HEAD_DIM] = ones

    lane = lax.broadcasted_iota(jnp.int32, (1, 3 * BLOCK), 1)
    first_edge = jnp.where((lane < BLOCK) & (i == 0), NEG, 0.0).astype(F32)
    last_edge = jnp.where((lane >= 2 * BLOCK) & (i == last), NEG, 0.0).astype(F32)
    exp2_scale = HEAD_DIM ** -0.5 * math.log2(math.e)

    n_sub = ATT_TQ // BLOCK
    groups = range(N_KV)

    def scores(j):
        r0 = j * BLOCK
        s = []
        for g in groups:
            qs = jnp.concatenate(
                [q_ref[0, r0:r0 + BLOCK, (GQA_G * g + h) * HEAD_DIM:(GQA_G * g + h + 1) * HEAD_DIM]
                 for h in range(GQA_G)], axis=0)
            kw = kbuf[r0:r0 + 3 * BLOCK, g * HEAD_DIM:(g + 1) * HEAD_DIM]
            sg = lax.dot_general(qs, kw, (((1,), (1,)), ((), ())), preferred_element_type=F32)
            sg = sg + bias_ref[g]
            if j == 0:
                sg = sg + first_edge
            if j == n_sub - 1:
                sg = sg + last_edge
            s.append(sg)
        return s

    def softmax_numerators(s):
        sk = [sink_ref[g][:, 0:1] for g in groups]
        m = [jnp.maximum(jnp.max(s[g], axis=-1, keepdims=True), sk[g]) for g in groups]
        p = [jnp.exp2((s[g] - m[g]) * exp2_scale).astype(BF16) for g in groups]
        sink_p = [jnp.exp2((sk[g] - m[g]) * exp2_scale) for g in groups]
        return p, sink_p

    def outputs(j, p, sink_p):
        r0 = j * BLOCK
        for g in groups:
            vw = vbuf[r0:r0 + 3 * BLOCK, 2 * g * HEAD_DIM:(2 * g + 2) * HEAD_DIM]
            pv = jnp.dot(p[g], vw, preferred_element_type=F32)
            o = pv[:, :HEAD_DIM] / (pv[:, HEAD_DIM:] + sink_p[g])
            for h in range(GQA_G):
                c0 = (GQA_G * g + h) * HEAD_DIM
                o_ref[0, r0:r0 + BLOCK, c0:c0 + HEAD_DIM] = o[h * BLOCK:(h + 1) * BLOCK].astype(o_ref.dtype)

    s_q, p_q = {}, {}
    for step in range(n_sub + 2):
        if step >= 2:
            outputs(step - 2, *p_q.pop(step - 2))
        if 1 <= step <= n_sub:
            p_q[step - 1] = softmax_numerators(s_q.pop(step - 1))
        if step < n_sub:
            s_q[step] = scores(step)


def window_attention(qkv, sink, rel_bias):
    B, L, _ = qkv.shape
    nq = N_HEADS * HEAD_DIM
    sub = ATT_TQ // BLOCK
    nblk = L // BLOCK
    kcol = nq // KV_W
    vcol = kcol + 1

    buckets, band = _band_structure()
    onehot = (jnp.asarray(buckets)[:, :, None] == jnp.arange(N_BUCKETS)).astype(F32)
    bias = jnp.einsum("qkn,nh->qkh", onehot, rel_bias.astype(F32), precision=lax.Precision.HIGHEST)
    inv_scale = HEAD_DIM ** 0.5
    bias = jnp.where(band[:, :, None], bias * inv_scale, NEG)
    bias = jnp.transpose(bias, (2, 0, 1)).reshape(N_KV, GQA_G * BLOCK, 3 * BLOCK)
    sink_rows = jnp.broadcast_to((sink.astype(F32) * inv_scale).reshape(N_KV, GQA_G, 1, 1),
                                 (N_KV, GQA_G, BLOCK, V7X_LANES)).reshape(N_KV, GQA_G * BLOCK, V7X_LANES)

    return pl.pallas_call(
        _attn_body,
        grid=(B, L // ATT_TQ),
        in_specs=[
            pl.BlockSpec((1, ATT_TQ, nq), lambda b, i: (b, i, 0)),
            pl.BlockSpec((1, BLOCK, KV_W), lambda b, i: (b, jnp.maximum(sub * i - 1, 0), kcol)),
            pl.BlockSpec((1, ATT_TQ, KV_W), lambda b, i: (b, i, kcol)),
            pl.BlockSpec((1, BLOCK, KV_W), lambda b, i: (b, jnp.minimum(sub * i + sub, nblk - 1), kcol)),
            pl.BlockSpec((1, BLOCK, KV_W), lambda b, i: (b, jnp.maximum(sub * i - 1, 0), vcol)),
            pl.BlockSpec((1, ATT_TQ, KV_W), lambda b, i: (b, i, vcol)),
            pl.BlockSpec((1, BLOCK, KV_W), lambda b, i: (b, jnp.minimum(sub * i + sub, nblk - 1), vcol)),
            pl.BlockSpec((N_KV, GQA_G * BLOCK, 3 * BLOCK), lambda b, i: (0, 0, 0)),
            pl.BlockSpec((N_KV, GQA_G * BLOCK, V7X_LANES), lambda b, i: (0, 0, 0)),
        ],
        out_specs=pl.BlockSpec((1, ATT_TQ, nq), lambda b, i: (b, i, 0)),
        out_shape=jax.ShapeDtypeStruct((B, L, nq), BF16),
        scratch_shapes=[pltpu.VMEM((ATT_TQ + 2 * BLOCK, KV_W), BF16),
                        pltpu.VMEM((ATT_TQ + 2 * BLOCK, 2 * KV_W), BF16)],
        compiler_params=_cparams("parallel", "parallel"),
        name="window_attention",
    )(qkv, qkv, qkv, qkv, qkv, qkv, qkv, bias, sink_rows)


FEAT_PAD = V7X_LANES


def _filter_mlp_body(ff_ref, fb_ref, w1_ref, b1_ref, w2_ref, b2_ref, w3_ref, b3_ref, fr_ref, af_ref, ab_ref):
    fr = fr_ref[...]
    for feat_ref, a_ref in ((ff_ref, af_ref), (fb_ref, ab_ref)):
        a = jnp.sin(fr * (jnp.dot(w1_ref[...], feat_ref[...], preferred_element_type=F32) + b1_ref[...]))
        a = jnp.sin(fr * (jnp.dot(w2_ref[...], a.astype(BF16), preferred_element_type=F32) + b2_ref[...]))
        a = jnp.sin(fr * (jnp.dot(w3_ref[...], a.astype(BF16), preferred_element_type=F32) + b3_ref[...]))
        a_ref[...] = a.astype(BF16)


def hyena_filter_mlp(L, f_w1, f_b1, f_w2, f_b2, f_w3, f_b3, f_freq, *, bt=2048):
    W = HY_FILTER_W
    f = jnp.linspace(1e-4, HY_BANDS - 1, HY_BANDS, dtype=F32)

    def features(pos):
        t = pos / (L - 1)
        ang = (2.0 * math.pi * pos / L)[None, :] * f[:, None]
        rows = jnp.concatenate([t[None, :], jnp.cos(ang), -jnp.sin(ang), jnp.zeros((FEAT_PAD - HY_EMB, L), F32)])
        return rows.astype(BF16)

    pos = jnp.arange(L, dtype=F32)
    w1_t = jnp.concatenate([f_w1.astype(F32).T, jnp.zeros((W, FEAT_PAD - HY_EMB), F32)], axis=1)
    col = lambda v: v.astype(F32).reshape(W, 1)
    const = lambda r, c: pl.BlockSpec((r, c), lambda i: (0, 0))
    fspec = pl.BlockSpec((FEAT_PAD, bt), lambda i: (0, i))
    ospec = pl.BlockSpec((W, bt), lambda i: (0, i))
    return pl.pallas_call(
        _filter_mlp_body,
        grid=(L // bt,),
        in_specs=[fspec, fspec, const(W, FEAT_PAD), const(W, 1), const(W, W), const(W, 1), const(W, W), const(W, 1),
                  const(W, 1)],
        out_specs=[ospec, ospec],
        out_shape=[jax.ShapeDtypeStruct((W, L), BF16), jax.ShapeDtypeStruct((W, L), BF16)],
        compiler_params=_cparams("parallel"),
        name="hyena_filter_mlp",
    )(features(pos), features(L - pos), w1_t.astype(BF16), col(f_b1), f_w2.T.astype(BF16), col(f_b2),
      f_w3.T.astype(BF16), col(f_b3), col(f_freq))


def _dft_tables(L, paired):
    N = 2 * L
    P = FFT_P
    Q = N // P
    S = L // P
    b = np.arange(Q, dtype=np.float64)
    fq = np.exp(-2j * np.pi * np.outer(b, np.arange(S)) / Q)
    fp =np.exp(-2j * np.pi * np.outer(np.arange(P), np.arange(P)) / P)
    tw = np.exp(-2j * np.pi * np.outer(b, np.arange(P)) / N)
    ci = np.conj(fq).T / N

    def stack(c):
        return np.block([[c.real, -c.imag], [c.imag, c.real]])

    if paired:
        g1 = stack(fq)
        g4 = stack(ci)
    else:
        g1 = np.concatenate([fq.real, fq.imag], axis=0)
        g4 = np.concatenate([ci.real, -ci.imag], axis=1)
    fq_full = np.exp(-2j * np.pi * np.outer(b, np.arange(Q)) / Q)
    g1_full = np.concatenate([fq_full.real, fq_full.imag], axis=0)
    g2 =np.block([[fp.real, fp.imag], [-fp.imag, fp.real]])
    g2c = np.block([[fp.real, -fp.imag], [fp.imag, fp.real]])
    twr = np.tile(tw.real, (1, 2))
    twi = np.tile(tw.imag, (1, 2))
    f32 = lambda a: np.ascontiguousarray(a, dtype=np.float32)
    return dict(g1=f32(g1), g4=f32(g4), g1_full=f32(g1_full), g2=f32(g2), g2c=f32(g2c),
                twr=f32(twr), twi=f32(twi), Q=Q, S=S, N=N)


def _to_tiles(x, n_tiles):
    chunks = [x[:, s * FFT_P:(s + 1) * FFT_P] for s in range(n_tiles)]
    return jnp.swapaxes(jnp.stack(chunks, axis=0), 0, 1)


def _from_tiles(x):
    y = jnp.swapaxes(x, 0, 1)
    return jnp.concatenate([y[s] for s in range(y.shape[0])], axis=1)


def _fwd_fft(re_tiles, im_tiles, g1, twr, twi, g2):
    G = len(re_tiles)
    Q = twr.shape[0]
    P = FFT_P
    rows = []
    for c in range(0, G, 2):
        top = jnp.concatenate([re_tiles[c], re_tiles[c + 1]], axis=1)
        if im_tiles is None:
            rhs = top
        else:
            rhs = jnp.concatenate([top, jnp.concatenate([im_tiles[c], im_tiles[c + 1]], axis=1)], axis=0)
        y = jnp.dot(g1, rhs.astype(BF16), preferred_element_type=F32)
        y = y.astype(twr.dtype)
        yr, yi = y[:Q], y[Q:]
        zr = yr * twr - yi * twi
        zi = yr * twi + yi * twr
        rows.append(jnp.concatenate([zr[:, :P], zi[:, :P]], axis=1))
        rows.append(jnp.concatenate([zr[:, P:], zi[:, P:]], axis=1))
    lhs = jnp.concatenate(rows, axis=0).astype(BF16)
    return jnp.dot(lhs, g2, preferred_element_type=F32)


def _inv_fft(spec, g2c, twr, twi, g4, want_imag):
    Q = twr.shape[0]
    P = FFT_P
    G = spec.shape[0] // Q
    S = g4.shape[0] // 2 if want_imag else g4.shape[0]
    y = jnp.dot(spec.astype(BF16), g2c, preferred_element_type=F32)
    y = y.astype(twr.dtype)
    out_re, out_im = [], []
    for c in range(0, G, 2):
        ya = y[c * Q:(c + 1) * Q]
        yb = y[(c + 1) * Q:(c + 2) * Q]
        yr = jnp.concatenate([ya[:, :P], yb[:, :P]], axis=1)
        yi = jnp.concatenate([ya[:, P:], yb[:, P:]], axis=1)
        zr = yr * twr + yi * twi
        zi = yi * twr - yr * twi
        rhs = jnp.concatenate([zr, zi], axis=0).astype(BF16)
        o = jnp.dot(g4, rhs, preferred_element_type=F32)
        out_re += [o[:S, :P], o[:S, P:]]
        if want_imag:
            out_im += [o[S:, :P], o[S:, P:]]
    return out_re, out_im


def _hyena_body(v_ref, x1_ref, x2_ref, af_ref, ab_ref, wof_ref, wob_ref, delta_ref, tf_ref, tb_ref, skip_ref,
                g1_ref, g1f_ref, twr_ref, twi_ref, g2_ref, g2c_ref, g4_ref, o_ref, taps_ref, kf_ref,
                *, S, Q, paired):
    g1, twr, twi = g1_ref[...], twr_ref[...], twi_ref[...]
    g2, g2c, g4 = g2_ref[...], g2c_ref[...], g4_ref[...]
    nb = 2 if paired else 1
    cb = kf_ref.shape[1]
    G = min(cb, CH_GROUP)
    n_groups = cb // G
    P = FFT_P

    @pl.when(pl.program_id(1) == 0)
    def _():
        g1f = g1f_ref[...]
        L = S * P
        delta = delta_ref[...]
        halves = ((wof_ref, af_ref, jnp.exp(-(delta * tf_ref[...]))),
                  (wob_ref, ab_ref, jnp.where(lax.broadcasted_iota(jnp.int32, (cb, L), 1) == 0, 0.0,
                                              jnp.exp(-(delta * tb_ref[...])))))
        for half, (wo_ref, a_ref, decay) in enumerate(halves):
            wo = wo_ref[...].reshape(HY_ORDER * cb, HY_FILTER_W)
            h = jnp.dot(wo, a_ref[...], preferred_element_type=F32)
            for o in range(HY_ORDER):
                taps_ref[o, :, half * L:(half + 1) * L] = h[o * cb:(o + 1) * cb] * decay

        def filter_group(gi, carry):
            c0 = pl.multiple_of(gi * G, G)
            for o in range(HY_ORDER):
                k = taps_ref[o, pl.ds(c0, G), :]
                norm = jnp.sum(jnp.abs(k), axis=-1, keepdims=True)
                tiles = _to_tiles(k, Q)
                spec = _fwd_fft([tiles[c] for c in range(G)], None, g1f, twr, twi, g2)
                kf = spec.reshape(G, Q, 2 * P) * (1.0 / norm)[:, :, None]
                kf_ref[o, pl.ds(c0, G)] = kf.astype(kf_ref.dtype)
            return carry

        lax.fori_loop(0, n_groups, filter_group, 0)

    def group(gi, carry):
        c0 = pl.multiple_of(gi * G, G)
        r0 = pl.multiple_of(gi * (G * V7X_SUBLANES), G * V7X_SUBLANES)

        def tiles(ref, b):
            blk = ref[b, :, pl.ds(r0, G * V7X_SUBLANES), :]
            return [blk[:, c * V7X_SUBLANES:(c + 1) * V7X_SUBLANES, :].reshape(S, P) for c in range(G)]

        z = [tiles(v_ref, b) for b in range(nb)]
        gates = [[tiles(x1_ref, b) for b in range(nb)], [tiles(x2_ref, b) for b in range(nb)]]
        for o in range(HY_ORDER):
            re = z[0]
            im = z[1] if paired else None
            spec = _fwd_fft(re, im, g1, twr, twi, g2)
            kf = kf_ref[o, pl.ds(c0, G)].reshape(G * Q, 2 * P)
            spec = spec.astype(kf.dtype)
            xr, xi = spec[:, :P], spec[:, P:]
            kr, ki = kf[:, :P], kf[:, P:]
            prod = jnp.concatenate([xr * kr - xi * ki, xr * ki + xi * kr], axis=1)
            out_re, out_im = _inv_fft(prod, g2c, twr, twi, g4, paired)
            skip = skip_ref[o, pl.ds(c0, G)]
            conv = [out_re] + ([out_im] if paired else [])
            z = [[gates[o][b][c] * (conv[b][c] + skip[c] * z[b][c]) for c in range(G)] for b in range(nb)]
        for b in range(nb):
            o_ref[b, pl.ds(c0, G), :] = _from_tiles(jnp.stack(z[b], axis=0))
        return carry

    lax.fori_loop(0, n_groups, group, 0)


def hyena_operator(u_tiles, a, a_b, f_wout, skip, tabs, *, cb):
    B, nt, rows, P = u_tiles.shape
    D = rows // (3 * V7X_SUBLANES)
    L = nt * IN_BT
    S = L // P
    Q = tabs["Q"]
    W = HY_FILTER_W
    paired = B % 2 == 0
    nb = 2 if paired else 1
    wo = f_wout.T.reshape(HY_ORDER, 2, D, W).astype(BF16)
    wo_f, wo_b = wo[:, 0], wo[:, 1]
    deltas = np.abs(np.linspace(math.log(HY_TARGET) / HY_SLOW_PCT,
                                math.log(HY_TARGET) / HY_FAST_PCT, D)).astype(np.float32).reshape(D, 1)
    t_f = jnp.linspace(0.0, 1.0, L, dtype=F32).reshape(1, L)
    t_b = ((L - jnp.arange(L, dtype=F32)) / (L - 1)).reshape(1, L)
    skip_rows = jnp.broadcast_to(skip.astype(F32)[:, :, None, None], (HY_ORDER, D, 1, P))

    bf = lambda name: jnp.asarray(tabs[name]).astype(BF16)
    g1, g1f, g2, g2c, g4 = bf("g1"), bf("g1_full"), bf("g2"), bf("g2c"), bf("g4")
    twr, twi = bf("twr"), bf("twi")
    const = lambda arr: pl.BlockSpec(arr.shape, lambda c, p: (0,) * arr.ndim, pipeline_mode=pl.Buffered(1))
    ncb = D // cb
    return pl.pallas_call(
        functools.partial(_hyena_body, S=S, Q=Q, paired=paired),
        grid=(ncb, B // nb),
        in_specs=[
            pl.BlockSpec((nb, nt, cb * V7X_SUBLANES, P), lambda c, p: (p, 0, c, 0)),
            pl.BlockSpec((nb, nt, cb * V7X_SUBLANES, P), lambda c, p: (p, 0, c + ncb, 0)),
            pl.BlockSpec((nb, nt, cb * V7X_SUBLANES, P), lambda c, p: (p, 0, c + 2 * ncb, 0)),
            const(a), const(a_b),
            pl.BlockSpec((HY_ORDER, cb, W), lambda c, p: (0, c, 0)),
            pl.BlockSpec((HY_ORDER, cb, W), lambda c, p: (0, c, 0)),
            pl.BlockSpec((cb, 1), lambda c, p: (c, 0)),
            const(t_f), const(t_b),
            pl.BlockSpec((HY_ORDER, cb, 1, P), lambda c, p: (0, c, 0, 0)),
            const(g1), const(g1f), const(twr), const(twi), const(g2), const(g2c), const(g4),
        ],
        out_specs=pl.BlockSpec((nb, cb, L), lambda c, p: (p, c, 0)),
        out_shape=jax.ShapeDtypeStruct((B, D, L), F32),
        scratch_shapes=[pltpu.VMEM((HY_ORDER, cb, 2 * L), F32), pltpu.VMEM((HY_ORDER, cb, Q, 2 * P), BF16)],
        compiler_params=_cparams("parallel", "arbitrary"),
        name="hyena_operator",
    )(u_tiles, u_tiles, u_tiles, a, a_b, wo_f, wo_b, jnp.asarray(deltas), t_f, t_b, skip_rows,
      g1, g1f, twr, twi, g2, g2c, g4)


def _trunk(x, p, cfg):
    B, L, D = x.shape
    tabs = _dft_tables(L, paired=(B % 2 == 0))
    for i in range(DEPTH):
        j = i // N_MIXERS
        if i % N_MIXERS == 0:
            a, a_b = hyena_filter_mlp(L, p["hy_f_w1"][j], p["hy_f_b1"][j], p["hy_f_w2"][j], p["hy_f_b2"][j],
                                      p["hy_f_w3"][j], p["hy_f_b3"][j], p["hy_f_freq"][j])
            u_tiles = hyena_in_projection(x, p["norm_mix_g"][i], p["hy_w_in_t"][j], p["hy_b_in"][j],
                                          p["hy_conv_w"][j], p["hy_conv_b"][j], bc=1024)
            z_t = hyena_operator(u_tiles, a, a_b, p["hy_f_wout"][j], p["hy_skip"][j], tabs, cb=cfg["hy_cb"])
            x = residual_matmul_t(x, z_t, p["hy_w_out"][j], p["hy_b_out"][j], bt=512)
        else:
            x2 = x.reshape(B * L, D)
            qkv = qkv_projection(x2, p["norm_mix_g"][i], p["at_w_qkv"][j], p["at_q_g"][j], p["at_k_g"][j], bm=1024)
            att = window_attention(qkv.reshape(B, L, -1), p["at_sink"][j], p["rel_bias"])
            x = residual_matmul(x2, att.reshape(B * L, -1), p["at_w_o"][j], bm=1024).reshape(B, L, D)
        x = ffn_block(x.reshape(B * L, D), p["norm_ffn_g"][i], p["ffn_w_gate_up"][i], p["ffn_w_down"][i],
                      bm=1024, bf=512).reshape(B, L, D)
    return x


def kernel(x_prompt, x_sample, norm_mix_g, norm_ffn_g, hy_w_in, hy_b_in, hy_conv_w, hy_conv_b, hy_f_w1, hy_f_b1,
           hy_f_w2, hy_f_b2, hy_f_w3, hy_f_b3, hy_f_wout, hy_f_freq, hy_skip, hy_w_out, hy_b_out, at_w_qkv, at_q_g,
           at_k_g, at_sink, at_w_o, rel_bias, ffn_w_gate_up, ffn_w_down):
    per_layer = lambda w: [w[i].astype(BF16) for i in range(w.shape[0])]
    p = dict(
        norm_mix_g=norm_mix_g.astype(F32), norm_ffn_g=norm_ffn_g.astype(F32),
        hy_w_in_t=[hy_w_in[j].T.astype(BF16) for j in range(hy_w_in.shape[0])], hy_b_in=hy_b_in,
        hy_conv_w=hy_conv_w, hy_conv_b=hy_conv_b,
        hy_f_w1=hy_f_w1, hy_f_b1=hy_f_b1, hy_f_w2=hy_f_w2, hy_f_b2=hy_f_b2, hy_f_w3=hy_f_w3, hy_f_b3=hy_f_b3,
        hy_f_wout=hy_f_wout, hy_f_freq=hy_f_freq, hy_skip=hy_skip,
        hy_w_out=per_layer(hy_w_out), hy_b_out=hy_b_out,
        at_w_qkv=per_layer(at_w_qkv), at_q_g=at_q_g, at_k_g=at_k_g, at_sink=at_sink,
        at_w_o=per_layer(at_w_o), rel_bias=rel_bias,
        ffn_w_gate_up=per_layer(ffn_w_gate_up), ffn_w_down=per_layer(ffn_w_down),
    )
    y_prompt = _trunk(x_prompt, p, dict(hy_cb=32))
    y_sample = _trunk(x_sample, p, dict(hy_cb=32))
    return (y_prompt, y_sample)
```

```python
import functools
import math

import jax
import jax.numpy as jnp
import numpy as np
from jax import lax
from jax.experimental import pallas as pl
from jax.experimental.pallas import tpu as pltpu

F32 = jnp.float32
BF16 = jnp.bfloat16

D_MODEL = 2048
DEPTH = 4
N_MIXERS = 2
HY_ORDER = 2
HY_EMB = 33
HY_BANDS = (HY_EMB - 1) // 2
HY_FILTER_W = 64
HY_FAST_PCT = 0.3
HY_SLOW_PCT = 1.5
HY_TARGET = 1e-2
N_HEADS = 16
HEAD_DIM = 128
N_KV = 4
GQA_G = N_HEADS // N_KV
WINDOW = 128
BLOCK = 128
N_BUCKETS = 32
MAX_DIST = 128
D_FF = -(-(8 * D_MODEL) // (3 * 256)) * 256
EPS = 1e-6
NEG = -1e30

V7X_LANES = 128
V7X_SUBLANES = 8
VMEM_LIMIT = 56 * 1024 * 1024

FFT_P = V7X_LANES
CH_GROUP = 4 * V7X_SUBLANES


def _cparams(*sem):
    return pltpu.CompilerParams(dimension_semantics=sem, vmem_limit_bytes=VMEM_LIMIT)


def _rms_bf16(x, g):
    ms = jnp.mean(x * x, axis=-1, keepdims=True)
    return (x * lax.rsqrt(ms + EPS) * g).astype(BF16)


def _qkv_proj_body(x_ref, g_ref, w_ref, qg_ref, kg_ref, o_ref):
    hn = _rms_bf16(x_ref[...], g_ref[...])
    nq, nk = N_HEADS * HEAD_DIM, N_KV * HEAD_DIM
    pair = 2 * HEAD_DIM
    for c0 in range(0, nq + nk, pair):
        acc = jnp.dot(hn, w_ref[:, c0:c0 + pair], preferred_element_type=F32)
        gain = qg_ref[...] if c0 < nq else kg_ref[...]
        for c in (0, HEAD_DIM):
            t = acc[:, c:c + HEAD_DIM]
            ms = jnp.mean(t * t, axis=-1, keepdims=True)
            o_ref[:, c0 + c:c0 + c + HEAD_DIM] = (t * lax.rsqrt(ms + EPS) * gain).astype(o_ref.dtype)
    o_ref[:, nq + nk:] = jnp.dot(hn, w_ref[:, nq + nk:], preferred_element_type=F32).astype(o_ref.dtype)


def qkv_projection(x, g, w, q_g, k_g, *, bm):
    T, K = x.shape
    N = w.shape[1]
    return pl.pallas_call(
        _qkv_proj_body,
        grid=(T // bm,),
        in_specs=[
            pl.BlockSpec((bm, K), lambda i: (i, 0)),
            pl.BlockSpec((1, K), lambda i: (0, 0)),
            pl.BlockSpec((K, N), lambda i: (0, 0), pipeline_mode=pl.Buffered(1)),
            pl.BlockSpec((1, HEAD_DIM), lambda i: (0, 0)),
            pl.BlockSpec((1, HEAD_DIM), lambda i: (0, 0)),
        ],
        out_specs=pl.BlockSpec((bm, N), lambda i: (i, 0)),
        out_shape=jax.ShapeDtypeStruct((T, N), BF16),
        compiler_params=_cparams("parallel"),
        name="qkv_projection",
    )(x, g.reshape(1, K), w, q_g.astype(F32).reshape(1, HEAD_DIM), k_g.astype(F32).reshape(1, HEAD_DIM))


IN_BT = V7X_SUBLANES * V7X_LANES
IN_ROW_SPLIT = 4


def _hyena_edge_body(x_ref, g_ref, w_ref, b_ref, o_ref):
    hn = _rms_bf16(x_ref[0], g_ref[...])
    o_ref[0] = lax.dot_general(hn, w_ref[...], (((1,), (1,)), ((), ())), preferred_element_type=F32) + b_ref[...]


def hyena_edge_projection(x_edge, g, w_t, bias, *, bc):
    B, n, K = x_edge.shape
    C = w_t.shape[0]
    return pl.pallas_call(
        _hyena_edge_body,
        grid=(B, C // bc),
        in_specs=[
            pl.BlockSpec((1, n, K), lambda b, c: (b, 0, 0)),
            pl.BlockSpec((1, K), lambda b, c: (0, 0)),
            pl.BlockSpec((bc, K), lambda b, c: (c, 0)),
            pl.BlockSpec((1, bc), lambda b, c: (0, c)),
        ],
        out_specs=pl.BlockSpec((1, n, bc), lambda b, c: (b, 0, c)),
        out_shape=jax.ShapeDtypeStruct((B, n, C), F32),
        compiler_params=_cparams("parallel", "parallel"),
        name="hyena_edge_projection",
    )(x_edge, g.reshape(1, K), w_t, bias.astype(F32).reshape(1, C))


def _hyena_in_body(x_ref, g_ref, w_ref, b_ref, k0_ref, k1_ref, k2_ref, kb_ref, edge_ref, o_ref, hn_ref):
    t = pl.program_id(1)
    nt = pl.num_programs(1)

    @pl.when(pl.program_id(2) == 0)
    def _():
        hn_ref[...] = _rms_bf16(x_ref[0], g_ref[...])

    P = V7X_LANES
    bt = hn_ref.shape[0]
    n_chunks = bt // P
    rows = w_ref.shape[0] // IN_ROW_SPLIT
    for r in range(IN_ROW_SPLIT):
        rs = slice(r * rows, (r + 1) * rows)
        u = lax.dot_general(w_ref[rs, :], hn_ref[...], (((1,), (1,)), ((), ())), preferred_element_type=F32)
        def border(k):
            parts = [jnp.transpose(jnp.broadcast_to(edge_ref[0, k, 0:1, c0:c0 + P], (P, P)))
                     for c0 in range(r * rows, (r + 1) * rows, P)]
            return jnp.concatenate(parts, axis=0)

        left = jnp.where(t == 0, 0.0, border(t))
        right = jnp.where(t == nt - 1, 0.0, border(nt + t))
        lane = lax.broadcasted_iota(jnp.int32, (rows, P), 1)
        bias, k0, k1, k2, kb = (ref[rs, :] for ref in (b_ref, k0_ref, k1_ref, k2_ref, kb_ref))
        chunks = [u[:, j * P:(j + 1) * P] + bias for j in range(n_chunks)]
        fwd = [pltpu.roll(c, 1, axis=1) for c in chunks]
        bwd = [pltpu.roll(c, P - 1, axis=1) for c in chunks]
        for j in range(n_chunks):
            prev = jnp.where(lane == 0, fwd[j - 1] if j > 0 else left, fwd[j])
            nxt = jnp.where(lane == P - 1, bwd[j + 1] if j + 1 < n_chunks else right, bwd[j])
            o_ref[pl.ds(r * rows * n_chunks + j, rows, stride=n_chunks), :] = (
                k0 * prev + k1 * chunks[j] + k2 * nxt + kb)


def hyena_in_projection(x, g, w_t, bias, conv_w, conv_b, *, bc):
    B, L, K = x.shape
    C = w_t.shape[0]
    bt = IN_BT
    nt = L // bt
    xb = x.reshape(B, nt, bt, K)
    x_edge = jnp.concatenate([jnp.roll(xb[:, :, bt - 1], 1, axis=1), jnp.roll(xb[:, :, 0], -1, axis=1)], axis=1)
    edge = hyena_edge_projection(x_edge, g, w_t, bias, bc=bc)
    edge = jnp.broadcast_to(edge[:, :, None, :], (B, 2 * nt, V7X_SUBLANES, C))
    col = lambda v: jnp.broadcast_to(v.astype(F32).reshape(C, 1), (C, V7X_LANES))
    cspec = pl.BlockSpec((bc, V7X_LANES), lambda b, t, c: (c, 0))
    return pl.pallas_call(
        _hyena_in_body,
        grid=(B, nt, C // bc),
        in_specs=[
            pl.BlockSpec((1, bt, K), lambda b, t, c: (b, t, 0)),
            pl.BlockSpec((1, K), lambda b, t, c: (0, 0)),
            pl.BlockSpec((bc, K), lambda b, t, c: (c, 0)),
            cspec, cspec, cspec, cspec, cspec,
            pl.BlockSpec((1, 2 * nt, V7X_SUBLANES, bc), lambda b, t, c: (b, 0, 0, c)),
        ],
        out_specs=pl.BlockSpec((None, None, bc * V7X_SUBLANES, V7X_LANES), lambda b, t, c: (b, t, c, 0)),
        out_shape=jax.ShapeDtypeStruct((B, nt, C * V7X_SUBLANES, V7X_LANES), F32),
        scratch_shapes=[pltpu.VMEM((bt, K), BF16)],
        compiler_params=_cparams("parallel", "parallel", "arbitrary"),
        name="hyena_in_projection",
    )(x, g.reshape(1, K), w_t, col(bias), col(conv_w[0]), col(conv_w[1]), col(conv_w[2]), col(conv_b), edge)


def _res_mm_body(a_ref, w_ref, x_ref, o_ref):
    o_ref[...] = x_ref[...] + jnp.dot(a_ref[...], w_ref[...], preferred_element_type=F32)


def residual_matmul(x, a, w, *, bm):
    T, K = a.shape
    N = w.shape[1]
    return pl.pallas_call(
        _res_mm_body,
        grid=(T // bm,),
        in_specs=[
            pl.BlockSpec((bm, K), lambda i: (i, 0)),
            pl.BlockSpec((K, N), lambda i: (0, 0), pipeline_mode=pl.Buffered(1)),
            pl.BlockSpec((bm, N), lambda i: (i, 0)),
        ],
        out_specs=pl.BlockSpec((bm, N), lambda i: (i, 0)),
        out_shape=jax.ShapeDtypeStruct((T, N), F32),
        compiler_params=_cparams("parallel"),
        name="residual_matmul",
    )(a, w, x)


def _res_mm_t_body(z_ref, w_ref, b_ref, x_ref, o_ref):
    z = z_ref[0].astype(BF16)
    y = lax.dot_general(z, w_ref[...], (((0,), (0,)), ((), ())), preferred_element_type=F32)
    o_ref[0] = x_ref[0] + y + b_ref[...]


def residual_matmul_t(x, z_t, w, bias, *, bt):
    B, K, L = z_t.shape
    N = w.shape[1]
    return pl.pallas_call(
        _res_mm_t_body,
        grid=(B, L // bt),
        in_specs=[
            pl.BlockSpec((1, K, bt), lambda b, t: (b, 0, t)),
            pl.BlockSpec((K, N), lambda b, t: (0, 0), pipeline_mode=pl.Buffered(1)),
            pl.BlockSpec((1, N), lambda b, t: (0, 0)),
            pl.BlockSpec((1, bt, N), lambda b, t: (b, t, 0)),
        ],
        out_specs=pl.BlockSpec((1, bt, N), lambda b, t: (b, t, 0)),
        out_shape=jax.ShapeDtypeStruct((B, L, N), F32),
        compiler_params=_cparams("parallel", "parallel"),
        name="residual_matmul_t",
    )(z_t, w, bias.reshape(1, N), x)


def _ffn_body(x_ref, g_ref, wg_ref, wu_ref, wd_ref, o_ref, hn_ref):
    @pl.when(pl.program_id(1) == 0)
    def _():
        x = x_ref[...]
        hn_ref[...] = _rms_bf16(x, g_ref[...])
        o_ref[...] = x

    h = hn_ref[...]
    gate = jnp.dot(h, wg_ref[...], preferred_element_type=F32)
    up = jnp.dot(h, wu_ref[...], preferred_element_type=F32)
    act = (gate * jax.nn.sigmoid(gate) * up).astype(BF16)
    o_ref[...] += jnp.dot(act, wd_ref[...], preferred_element_type=F32)


def ffn_block(x, g, w_gate_up, w_down, *, bm, bf):
    T, K = x.shape
    nf = D_FF // bf
    return pl.pallas_call(
        _ffn_body,
        grid=(T // bm, nf),
        in_specs=[
            pl.BlockSpec((bm, K), lambda i, f: (i, 0)),
            pl.BlockSpec((1, K), lambda i, f: (0, 0)),
            pl.BlockSpec((K, bf), lambda i, f: (0, f)),
            pl.BlockSpec((K, bf), lambda i, f: (0, f + nf)),
            pl.BlockSpec((bf, K), lambda i, f: (f, 0)),
        ],
        out_specs=pl.BlockSpec((bm, K), lambda i, f: (i, 0)),
        out_shape=jax.ShapeDtypeStruct((T, K), F32),
        scratch_shapes=[pltpu.VMEM((bm, K), BF16)],
        compiler_params=_cparams("parallel", "arbitrary"),
        name="ffn_block",
    )(x, g.reshape(1, K), w_gate_up, w_gate_up, w_down)


ATT_TQ = 4 * BLOCK
KV_W = N_KV * HEAD_DIM


def _band_structure():
    qi = np.arange(BLOCK)[:, None]
    ki = np.arange(3 * BLOCK)[None, :]
    rel = ki - BLOCK - qi
    nb = N_BUCKETS // 2
    max_exact = nb // 2
    n = np.abs(rel)
    large = max_exact + (np.log(np.maximum(n, 1) / max_exact) / math.log(MAX_DIST / max_exact)
                         * (nb - max_exact)).astype(np.int32)
    large = np.minimum(large, nb - 1)
    buckets = (rel > 0).astype(np.int32) * nb + np.where(n < max_exact, n, large).astype(np.int32)
    band = n <= WINDOW
    return buckets, band


def _attn_body(q_ref, kp_ref, kc_ref, kn_ref, vp_ref, vc_ref, vn_ref, bias_ref, sink_ref,
               o_ref, kbuf, vbuf):
    i = pl.program_id(1)
    last = pl.num_programs(1) - 1

    kbuf[0:BLOCK] = kp_ref[0]
    kbuf[BLOCK:BLOCK + ATT_TQ] = kc_ref[0]
    kbuf[BLOCK + ATT_TQ:] = kn_ref[0]
    ones = jnp.ones((ATT_TQ + 2 * BLOCK, HEAD_DIM), BF16)
    for g in range(N_KV):
        src = slice(g * HEAD_DIM, (g + 1) * HEAD_DIM)
        dst = slice(2 * g * HEAD_DIM, (2 * g + 1) * HEAD_DIM)
        vbuf[0:BLOCK, dst] = vp_ref[0, :, src]
        vbuf[BLOCK:BLOCK + ATT_TQ, dst] = vc_ref[0, :, src]
        vbuf[BLOCK + ATT_TQ:, dst] = vn_ref[0, :, src]
        vbuf[:, (2 * g + 1) * HEAD_DIM:(2 * g + 2) * HEAD_DIM] = ones

    lane = lax.broadcasted_iota(jnp.int32, (1, 3 * BLOCK), 1)
    first_edge = jnp.where((lane < BLOCK) & (i == 0), NEG, 0.0).astype(F32)
    last_edge = jnp.where((lane >= 2 * BLOCK) & (i == last), NEG, 0.0).astype(F32)
    exp2_scale = HEAD_DIM ** -0.5 * math.log2(math.e)

    n_sub = ATT_TQ // BLOCK

    def scores(j, g):
        r0 = j * BLOCK
        qs = jnp.concatenate(
            [q_ref[0, r0:r0 + BLOCK, (GQA_G * g + h) * HEAD_DIM:(GQA_G * g + h + 1) * HEAD_DIM]
             for h in range(GQA_G)], axis=0)
        kw = kbuf[r0:r0 + 3 * BLOCK, g * HEAD_DIM:(g + 1) * HEAD_DIM]
        s = lax.dot_general(qs, kw, (((1,), (1,)), ((), ())), preferred_element_type=F32)
        s = s + bias_ref[g]
        if j == 0:
            s = s + first_edge
        if j == n_sub - 1:
            s = s + last_edge
        return s

    def softmax_numerators(g, s):
        sk = sink_ref[g][:, 0:1]
        m = jnp.maximum(jnp.max(s, axis=-1, keepdims=True), sk)
        return jnp.exp2((s - m) * exp2_scale).astype(BF16), jnp.exp2((sk - m) * exp2_scale)

    def outputs(j, g, p, sink_p):
        r0 = j * BLOCK
        vw = vbuf[r0:r0 + 3 * BLOCK, 2 * g * HEAD_DIM:(2 * g + 2) * HEAD_DIM]
        pv = jnp.dot(p, vw, preferred_element_type=F32)
        o = pv[:, :HEAD_DIM] / (pv[:, HEAD_DIM:] + sink_p)
        for h in range(GQA_G):
            c0 = (GQA_G * g + h) * HEAD_DIM
            o_ref[0, r0:r0 + BLOCK, c0:c0 + HEAD_DIM] = o[h * BLOCK:(h + 1) * BLOCK].astype(o_ref.dtype)

    chains = [(j, g) for j in range(n_sub) for g in range(N_KV)]
    s_q, p_q = {}, {}
    for step in range(len(chains) + 2):
        if step >= 2:
            outputs(*chains[step - 2], *p_q.pop(step - 2))
        if 1 <= step <= len(chains):
            p_q[step - 1] = softmax_numerators(chains[step - 1][1], s_q.pop(step - 1))
        if step < len(chains):
            s_q[step] = scores(*chains[step])


def window_attention(qkv, sink, rel_bias):
    B, L, _ = qkv.shape
    nq = N_HEADS * HEAD_DIM
    sub = ATT_TQ // BLOCK
    nblk = L // BLOCK
    kcol = nq // KV_W
    vcol = kcol + 1

    buckets, band = _band_structure()
    onehot = (jnp.asarray(buckets)[:, :, None] == jnp.arange(N_BUCKETS)).astype(F32)
    bias = jnp.einsum("qkn,nh->qkh", onehot, rel_bias.astype(F32), precision=lax.Precision.HIGHEST)
    inv_scale = HEAD_DIM ** 0.5
    bias = jnp.where(band[:, :, None], bias * inv_scale, NEG)
    bias = jnp.transpose(bias, (2, 0, 1)).reshape(N_KV, GQA_G * BLOCK, 3 * BLOCK)
    sink_rows = jnp.broadcast_to((sink.astype(F32) * inv_scale).reshape(N_KV, GQA_G, 1, 1),
                                 (N_KV, GQA_G, BLOCK, V7X_LANES)).reshape(N_KV, GQA_G * BLOCK, V7X_LANES)

    return pl.pallas_call(
        _attn_body,
        grid=(B, L // ATT_TQ),
        in_specs=[
            pl.BlockSpec((1, ATT_TQ, nq), lambda b, i: (b, i, 0)),
            pl.BlockSpec((1, BLOCK, KV_W), lambda b, i: (b, jnp.maximum(sub * i - 1, 0), kcol)),
            pl.BlockSpec((1, ATT_TQ, KV_W), lambda b, i: (b, i, kcol)),
            pl.BlockSpec((1, BLOCK, KV_W), lambda b, i: (b, jnp.minimum(sub * i + sub, nblk - 1), kcol)),
            pl.BlockSpec((1, BLOCK, KV_W), lambda b, i: (b, jnp.maximum(sub * i - 1, 0), vcol)),
            pl.BlockSpec((1, ATT_TQ, KV_W), lambda b, i: (b, i, vcol)),
            pl.BlockSpec((1, BLOCK, KV_W), lambda b, i: (b, jnp.minimum(sub * i + sub, nblk - 1), vcol)),
            pl.BlockSpec((N_KV, GQA_G * BLOCK, 3 * BLOCK), lambda b, i: (0, 0, 0)),
            pl.BlockSpec((N_KV, GQA_G * BLOCK, V7X_LANES), lambda b, i: (0, 0, 0)),
        ],
        out_specs=pl.BlockSpec((1, ATT_TQ, nq), lambda b, i: (b, i, 0)),
        out_shape=jax.ShapeDtypeStruct((B, L, nq), BF16),
        scratch_shapes=[pltpu.VMEM((ATT_TQ + 2 * BLOCK, KV_W), BF16),
                        pltpu.VMEM((ATT_TQ + 2 * BLOCK, 2 * KV_W), BF16)],
        compiler_params=_cparams("parallel", "parallel"),
        name="window_attention",
    )(qkv, qkv, qkv, qkv, qkv, qkv, qkv, bias, sink_rows)


FEAT_PAD = V7X_LANES


def _filter_mlp_body(ff_ref, fb_ref, w1_ref, b1_ref, w2_ref, b2_ref, w3_ref, b3_ref, fr_ref, af_ref, ab_ref):
    fr = fr_ref[...]
    for feat_ref, a_ref in ((ff_ref, af_ref), (fb_ref, ab_ref)):
        a = jnp.sin(fr * (jnp.dot(w1_ref[...], feat_ref[...], preferred_element_type=F32) + b1_ref[...]))
        a = jnp.sin(fr * (jnp.dot(w2_ref[...], a.astype(BF16), preferred_element_type=F32) + b2_ref[...]))
        a = jnp.sin(fr * (jnp.dot(w3_ref[...], a.astype(BF16), preferred_element_type=F32) + b3_ref[...]))
        a_ref[...] = a.astype(BF16)


def hyena_filter_mlp(L, f_w1, f_b1, f_w2, f_b2, f_w3, f_b3, f_freq, *, bt=2048):
    W = HY_FILTER_W
    f = jnp.linspace(1e-4, HY_BANDS - 1, HY_BANDS, dtype=F32)

    def features(pos):
        t = pos / (L - 1)
        ang = (2.0 * math.pi * pos / L)[None, :] * f[:, None]
        rows = jnp.concatenate([t[None, :], jnp.cos(ang), -jnp.sin(ang), jnp.zeros((FEAT_PAD - HY_EMB, L), F32)])
        return rows.astype(BF16)

    pos = jnp.arange(L, dtype=F32)
    w1_t = jnp.concatenate([f_w1.astype(F32).T, jnp.zeros((W, FEAT_PAD - HY_EMB), F32)], axis=1)
    col = lambda v: v.astype(F32).reshape(W, 1)
    const = lambda r, c: pl.BlockSpec((r, c), lambda i: (0, 0))
    fspec = pl.BlockSpec((FEAT_PAD, bt), lambda i: (0, i))
    ospec = pl.BlockSpec((W, bt), lambda i: (0, i))
    return pl.pallas_call(
        _filter_mlp_body,
        grid=(L // bt,),
        in_specs=[fspec, fspec, const(W, FEAT_PAD), const(W, 1), const(W, W), const(W, 1), const(W, W), const(W, 1),
                  const(W, 1)],
        out_specs=[ospec, ospec],
        out_shape=[jax.ShapeDtypeStruct((W, L), BF16), jax.ShapeDtypeStruct((W, L), BF16)],
        compiler_params=_cparams("parallel"),
        name="hyena_filter_mlp",
    )(features(pos), features(L - pos), w1_t.astype(BF16), col(f_b1), f_w2.T.astype(BF16), col(f_b2),
      f_w3.T.astype(BF16), col(f_b3), col(f_freq))


def _dft_tables(L, paired):
    N = 2 * L
    P = FFT_P
    Q = N // P
    S = L // P
    b = np.arange(Q, dtype=np.float64)
    fq = np.exp(-2j * np.pi * np.outer(b, np.arange(S)) / Q)
    fp =np.exp(-2j * np.pi * np.outer(np.arange(P), np.arange(P)) / P)
    tw = np.exp(-2j * np.pi * np.outer(b, np.arange(P)) / N)
    ci = np.conj(fq).T / N

    def stack(c):
        return np.block([[c.real, -c.imag], [c.imag, c.real]])

    if paired:
        g1 = stack(fq)
        g4 = stack(ci)
    else:
        g1 = np.concatenate([fq.real, fq.imag], axis=0)
        g4 = np.concatenate([ci.real, -ci.imag], axis=1)
    fq_full = np.exp(-2j * np.pi * np.outer(b, np.arange(Q)) / Q)
    g1_full = np.concatenate([fq_full.real, fq_full.imag], axis=0)
    g2 =np.block([[fp.real, fp.imag], [-fp.imag, fp.real]])
    g2c = np.block([[fp.real, -fp.imag], [fp.imag, fp.real]])
    twr = np.tile(tw.real, (1, 2))
    twi = np.tile(tw.imag, (1, 2))
    f32 = lambda a: np.ascontiguousarray(a, dtype=np.float32)
    return dict(g1=f32(g1), g4=f32(g4), g1_full=f32(g1_full), g2=f32(g2), g2c=f32(g2c),
                twr=f32(twr), twi=f32(twi), Q=Q, S=S, N=N)


def _to_tiles(x, n_tiles):
    chunks = [x[:, s * FFT_P:(s + 1) * FFT_P] for s in range(n_tiles)]
    return jnp.swapaxes(jnp.stack(chunks, axis=0), 0, 1)


def _from_tiles(x):
    y = jnp.swapaxes(x, 0, 1)
    return jnp.concatenate([y[s] for s in range(y.shape[0])], axis=1)


def _fwd_fft(re_tiles, im_tiles, g1, twr, twi, g2):
    G = len(re_tiles)
    Q = twr.shape[0]
    P = FFT_P
    rows = []
    for c in range(0, G, 2):
        top = jnp.concatenate([re_tiles[c], re_tiles[c + 1]], axis=1)
        if im_tiles is None:
            rhs = top
        else:
            rhs = jnp.concatenate([top, jnp.concatenate([im_tiles[c], im_tiles[c + 1]], axis=1)], axis=0)
        y = jnp.dot(g1, rhs.astype(BF16), preferred_element_type=F32)
        y = y.astype(twr.dtype)
        yr, yi = y[:Q], y[Q:]
        zr = yr * twr - yi * twi
        zi = yr * twi + yi * twr
        rows.append(jnp.concatenate([zr[:, :P], zi[:, :P]], axis=1))
        rows.append(jnp.concatenate([zr[:, P:], zi[:, P:]], axis=1))
    lhs = jnp.concatenate(rows, axis=0).astype(BF16)
    return jnp.dot(lhs, g2, preferred_element_type=F32)


def _inv_fft(spec, g2c, twr, twi, g4, want_imag):
    Q = twr.shape[0]
    P = FFT_P
    G = spec.shape[0] // Q
    S = g4.shape[0] // 2 if want_imag else g4.shape[0]
    y = jnp.dot(spec.astype(BF16), g2c, preferred_element_type=F32)
    y = y.astype(twr.dtype)
    out_re, out_im = [], []
    for c in range(0, G, 2):
        ya = y[c * Q:(c + 1) * Q]
        yb = y[(c + 1) * Q:(c + 2) * Q]
        yr = jnp.concatenate([ya[:, :P], yb[:, :P]], axis=1)
        yi = jnp.concatenate([ya[:, P:], yb[:, P:]], axis=1)
        zr = yr * twr + yi * twi
        zi = yi * twr - yr * twi
        rhs = jnp.concatenate([zr, zi], axis=0).astype(BF16)
        o = jnp.dot(g4, rhs, preferred_element_type=F32)
        out_re += [o[:S, :P], o[:S, P:]]
        if want_imag:
            out_im += [o[S:, :P], o[S:, P:]]
    return out_re, out_im


def _hyena_body(v_ref, x1_ref, x2_ref, af_ref, ab_ref, wof_ref, wob_ref, delta_ref, tf_ref, tb_ref, skip_ref,
                g1_ref, g1f_ref, twr_ref, twi_ref, g2_ref, g2c_ref, g4_ref, o_ref, taps_ref, kf_ref,
                *, S, Q, paired):
    g1, twr, twi = g1_ref[...], twr_ref[...], twi_ref[...]
    g2, g2c, g4 = g2_ref[...], g2c_ref[...], g4_ref[...]
    nb = 2 if paired else 1
    cb = kf_ref.shape[1]
    G = min(cb, CH_GROUP)
    n_groups = cb // G
    P = FFT_P

    @pl.when(pl.program_id(1) == 0)
    def _():
        g1f = g1f_ref[...]
        L = S * P
        delta = delta_ref[...]
        halves = ((wof_ref, af_ref, jnp.exp(-(delta * tf_ref[...]))),
                  (wob_ref, ab_ref, jnp.where(lax.broadcasted_iota(jnp.int32, (cb, L), 1) == 0, 0.0,
                                              jnp.exp(-(delta * tb_ref[...])))))
        for half, (wo_ref, a_ref, decay) in enumerate(halves):
            wo = wo_ref[...].reshape(HY_ORDER * cb, HY_FILTER_W)
            h = jnp.dot(wo, a_ref[...], preferred_element_type=F32)
            for o in range(HY_ORDER):
                taps_ref[o, :, half * L:(half + 1) * L] = h[o * cb:(o + 1) * cb] * decay

        def filter_group(gi, carry):
            c0 = pl.multiple_of(gi * G, G)
            for o in range(HY_ORDER):
                k = taps_ref[o, pl.ds(c0, G), :]
                norm = jnp.sum(jnp.abs(k), axis=-1, keepdims=True)
                tiles = _to_tiles(k, Q)
                spec = _fwd_fft([tiles[c] for c in range(G)], None, g1f, twr, twi, g2)
                kf = spec.reshape(G, Q, 2 * P) * (1.0 / norm)[:, :, None]
                kf_ref[o, pl.ds(c0, G)] = kf.astype(kf_ref.dtype)
            return carry

        lax.fori_loop(0, n_groups, filter_group, 0)

    def group(gi, carry):
        c0 = pl.multiple_of(gi * G, G)
        r0 = pl.multiple_of(gi * (G * V7X_SUBLANES), G * V7X_SUBLANES)

        def tiles(ref, b):
            blk = ref[b, :, pl.ds(r0, G * V7X_SUBLANES), :]
            return [blk[:, c * V7X_SUBLANES:(c + 1) * V7X_SUBLANES, :].reshape(S, P) for c in range(G)]

        z = [tiles(v_ref, b) for b in range(nb)]
        gates = [[tiles(x1_ref, b) for b in range(nb)], [tiles(x2_ref, b) for b in range(nb)]]
        for o in range(HY_ORDER):
            re = z[0]
            im = z[1] if paired else None
            spec = _fwd_fft(re, im, g1, twr, twi, g2)
            kf = kf_ref[o, pl.ds(c0, G)].reshape(G * Q, 2 * P)
            spec = spec.astype(kf.dtype)
            xr, xi = spec[:, :P], spec[:, P:]
            kr, ki = kf[:, :P], kf[:, P:]
            prod = jnp.concatenate([xr * kr - xi * ki, xr * ki + xi * kr], axis=1)
            out_re, out_im = _inv_fft(prod, g2c, twr, twi, g4, paired)
            skip = skip_ref[o, pl.ds(c0, G)]
            conv = [out_re] + ([out_im] if paired else [])
            z = [[gates[o][b][c] * (conv[b][c] + skip[c] * z[b][c]) for c in range(G)] for b in range(nb)]
        for b in range(nb):
            o_ref[b, pl.ds(c0, G), :] = _from_tiles(jnp.stack(z[b], axis=0))
        return carry

    lax.fori_loop(0, n_groups, group, 0)


def hyena_operator(u_tiles, a, a_b, f_wout, skip, tabs, *, cb):
    B, nt, rows, P = u_tiles.shape
    D = rows // (3 * V7X_SUBLANES)
    L = nt * IN_BT
    S = L // P
    Q = tabs["Q"]
    W = HY_FILTER_W
    paired = B % 2 == 0
    nb = 2 if paired else 1
    wo = f_wout.T.reshape(HY_ORDER, 2, D, W).astype(BF16)
    wo_f, wo_b = wo[:, 0], wo[:, 1]
    deltas = np.abs(np.linspace(math.log(HY_TARGET) / HY_SLOW_PCT,
                                math.log(HY_TARGET) / HY_FAST_PCT, D)).astype(np.float32).reshape(D, 1)
    t_f = jnp.linspace(0.0, 1.0, L, dtype=F32).reshape(1, L)
    t_b = ((L - jnp.arange(L, dtype=F32)) / (L - 1)).reshape(1, L)
    skip_rows = jnp.broadcast_to(skip.astype(F32)[:, :, None, None], (HY_ORDER, D, 1, P))

    bf = lambda name: jnp.asarray(tabs[name]).astype(BF16)
    g1, g1f, g2, g2c, g4 = bf("g1"), bf("g1_full"), bf("g2"), bf("g2c"), bf("g4")
    twr, twi = bf("twr"), bf("twi")
    const = lambda arr: pl.BlockSpec(arr.shape, lambda c, p: (0,) * arr.ndim, pipeline_mode=pl.Buffered(1))
    ncb = D // cb
    return pl.pallas_call(
        functools.partial(_hyena_body, S=S, Q=Q, paired=paired),
        grid=(ncb, B // nb),
        in_specs=[
            pl.BlockSpec((nb, nt, cb * V7X_SUBLANES, P), lambda c, p: (p, 0, c, 0)),
            pl.BlockSpec((nb, nt, cb * V7X_SUBLANES, P), lambda c, p: (p, 0, c + ncb, 0)),
            pl.BlockSpec((nb, nt, cb * V7X_SUBLANES, P), lambda c, p: (p, 0, c + 2 * ncb, 0)),
            const(a), const(a_b),
            pl.BlockSpec((HY_ORDER, cb, W), lambda c, p: (0, c, 0)),
            pl.BlockSpec((HY_ORDER, cb, W), lambda c, p: (0, c, 0)),
            pl.BlockSpec((cb, 1), lambda c, p: (c, 0)),
            const(t_f), const(t_b),
            pl.BlockSpec((HY_ORDER, cb, 1, P), lambda c, p: (0, c, 0, 0)),
            const(g1), const(g1f), const(twr), const(twi), const(g2), const(g2c), const(g4),
        ],
        out_specs=pl.BlockSpec((nb, cb, L), lambda c, p: (p, c, 0)),
        out_shape=jax.ShapeDtypeStruct((B, D, L), F32),
        scratch_shapes=[pltpu.VMEM((HY_ORDER, cb, 2 * L), F32), pltpu.VMEM((HY_ORDER, cb, Q, 2 * P), BF16)],
        compiler_params=_cparams("parallel", "arbitrary"),
        name="hyena_operator",
    )(u_tiles, u_tiles, u_tiles, a, a_b, wo_f, wo_b, jnp.asarray(deltas), t_f, t_b, skip_rows,
      g1, g1f, twr, twi, g2, g2c, g4)


def _trunk(x, p, cfg):
    B, L, D = x.shape
    tabs = _dft_tables(L, paired=(B % 2 == 0))
    for i in range(DEPTH):
        j = i // N_MIXERS
        if i % N_MIXERS == 0:
            a, a_b = hyena_filter_mlp(L, p["hy_f_w1"][j], p["hy_f_b1"][j], p["hy_f_w2"][j], p["hy_f_b2"][j],
                                      p["hy_f_w3"][j], p["hy_f_b3"][j], p["hy_f_freq"][j])
            u_tiles = hyena_in_projection(x, p["norm_mix_g"][i], p["hy_w_in_t"][j], p["hy_b_in"][j],
                                          p["hy_conv_w"][j], p["hy_conv_b"][j], bc=1024)
            z_t = hyena_operator(u_tiles, a, a_b, p["hy_f_wout"][j], p["hy_skip"][j], tabs, cb=cfg["hy_cb"])
            x = residual_matmul_t(x, z_t, p["hy_w_out"][j], p["hy_b_out"][j], bt=512)
        else:
            x2 = x.reshape(B * L, D)
            qkv = qkv_projection(x2, p["norm_mix_g"][i], p["at_w_qkv"][j], p["at_q_g"][j], p["at_k_g"][j], bm=1024)
            att = window_attention(qkv.reshape(B, L, -1), p["at_sink"][j], p["rel_bias"])
            x = residual_matmul(x2, att.reshape(B * L, -1), p["at_w_o"][j], bm=1024).reshape(B, L, D)
        x = ffn_block(x.reshape(B * L, D), p["norm_ffn_g"][i], p["ffn_w_gate_up"][i], p["ffn_w_down"][i],
                      bm=1024, bf=512).reshape(B, L, D)
    return x


def kernel(x_prompt, x_sample, norm_mix_g, norm_ffn_g, hy_w_in, hy_b_in, hy_conv_w, hy_conv_b, hy_f_w1, hy_f_b1,
           hy_f_w2, hy_f_b2, hy_f_w3, hy_f_b3, hy_f_wout, hy_f_freq, hy_skip, hy_w_out, hy_b_out, at_w_qkv, at_q_g,
           at_k_g, at_sink, at_w_o, rel_bias, ffn_w_gate_up, ffn_w_down):
    per_layer = lambda w: [w[i].astype(BF16) for i in range(w.shape[0])]
    p = dict(
        norm_mix_g=norm_mix_g.astype(F32), norm_ffn_g=norm_ffn_g.astype(F32),
        hy_w_in_t=[hy_w_in[j].T.astype(BF16) for j in range(hy_w_in.shape[0])], hy_b_in=hy_b_in,
        hy_conv_w=hy_conv_w, hy_conv_b=hy_conv_b,
        hy_f_w1=hy_f_w1, hy_f_b1=hy_f_b1, hy_f_w2=hy_f_w2, hy_f_b2=hy_f_b2, hy_f_w3=hy_f_w3, hy_f_b3=hy_f_b3,
        hy_f_wout=hy_f_wout, hy_f_freq=hy_f_freq, hy_skip=hy_skip,
        hy_w_out=per_layer(hy_w_out), hy_b_out=hy_b_out,
        at_w_qkv=per_layer(at_w_qkv), at_q_g=at_q_g, at_k_g=at_k_g, at_sink=at_sink,
        at_w_o=per_layer(at_w_o), rel_bias=rel_bias,
        ffn_w_gate_up=per_layer(ffn_w_gate_up), ffn_w_down=per_layer(ffn_w_down),
    )
    y_prompt = _trunk(x_prompt, p, dict(hy_cb=32))
    y_sample = _trunk(x_sample, p, dict(hy_cb=32))
    return (y_prompt, y_sample)
```

```python
import functools
import math

import jax
import jax.numpy as jnp
import numpy as np
from jax import lax
from jax.experimental import pallas as pl
from jax.experimental.pallas import tpu as pltpu

F32 = jnp.float32
BF16 = jnp.bfloat16

D_MODEL = 2048
DEPTH = 4
N_MIXERS = 2
HY_ORDER = 2
HY_EMB = 33
HY_BANDS = (HY_EMB - 1) // 2
HY_FILTER_W = 64
HY_FAST_PCT = 0.3
HY_SLOW_PCT = 1.5
HY_TARGET = 1e-2
N_HEADS = 16
HEAD_DIM = 128
N_KV = 4
GQA_G = N_HEADS // N_KV
WINDOW = 128
BLOCK = 128
N_BUCKETS = 32
MAX_DIST = 128
D_FF = -(-(8 * D_MODEL) // (3 * 256)) * 256
EPS = 1e-6
NEG = -1e30

V7X_LANES = 128
V7X_SUBLANES = 8
VMEM_LIMIT = 56 * 1024 * 1024

FFT_P = V7X_LANES
CH_GROUP = 4 * V7X_SUBLANES


def _cparams(*sem):
    return pltpu.CompilerParams(dimension_semantics=sem, vmem_limit_bytes=VMEM_LIMIT)


def _rms_bf16(x, g):
    ms = jnp.mean(x * x, axis=-1, keepdims=True)
    return (x * lax.rsqrt(ms + EPS) * g).astype(BF16)


def _qkv_proj_body(x_ref, g_ref, w_ref, qg_ref, kg_ref, o_ref):
    hn = _rms_bf16(x_ref[...], g_ref[...])
    nq, nk = N_HEADS * HEAD_DIM, N_KV * HEAD_DIM
    pair = 2 * HEAD_DIM
    for c0 in range(0, nq + nk, pair):
        acc = jnp.dot(hn, w_ref[:, c0:c0 + pair], preferred_element_type=F32)
        gain = qg_ref[...] if c0 < nq else kg_ref[...]
        for c in (0, HEAD_DIM):
            t = acc[:, c:c + HEAD_DIM]
            ms = jnp.mean(t * t, axis=-1, keepdims=True)
            o_ref[:, c0 + c:c0 + c + HEAD_DIM] = (t * lax.rsqrt(ms + EPS) * gain).astype(o_ref.dtype)
    o_ref[:, nq + nk:] = jnp.dot(hn, w_ref[:, nq + nk:], preferred_element_type=F32).astype(o_ref.dtype)


def qkv_projection(x, g, w, q_g, k_g, *, bm):
    T, K = x.shape
    N = w.shape[1]
    return pl.pallas_call(
        _qkv_proj_body,
        grid=(T // bm,),
        in_specs=[
            pl.BlockSpec((bm, K), lambda i: (i, 0)),
            pl.BlockSpec((1, K), lambda i: (0, 0)),
            pl.BlockSpec((K, N), lambda i: (0, 0), pipeline_mode=pl.Buffered(1)),
            pl.BlockSpec((1, HEAD_DIM), lambda i: (0, 0)),
            pl.BlockSpec((1, HEAD_DIM), lambda i: (0, 0)),
        ],
        out_specs=pl.BlockSpec((bm, N), lambda i: (i, 0)),
        out_shape=jax.ShapeDtypeStruct((T, N), BF16),
        compiler_params=_cparams("parallel"),
        name="qkv_projection",
    )(x, g.reshape(1, K), w, q_g.astype(F32).reshape(1, HEAD_DIM), k_g.astype(F32).reshape(1, HEAD_DIM))


IN_BT = V7X_SUBLANES * V7X_LANES
IN_ROW_SPLIT = 4


def _hyena_edge_body(x_ref, g_ref, w_ref, b_ref, o_ref):
    hn = _rms_bf16(x_ref[0], g_ref[...])
    o_ref[0] = lax.dot_general(hn, w_ref[...], (((1,), (1,)), ((), ())), preferred_element_type=F32) + b_ref[...]


def hyena_edge_projection(x_edge, g, w_t, bias, *, bc):
    B, n, K = x_edge.shape
    C = w_t.shape[0]
    return pl.pallas_call(
        _hyena_edge_body,
        grid=(B, C // bc),
        in_specs=[
            pl.BlockSpec((1, n, K), lambda b, c: (b, 0, 0)),
            pl.BlockSpec((1, K), lambda b, c: (0, 0)),
            pl.BlockSpec((bc, K), lambda b, c: (c, 0)),
            pl.BlockSpec((1, bc), lambda b, c: (0, c)),
        ],
        out_specs=pl.BlockSpec((1, n, bc), lambda b, c: (b, 0, c)),
        out_shape=jax.ShapeDtypeStruct((B, n, C), F32),
        compiler_params=_cparams("parallel", "parallel"),
        name="hyena_edge_projection",
    )(x_edge, g.reshape(1, K), w_t, bias.astype(F32).reshape(1, C))


def _hyena_in_body(x_ref, g_ref, w_ref, b_ref, k0_ref, k1_ref, k2_ref, kb_ref, edge_ref, o_ref, hn_ref):
    t = pl.program_id(1)
    nt = pl.num_programs(1)

    @pl.when(pl.program_id(2) == 0)
    def _():
        xx = x_ref[0]
        ms = jnp.mean(xx * xx, axis=-1, keepdims=True)
        hn_ref[...] = jnp.transpose(xx * lax.rsqrt(ms + EPS) * g_ref[...]).astype(BF16)

    P = V7X_LANES
    bt = hn_ref.shape[1]
    n_chunks = bt // P
    rows = w_ref.shape[0] // IN_ROW_SPLIT
    for r in range(IN_ROW_SPLIT):
        rs = slice(r * rows, (r + 1) * rows)
        u = jnp.dot(w_ref[rs, :], hn_ref[...], preferred_element_type=F32)
        def border(k):
            parts = [jnp.transpose(jnp.broadcast_to(edge_ref[0, k, 0:1, c0:c0 + P], (P, P)))
                     for c0 in range(r * rows, (r + 1) * rows, P)]
            return jnp.concatenate(parts, axis=0)

        left = jnp.where(t == 0, 0.0, border(t))
        right = jnp.where(t == nt - 1, 0.0, border(nt + t))
        lane = lax.broadcasted_iota(jnp.int32, (rows, P), 1)
        bias, k0, k1, k2, kb = (ref[rs, :] for ref in (b_ref, k0_ref, k1_ref, k2_ref, kb_ref))
        chunks = [u[:, j * P:(j + 1) * P] + bias for j in range(n_chunks)]
        fwd = [pltpu.roll(c, 1, axis=1) for c in chunks]
        bwd = [pltpu.roll(c, P - 1, axis=1) for c in chunks]
        for j in range(n_chunks):
            prev = jnp.where(lane == 0, fwd[j - 1] if j > 0 else left, fwd[j])
            nxt = jnp.where(lane == P - 1, bwd[j + 1] if j + 1 < n_chunks else right, bwd[j])
            o_ref[pl.ds(r * rows * n_chunks + j, rows, stride=n_chunks), :] = (
                k0 * prev + k1 * chunks[j] + k2 * nxt + kb)


def hyena_in_projection(x, g, w_t, bias, conv_w, conv_b, *, bc):
    B, L, K = x.shape
    C = w_t.shape[0]
    bt = IN_BT
    nt = L // bt
    xb = x.reshape(B, nt, bt, K)
    x_edge = jnp.concatenate([jnp.roll(xb[:, :, bt - 1], 1, axis=1), jnp.roll(xb[:, :, 0], -1, axis=1)], axis=1)
    edge = hyena_edge_projection(x_edge, g, w_t, bias, bc=bc)
    edge = jnp.broadcast_to(edge[:, :, None, :], (B, 2 * nt, V7X_SUBLANES, C))
    col = lambda v: jnp.broadcast_to(v.astype(F32).reshape(C, 1), (C, V7X_LANES))
    cspec = pl.BlockSpec((bc, V7X_LANES), lambda b, t, c: (c, 0))
    return pl.pallas_call(
        _hyena_in_body,
        grid=(B, nt, C // bc),
        in_specs=[
            pl.BlockSpec((1, bt, K), lambda b, t, c: (b, t, 0)),
            pl.BlockSpec((1, K), lambda b, t, c: (0, 0)),
            pl.BlockSpec((bc, K), lambda b, t, c: (c, 0)),
            cspec, cspec, cspec, cspec, cspec,
            pl.BlockSpec((1, 2 * nt, V7X_SUBLANES, bc), lambda b, t, c: (b, 0, 0, c)),
        ],
        out_specs=pl.BlockSpec((None, None, bc * V7X_SUBLANES, V7X_LANES), lambda b, t, c: (b, t, c, 0)),
        out_shape=jax.ShapeDtypeStruct((B, nt, C * V7X_SUBLANES, V7X_LANES), F32),
        scratch_shapes=[pltpu.VMEM((K, bt), BF16)],
        compiler_params=_cparams("parallel", "parallel", "arbitrary"),
        name="hyena_in_projection",
    )(x, g.reshape(1, K), w_t, col(bias), col(conv_w[0]), col(conv_w[1]), col(conv_w[2]), col(conv_b), edge)


def _res_mm_body(a_ref, w_ref, x_ref, o_ref):
    o_ref[...] = x_ref[...] + jnp.dot(a_ref[...], w_ref[...], preferred_element_type=F32)


def residual_matmul(x, a, w, *, bm):
    T, K = a.shape
    N = w.shape[1]
    return pl.pallas_call(
        _res_mm_body,
        grid=(T // bm,),
        in_specs=[
            pl.BlockSpec((bm, K), lambda i: (i, 0)),
            pl.BlockSpec((K, N), lambda i: (0, 0), pipeline_mode=pl.Buffered(1)),
            pl.BlockSpec((bm, N), lambda i: (i, 0)),
        ],
        out_specs=pl.BlockSpec((bm, N), lambda i: (i, 0)),
        out_shape=jax.ShapeDtypeStruct((T, N), F32),
        compiler_params=_cparams("parallel"),
        name="residual_matmul",
    )(a, w, x)


def _res_mm_t_body(z_ref, w_ref, b_ref, x_ref, o_ref):
    y = lax.dot_general(z_ref[0], w_ref[...], (((0,), (0,)), ((), ())), preferred_element_type=F32)
    o_ref[0] = x_ref[0] + y + b_ref[...]


def residual_matmul_t(x, z_t, w, bias, *, bt):
    B, K, L = z_t.shape
    N = w.shape[1]
    return pl.pallas_call(
        _res_mm_t_body,
        grid=(B, L // bt),
        in_specs=[
            pl.BlockSpec((1, K, bt), lambda b, t: (b, 0, t)),
            pl.BlockSpec((K, N), lambda b, t: (0, 0), pipeline_mode=pl.Buffered(1)),
            pl.BlockSpec((1, N), lambda b, t: (0, 0)),
            pl.BlockSpec((1, bt, N), lambda b, t: (b, t, 0)),
        ],
        out_specs=pl.BlockSpec((1, bt, N), lambda b, t: (b, t, 0)),
        out_shape=jax.ShapeDtypeStruct((B, L, N), F32),
        compiler_params=_cparams("parallel", "parallel"),
        name="residual_matmul_t",
    )(z_t, w, bias.reshape(1, N), x)


def _ffn_body(x_ref, g_ref, wg_ref, wu_ref, wd_ref, o_ref, hn_ref):
    @pl.when(pl.program_id(1) == 0)
    def _():
        x = x_ref[...]
        hn_ref[...] = _rms_bf16(x, g_ref[...])
        o_ref[...] = x

    h = hn_ref[...]
    gate = jnp.dot(h, wg_ref[...], preferred_element_type=F32)
    up = jnp.dot(h, wu_ref[...], preferred_element_type=F32)
    act = (gate * jax.nn.sigmoid(gate) * up).astype(BF16)
    o_ref[...] += jnp.dot(act, wd_ref[...], preferred_element_type=F32)


def ffn_block(x, g, w_gate_up, w_down, *, bm, bf):
    T, K = x.shape
    nf = D_FF // bf
    return pl.pallas_call(
        _ffn_body,
        grid=(T // bm, nf),
        in_specs=[
            pl.BlockSpec((bm, K), lambda i, f: (i, 0)),
            pl.BlockSpec((1, K), lambda i, f: (0, 0)),
            pl.BlockSpec((K, bf), lambda i, f: (0, f)),
            pl.BlockSpec((K, bf), lambda i, f: (0, f + nf)),
            pl.BlockSpec((bf, K), lambda i, f: (f, 0)),
        ],
        out_specs=pl.BlockSpec((bm, K), lambda i, f: (i, 0)),
        out_shape=jax.ShapeDtypeStruct((T, K), F32),
        scratch_shapes=[pltpu.VMEM((bm, K), BF16)],
        compiler_params=_cparams("parallel", "arbitrary"),
        name="ffn_block",
    )(x, g.reshape(1, K), w_gate_up, w_gate_up, w_down)


ATT_TQ = 4 * BLOCK
KV_W = N_KV * HEAD_DIM


def _band_structure():
    qi = np.arange(BLOCK)[:, None]
    ki = np.arange(3 * BLOCK)[None, :]
    rel = ki - BLOCK - qi
    nb = N_BUCKETS // 2
    max_exact = nb // 2
    n = np.abs(rel)
    large = max_exact + (np.log(np.maximum(n, 1) / max_exact) / math.log(MAX_DIST / max_exact)
                         * (nb - max_exact)).astype(np.int32)
    large = np.minimum(large, nb - 1)
    buckets = (rel > 0).astype(np.int32) * nb + np.where(n < max_exact, n, large).astype(np.int32)
    band = n <= WINDOW
    return buckets, band


def _attn_body(q_ref, kp_ref, kc_ref, kn_ref, vp_ref, vc_ref, vn_ref, bias_ref, sink_ref,
               o_ref, kbuf, vbuf):
    i = pl.program_id(1)
    last = pl.num_programs(1) - 1

    kbuf[0:BLOCK] = kp_ref[0]
    kbuf[BLOCK:BLOCK + ATT_TQ] = kc_ref[0]
    kbuf[BLOCK + ATT_TQ:] = kn_ref[0]
    ones = jnp.ones((ATT_TQ + 2 * BLOCK, HEAD_DIM), BF16)
    for g in range(N_KV):
        src = slice(g * HEAD_DIM, (g + 1) * HEAD_DIM)
        dst = slice(2 * g * HEAD_DIM, (2 * g + 1) * HEAD_DIM)
        vbuf[0:BLOCK, dst] = vp_ref[0, :, src]
        vbuf[BLOCK:BLOCK + ATT_TQ, dst] = vc_ref[0, :, src]
        vbuf[BLOCK + ATT_TQ:, dst] = vn_ref[0, :, src]
        vbuf[:, (2 * g + 1) * HEAD_DIM:(2 * g + 2) * HEAD_DIM] = ones

    lane = lax.broadcasted_iota(jnp.int32, (1, 3 * BLOCK), 1)
    first_edge = jnp.where((lane < BLOCK) & (i == 0), NEG, 0.0).astype(F32)
    last_edge = jnp.where((lane >= 2 * BLOCK) & (i == last), NEG, 0.0).astype(F32)
    exp2_scale = HEAD_DIM ** -0.5 * math.log2(math.e)

    n_sub = ATT_TQ // BLOCK

    def scores(j, g):
        r0 = j * BLOCK
        qs = jnp.concatenate(
            [q_ref[0, r0:r0 + BLOCK, (GQA_G * g + h) * HEAD_DIM:(GQA_G * g + h + 1) * HEAD_DIM]
             for h in range(GQA_G)], axis=0)
        kw = kbuf[r0:r0 + 3 * BLOCK, g * HEAD_DIM:(g + 1) * HEAD_DIM]
        s = lax.dot_general(qs, kw, (((1,), (1,)), ((), ())), preferred_element_type=F32)
        s = s + bias_ref[g]
        if j == 0:
            s = s + first_edge
        if j == n_sub - 1:
            s = s + last_edge
        return s

    def softmax_numerators(g, s):
        sk = sink_ref[g][:, 0:1]
        m = jnp.maximum(jnp.max(s, axis=-1, keepdims=True), sk)
        return jnp.exp2((s - m) * exp2_scale).astype(BF16), jnp.exp2((sk - m) * exp2_scale)

    def outputs(j, g, p, sink_p):
        r0 = j * BLOCK
        vw = vbuf[r0:r0 + 3 * BLOCK, 2 * g * HEAD_DIM:(2 * g + 2) * HEAD_DIM]
        pv = jnp.dot(p, vw, preferred_element_type=F32)
        o = pv[:, :HEAD_DIM] / (pv[:, HEAD_DIM:] + sink_p)
        for h in range(GQA_G):
            c0 = (GQA_G * g + h) * HEAD_DIM
            o_ref[0, r0:r0 + BLOCK, c0:c0 + HEAD_DIM] = o[h * BLOCK:(h + 1) * BLOCK].astype(o_ref.dtype)

    chains = [(j, g) for j in range(n_sub) for g in range(N_KV)]
    s_q, p_q = {}, {}
    for step in range(len(chains) + 2):
        if step >= 2:
            outputs(*chains[step - 2], *p_q.pop(step - 2))
        if 1 <= step <= len(chains):
            p_q[step - 1] = softmax_numerators(chains[step - 1][1], s_q.pop(step - 1))
        if step < len(chains):
            s_q[step] = scores(*chains[step])


def window_attention(qkv, sink, rel_bias):
    B, L, _ = qkv.shape
    nq = N_HEADS * HEAD_DIM
    sub = ATT_TQ // BLOCK
    nblk = L // BLOCK
    kcol = nq // KV_W
    vcol = kcol + 1

    buckets, band = _band_structure()
    onehot = (jnp.asarray(buckets)[:, :, None] == jnp.arange(N_BUCKETS)).astype(F32)
    bias = jnp.einsum("qkn,nh->qkh", onehot, rel_bias.astype(F32), precision=lax.Precision.HIGHEST)
    inv_scale = HEAD_DIM ** 0.5
    bias = jnp.where(band[:, :, None], bias * inv_scale, NEG)
    bias = jnp.transpose(bias, (2, 0, 1)).reshape(N_KV, GQA_G * BLOCK, 3 * BLOCK)
    sink_rows = jnp.broadcast_to((sink.astype(F32) * inv_scale).reshape(N_KV, GQA_G, 1, 1),
                                 (N_KV, GQA_G, BLOCK, V7X_LANES)).reshape(N_KV, GQA_G * BLOCK, V7X_LANES)

    return pl.pallas_call(
        _attn_body,
        grid=(B, L // ATT_TQ),
        in_specs=[
            pl.BlockSpec((1, ATT_TQ, nq), lambda b, i: (b, i, 0)),
            pl.BlockSpec((1, BLOCK, KV_W), lambda b, i: (b, jnp.maximum(sub * i - 1, 0), kcol)),
            pl.BlockSpec((1, ATT_TQ, KV_W), lambda b, i: (b, i, kcol)),
            pl.BlockSpec((1, BLOCK, KV_W), lambda b, i: (b, jnp.minimum(sub * i + sub, nblk - 1), kcol)),
            pl.BlockSpec((1, BLOCK, KV_W), lambda b, i: (b, jnp.maximum(sub * i - 1, 0), vcol)),
            pl.BlockSpec((1, ATT_TQ, KV_W), lambda b, i: (b, i, vcol)),
            pl.BlockSpec((1, BLOCK, KV_W), lambda b, i: (b, jnp.minimum(sub * i + sub, nblk - 1), vcol)),
            pl.BlockSpec((N_KV, GQA_G * BLOCK, 3 * BLOCK), lambda b, i: (0, 0, 0)),
            pl.BlockSpec((N_KV, GQA_G * BLOCK, V7X_LANES), lambda b, i: (0, 0, 0)),
        ],
        out_specs=pl.BlockSpec((1, ATT_TQ, nq), lambda b, i: (b, i, 0)),
        out_shape=jax.ShapeDtypeStruct((B, L, nq), BF16),
        scratch_shapes=[pltpu.VMEM((ATT_TQ + 2 * BLOCK, KV_W), BF16),
                        pltpu.VMEM((ATT_TQ + 2 * BLOCK, 2 * KV_W), BF16)],
        compiler_params=_cparams("parallel", "parallel"),
        name="window_attention",
    )(qkv, qkv, qkv, qkv, qkv, qkv, qkv, bias, sink_rows)


FEAT_PAD = V7X_LANES


def _filter_mlp_body(ff_ref, fb_ref, w1_ref, b1_ref, w2_ref, b2_ref, w3_ref, b3_ref, fr_ref, af_ref, ab_ref):
    fr = fr_ref[...]
    for feat_ref, a_ref in ((ff_ref, af_ref), (fb_ref, ab_ref)):
        a = jnp.sin(fr * (jnp.dot(w1_ref[...], feat_ref[...], preferred_element_type=F32) + b1_ref[...]))
        a = jnp.sin(fr * (jnp.dot(w2_ref[...], a.astype(BF16), preferred_element_type=F32) + b2_ref[...]))
        a = jnp.sin(fr * (jnp.dot(w3_ref[...], a.astype(BF16), preferred_element_type=F32) + b3_ref[...]))
        a_ref[...] = a.astype(BF16)


def hyena_filter_mlp(L, f_w1, f_b1, f_w2, f_b2, f_w3, f_b3, f_freq, *, bt=2048):
    W = HY_FILTER_W
    f = jnp.linspace(1e-4, HY_BANDS - 1, HY_BANDS, dtype=F32)

    def features(pos):
        t = pos / (L - 1)
        ang = (2.0 * math.pi * pos / L)[None, :] * f[:, None]
        rows = jnp.concatenate([t[None, :], jnp.cos(ang), -jnp.sin(ang), jnp.zeros((FEAT_PAD - HY_EMB, L), F32)])
        return rows.astype(BF16)

    pos = jnp.arange(L, dtype=F32)
    w1_t = jnp.concatenate([f_w1.astype(F32).T, jnp.zeros((W, FEAT_PAD - HY_EMB), F32)], axis=1)
    col = lambda v: v.astype(F32).reshape(W, 1)
    const = lambda r, c: pl.BlockSpec((r, c), lambda i: (0, 0))
    fspec = pl.BlockSpec((FEAT_PAD, bt), lambda i: (0, i))
    ospec = pl.BlockSpec((W, bt), lambda i: (0, i))
    return pl.pallas_call(
        _filter_mlp_body,
        grid=(L // bt,),
        in_specs=[fspec, fspec, const(W, FEAT_PAD), const(W, 1), const(W, W), const(W, 1), const(W, W), const(W, 1),
                  const(W, 1)],
        out_specs=[ospec, ospec],
        out_shape=[jax.ShapeDtypeStruct((W, L), BF16), jax.ShapeDtypeStruct((W, L), BF16)],
        compiler_params=_cparams("parallel"),
        name="hyena_filter_mlp",
    )(features(pos), features(L - pos), w1_t.astype(BF16), col(f_b1), f_w2.T.astype(BF16), col(f_b2),
      f_w3.T.astype(BF16), col(f_b3), col(f_freq))


def _dft_tables(L, paired):
    N = 2 * L
    P = FFT_P
    Q = N // P
    S = L // P
    b = np.arange(Q, dtype=np.float64)
    fq = np.exp(-2j * np.pi * np.outer(b, np.arange(S)) / Q)
    fp =np.exp(-2j * np.pi * np.outer(np.arange(P), np.arange(P)) / P)
    tw = np.exp(-2j * np.pi * np.outer(b, np.arange(P)) / N)
    ci = np.conj(fq).T / N

    def stack(c):
        return np.block([[c.real, -c.imag], [c.imag, c.real]])

    if paired:
        g1 = stack(fq)
        g4 = stack(ci)
    else:
        g1 = np.concatenate([fq.real, fq.imag], axis=0)
        g4 = np.concatenate([ci.real, -ci.imag], axis=1)
    fq_full = np.exp(-2j * np.pi * np.outer(b, np.arange(Q)) / Q)
    g1_full = np.concatenate([fq_full.real, fq_full.imag], axis=0)
    g2 =np.block([[fp.real, fp.imag], [-fp.imag, fp.real]])
    g2c = np.block([[fp.real, -fp.imag], [fp.imag, fp.real]])
    twr = np.tile(tw.real, (1, 2))
    twi = np.tile(tw.imag, (1, 2))
    f32 = lambda a: np.ascontiguousarray(a, dtype=np.float32)
    return dict(g1=f32(g1), g4=f32(g4), g1_full=f32(g1_full), g2=f32(g2), g2c=f32(g2c),
                twr=f32(twr), twi=f32(twi), Q=Q, S=S, N=N)


def _to_tiles(x, n_tiles):
    chunks = [x[:, s * FFT_P:(s + 1) * FFT_P] for s in range(n_tiles)]
    return jnp.swapaxes(jnp.stack(chunks, axis=0), 0, 1)


def _from_tiles(x):
    y = jnp.swapaxes(x, 0, 1)
    return jnp.concatenate([y[s] for s in range(y.shape[0])], axis=1)


def _fwd_fft(re_tiles, im_tiles, g1, twr, twi, g2):
    G = len(re_tiles)
    Q = twr.shape[0]
    P = FFT_P
    rows = []
    for c in range(0, G, 2):
        top = jnp.concatenate([re_tiles[c], re_tiles[c + 1]], axis=1)
        if im_tiles is None:
            rhs = top
        else:
            rhs = jnp.concatenate([top, jnp.concatenate([im_tiles[c], im_tiles[c + 1]], axis=1)], axis=0)
        y = jnp.dot(g1, rhs.astype(BF16), preferred_element_type=F32)
        y = y.astype(twr.dtype)
        yr, yi = y[:Q], y[Q:]
        zr = yr * twr - yi * twi
        zi = yr * twi + yi * twr
        rows.append(jnp.concatenate([zr[:, :P], zi[:, :P]], axis=1))
        rows.append(jnp.concatenate([zr[:, P:], zi[:, P:]], axis=1))
    lhs = jnp.concatenate(rows, axis=0).astype(BF16)
    return jnp.dot(lhs, g2, preferred_element_type=F32)


def _inv_fft(spec, g2c, twr, twi, g4, want_imag):
    Q = twr.shape[0]
    P = FFT_P
    G = spec.shape[0] // Q
    S = g4.shape[0] // 2 if want_imag else g4.shape[0]
    y = jnp.dot(spec.astype(BF16), g2c, preferred_element_type=F32)
    y = y.astype(twr.dtype)
    out_re, out_im = [], []
    for c in range(0, G, 2):
        ya = y[c * Q:(c + 1) * Q]
        yb = y[(c + 1) * Q:(c + 2) * Q]
        yr = jnp.concatenate([ya[:, :P], yb[:, :P]], axis=1)
        yi = jnp.concatenate([ya[:, P:], yb[:, P:]], axis=1)
        zr = yr * twr + yi * twi
        zi = yi * twr - yr * twi
        rhs = jnp.concatenate([zr, zi], axis=0).astype(BF16)
        o = jnp.dot(g4, rhs, preferred_element_type=F32)
        out_re += [o[:S, :P], o[:S, P:]]
        if want_imag:
            out_im += [o[S:, :P], o[S:, P:]]
    return out_re, out_im


def _hyena_body(v_ref, x1_ref, x2_ref, af_ref, ab_ref, wof_ref, wob_ref, delta_ref, tf_ref, tb_ref, skip_ref,
                g1_ref, g1f_ref, twr_ref, twi_ref, g2_ref, g2c_ref, g4_ref, o_ref, taps_ref, kf_ref,
                *, S, Q, paired):
    g1, twr, twi = g1_ref[...], twr_ref[...], twi_ref[...]
    g2, g2c, g4 = g2_ref[...], g2c_ref[...], g4_ref[...]
    nb = 2 if paired else 1
    cb = kf_ref.shape[1]
    G = min(cb, CH_GROUP)
    n_groups = cb // G
    P = FFT_P

    @pl.when(pl.program_id(1) == 0)
    def _():
        g1f = g1f_ref[...]
        L = S * P
        delta = delta_ref[...]
        halves = ((wof_ref, af_ref, jnp.exp(-(delta * tf_ref[...]))),
                  (wob_ref, ab_ref, jnp.where(lax.broadcasted_iota(jnp.int32, (cb, L), 1) == 0, 0.0,
                                              jnp.exp(-(delta * tb_ref[...])))))
        for half, (wo_ref, a_ref, decay) in enumerate(halves):
            wo = wo_ref[...].reshape(HY_ORDER * cb, HY_FILTER_W)
            h = jnp.dot(wo, a_ref[...], preferred_element_type=F32)
            for o in range(HY_ORDER):
                taps_ref[o, :, half * L:(half + 1) * L] = h[o * cb:(o + 1) * cb] * decay

        def filter_group(gi, carry):
            c0 = pl.multiple_of(gi * G, G)
            for o in range(HY_ORDER):
                k = taps_ref[o, pl.ds(c0, G), :]
                norm = jnp.sum(jnp.abs(k), axis=-1, keepdims=True)
                tiles = _to_tiles(k, Q)
                spec = _fwd_fft([tiles[c] for c in range(G)], None, g1f, twr, twi, g2)
                kf = spec.reshape(G, Q, 2 * P) * (1.0 / norm)[:, :, None]
                kf_ref[o, pl.ds(c0, G)] = kf.astype(kf_ref.dtype)
            return carry

        lax.fori_loop(0, n_groups, filter_group, 0)

    def group(gi, carry):
        c0 = pl.multiple_of(gi * G, G)
        r0 = pl.multiple_of(gi * (G * V7X_SUBLANES), G * V7X_SUBLANES)

        def tiles(ref, b):
            blk = ref[b, :, pl.ds(r0, G * V7X_SUBLANES), :]
            return [blk[:, c * V7X_SUBLANES:(c + 1) * V7X_SUBLANES, :].reshape(S, P) for c in range(G)]

        z = [tiles(v_ref, b) for b in range(nb)]
        gates = [[tiles(x1_ref, b) for b in range(nb)], [tiles(x2_ref, b) for b in range(nb)]]
        for o in range(HY_ORDER):
            re = z[0]
            im = z[1] if paired else None
            spec = _fwd_fft(re, im, g1, twr, twi, g2)
            kf = kf_ref[o, pl.ds(c0, G)].reshape(G * Q, 2 * P)
            spec = spec.astype(kf.dtype)
            xr, xi = spec[:, :P], spec[:, P:]
            kr, ki = kf[:, :P], kf[:, P:]
            prod = jnp.concatenate([xr * kr - xi * ki, xr * ki + xi * kr], axis=1)
            out_re, out_im = _inv_fft(prod, g2c, twr, twi, g4, paired)
            skip = skip_ref[o, pl.ds(c0, G)]
            conv = [out_re] + ([out_im] if paired else [])
            z = [[gates[o][b][c] * (conv[b][c] + skip[c] * z[b][c]) for c in range(G)] for b in range(nb)]
        for b in range(nb):
            o_ref[b, pl.ds(c0, G), :] = _from_tiles(jnp.stack(z[b], axis=0)).astype(o_ref.dtype)
        return carry

    lax.fori_loop(0, n_groups, group, 0)


def hyena_operator(u_tiles, a, a_b, f_wout, skip, tabs, *, cb):
    B, nt, rows, P = u_tiles.shape
    D = rows // (3 * V7X_SUBLANES)
    L = nt * IN_BT
    S = L // P
    Q = tabs["Q"]
    W = HY_FILTER_W
    paired = B % 2 == 0
    nb = 2 if paired else 1
    wo = f_wout.T.reshape(HY_ORDER, 2, D, W).astype(BF16)
    wo_f, wo_b = wo[:, 0], wo[:, 1]
    deltas = np.abs(np.linspace(math.log(HY_TARGET) / HY_SLOW_PCT,
                                math.log(HY_TARGET) / HY_FAST_PCT, D)).astype(np.float32).reshape(D, 1)
    t_f = jnp.linspace(0.0, 1.0, L, dtype=F32).reshape(1, L)
    t_b = ((L - jnp.arange(L, dtype=F32)) / (L - 1)).reshape(1, L)
    skip_rows = jnp.broadcast_to(skip.astype(F32)[:, :, None, None], (HY_ORDER, D, 1, P))

    bf = lambda name: jnp.asarray(tabs[name]).astype(BF16)
    g1, g1f, g2, g2c, g4 = bf("g1"), bf("g1_full"), bf("g2"), bf("g2c"), bf("g4")
    twr, twi = bf("twr"), bf("twi")
    const = lambda arr: pl.BlockSpec(arr.shape, lambda c, p: (0,) * arr.ndim, pipeline_mode=pl.Buffered(1))
    ncb = D // cb
    return pl.pallas_call(
        functools.partial(_hyena_body, S=S, Q=Q, paired=paired),
        grid=(ncb, B // nb),
        in_specs=[
            pl.BlockSpec((nb, nt, cb * V7X_SUBLANES, P), lambda c, p: (p, 0, c, 0)),
            pl.BlockSpec((nb, nt, cb * V7X_SUBLANES, P), lambda c, p: (p, 0, c + ncb, 0)),
            pl.BlockSpec((nb, nt, cb * V7X_SUBLANES, P), lambda c, p: (p, 0, c + 2 * ncb, 0)),
            const(a), const(a_b),
            pl.BlockSpec((HY_ORDER, cb, W), lambda c, p: (0, c, 0)),
            pl.BlockSpec((HY_ORDER, cb, W), lambda c, p: (0, c, 0)),
            pl.BlockSpec((cb, 1), lambda c, p: (c, 0)),
            const(t_f), const(t_b),
            pl.BlockSpec((HY_ORDER, cb, 1, P), lambda c, p: (0, c, 0, 0)),
            const(g1), const(g1f), const(twr), const(twi), const(g2), const(g2c), const(g4),
        ],
        out_specs=pl.BlockSpec((nb, cb, L), lambda c, p: (p, c, 0)),
        out_shape=jax.ShapeDtypeStruct((B, D, L), BF16),
        scratch_shapes=[pltpu.VMEM((HY_ORDER, cb, 2 * L), F32), pltpu.VMEM((HY_ORDER, cb, Q, 2 * P), BF16)],
        compiler_params=_cparams("parallel", "arbitrary"),
        name="hyena_operator",
    )(u_tiles, u_tiles, u_tiles, a, a_b, wo_f, wo_b, jnp.asarray(deltas), t_f, t_b, skip_rows,
      g1, g1f, twr, twi, g2, g2c, g4)


def _trunk(x, p, cfg):
    B, L, D = x.shape
    tabs = _dft_tables(L, paired=(B % 2 == 0))
    for i in range(DEPTH):
        j = i // N_MIXERS
        if i % N_MIXERS == 0:
            a, a_b = hyena_filter_mlp(L, p["hy_f_w1"][j], p["hy_f_b1"][j], p["hy_f_w2"][j], p["hy_f_b2"][j],
                                      p["hy_f_w3"][j], p["hy_f_b3"][j], p["hy_f_freq"][j])
            u_tiles = hyena_in_projection(x, p["norm_mix_g"][i], p["hy_w_in_t"][j], p["hy_b_in"][j],
                                          p["hy_conv_w"][j], p["hy_conv_b"][j], bc=1024)
            z_t = hyena_operator(u_tiles, a, a_b, p["hy_f_wout"][j], p["hy_skip"][j], tabs, cb=cfg["hy_cb"])
            x = residual_matmul_t(x, z_t, p["hy_w_out"][j], p["hy_b_out"][j], bt=512)
        else:
            x2 = x.reshape(B * L, D)
            qkv = qkv_projection(x2, p["norm_mix_g"][i], p["at_w_qkv"][j], p["at_q_g"][j], p["at_k_g"][j], bm=1024)
            att = window_attention(qkv.reshape(B, L, -1), p["at_sink"][j], p["rel_bias"])
            x = residual_matmul(x2, att.reshape(B * L, -1), p["at_w_o"][j], bm=1024).reshape(B, L, D)
        x = ffn_block(x.reshape(B * L, D), p["norm_ffn_g"][i], p["ffn_w_gate_up"][i], p["ffn_w_down"][i],
                      bm=1024, bf=512).reshape(B, L, D)
    return x


def kernel(x_prompt, x_sample, norm_mix_g, norm_ffn_g, hy_w_in, hy_b_in, hy_conv_w, hy_conv_b, hy_f_w1, hy_f_b1,
           hy_f_w2, hy_f_b2, hy_f_w3, hy_f_b3, hy_f_wout, hy_f_freq, hy_skip, hy_w_out, hy_b_out, at_w_qkv, at_q_g,
           at_k_g, at_sink, at_w_o, rel_bias, ffn_w_gate_up, ffn_w_down):
    per_layer = lambda w: [w[i].astype(BF16) for i in range(w.shape[0])]
    p = dict(
        norm_mix_g=norm_mix_g.astype(F32), norm_ffn_g=norm_ffn_g.astype(F32),
        hy_w_in_t=[hy_w_in[j].T.astype(BF16) for j in range(hy_w_in.shape[0])], hy_b_in=hy_b_in,
        hy_conv_w=hy_conv_w, hy_conv_b=hy_conv_b,
        hy_f_w1=hy_f_w1, hy_f_b1=hy_f_b1, hy_f_w2=hy_f_w2, hy_f_b2=hy_f_b2, hy_f_w3=hy_f_w3, hy_f_b3=hy_f_b3,
        hy_f_wout=hy_f_wout, hy_f_freq=hy_f_freq, hy_skip=hy_skip,
        hy_w_out=per_layer(hy_w_out), hy_b_out=hy_b_out,
        at_w_qkv=per_layer(at_w_qkv), at_q_g=at_q_g, at_k_g=at_k_g, at_sink=at_sink,
        at_w_o=per_layer(at_w_o), rel_bias=rel_bias,
        ffn_w_gate_up=per_layer(ffn_w_gate_up), ffn_w_down=per_layer(ffn_w_down),
    )
    y_prompt = _trunk(x_prompt, p, dict(hy_cb=32))
    y_sample = _trunk(x_sample, p, dict(hy_cb=32))
    return (y_prompt, y_sample)
```

```python
import functools
import math

import jax
import jax.numpy as jnp
import numpy as np
from jax import lax
from jax.experimental import pallas as pl
from jax.experimental.pallas import tpu as pltpu

F32 = jnp.float32
BF16 = jnp.bfloat16

D_MODEL = 2048
DEPTH = 4
N_MIXERS = 2
HY_ORDER = 2
HY_EMB = 33
HY_BANDS = (HY_EMB - 1) // 2
HY_FILTER_W = 64
HY_FAST_PCT = 0.3
HY_SLOW_PCT = 1.5
HY_TARGET = 1e-2
N_HEADS = 16
HEAD_DIM = 128
N_KV = 4
GQA_G = N_HEADS // N_KV
WINDOW = 128
BLOCK = 128
N_BUCKETS = 32
MAX_DIST = 128
D_FF = -(-(8 * D_MODEL) // (3 * 256)) * 256
EPS = 1e-6
NEG = -1e30

V7X_LANES = 128
V7X_SUBLANES = 8
VMEM_LIMIT = 56 * 1024 * 1024

FFT_P = V7X_LANES
CH_GROUP = 4 * V7X_SUBLANES

ROW_BLOCK = 1024
FFN_CHUNK = 512
IN_PROJ_CHANNELS = 1024
OUT_PROJ_T = 512
CONV_CHANNELS = 32


def _cparams(*sem):
    return pltpu.CompilerParams(dimension_semantics=sem, vmem_limit_bytes=VMEM_LIMIT)


def _rms_bf16(x, g):
    ms = jnp.mean(x * x, axis=-1, keepdims=True)
    return (x * lax.rsqrt(ms + EPS) * g).astype(BF16)


def _qkv_proj_body(x_ref, g_ref, w_ref, qg_ref, kg_ref, o_ref):
    hn = _rms_bf16(x_ref[...], g_ref[...])
    nq, nk = N_HEADS * HEAD_DIM, N_KV * HEAD_DIM
    pair = 2 * HEAD_DIM
    for c0 in range(0, nq + nk, pair):
        acc = jnp.dot(hn, w_ref[:, c0:c0 + pair], preferred_element_type=F32)
        gain = qg_ref[...] if c0 < nq else kg_ref[...]
        for c in (0, HEAD_DIM):
            t = acc[:, c:c + HEAD_DIM]
            ms = jnp.mean(t * t, axis=-1, keepdims=True)
            o_ref[:, c0 + c:c0 + c + HEAD_DIM] = (t * lax.rsqrt(ms + EPS) * gain).astype(o_ref.dtype)
    o_ref[:, nq + nk:] = jnp.dot(hn, w_ref[:, nq + nk:], preferred_element_type=F32).astype(o_ref.dtype)


def qkv_projection(x, g, w, q_g, k_g, *, bm):
    T, K = x.shape
    N = w.shape[1]
    return pl.pallas_call(
        _qkv_proj_body,
        grid=(T // bm,),
        in_specs=[
            pl.BlockSpec((bm, K), lambda i: (i, 0)),
            pl.BlockSpec((1, K), lambda i: (0, 0)),
            pl.BlockSpec((K, N), lambda i: (0, 0), pipeline_mode=pl.Buffered(1)),
            pl.BlockSpec((1, HEAD_DIM), lambda i: (0, 0)),
            pl.BlockSpec((1, HEAD_DIM), lambda i: (0, 0)),
        ],
        out_specs=pl.BlockSpec((bm, N), lambda i: (i, 0)),
        out_shape=jax.ShapeDtypeStruct((T, N), BF16),
        compiler_params=_cparams("parallel"),
        name="qkv_projection",
    )(x, g.reshape(1, K), w, q_g.astype(F32).reshape(1, HEAD_DIM), k_g.astype(F32).reshape(1, HEAD_DIM))


IN_BT = V7X_SUBLANES * V7X_LANES
IN_ROW_SPLIT = 4


def _hyena_edge_body(x_ref, g_ref, w_ref, b_ref, o_ref):
    hn = _rms_bf16(x_ref[0], g_ref[...])
    o_ref[0] = lax.dot_general(hn, w_ref[...], (((1,), (1,)), ((), ())), preferred_element_type=F32) + b_ref[...]


def hyena_edge_projection(x_edge, g, w_t, bias, *, bc):
    B, n, K = x_edge.shape
    C = w_t.shape[0]
    return pl.pallas_call(
        _hyena_edge_body,
        grid=(B, C // bc),
        in_specs=[
            pl.BlockSpec((1, n, K), lambda b, c: (b, 0, 0)),
            pl.BlockSpec((1, K), lambda b, c: (0, 0)),
            pl.BlockSpec((bc, K), lambda b, c: (c, 0)),
            pl.BlockSpec((1, bc), lambda b, c: (0, c)),
        ],
        out_specs=pl.BlockSpec((1, n, bc), lambda b, c: (b, 0, c)),
        out_shape=jax.ShapeDtypeStruct((B, n, C), F32),
        compiler_params=_cparams("parallel", "parallel"),
        name="hyena_edge_projection",
    )(x_edge, g.reshape(1, K), w_t, bias.astype(F32).reshape(1, C))


def _hyena_in_body(x_ref, g_ref, w_ref, b_ref, k0_ref, k1_ref, k2_ref, kb_ref, edge_ref, o_ref, hn_ref):
    t = pl.program_id(1)
    nt = pl.num_programs(1)

    @pl.when(pl.program_id(2) == 0)
    def _():
        hn_ref[...] = _rms_bf16(x_ref[0], g_ref[...])

    P = V7X_LANES
    bt = hn_ref.shape[0]
    n_chunks = bt // P
    rows = w_ref.shape[0] // IN_ROW_SPLIT
    for r in range(IN_ROW_SPLIT):
        rs = slice(r * rows, (r + 1) * rows)
        u = lax.dot_general(w_ref[rs, :], hn_ref[...], (((1,), (1,)), ((), ())), preferred_element_type=F32)
        def border(k):
            parts = [jnp.transpose(jnp.broadcast_to(edge_ref[0, k, 0:1, c0:c0 + P], (P, P)))
                     for c0 in range(r * rows, (r + 1) * rows, P)]
            return jnp.concatenate(parts, axis=0)

        left = jnp.where(t == 0, 0.0, border(t))
        right = jnp.where(t == nt - 1, 0.0, border(nt + t))
        lane = lax.broadcasted_iota(jnp.int32, (rows, P), 1)
        bias, k0, k1, k2, kb = (ref[rs, :] for ref in (b_ref, k0_ref, k1_ref, k2_ref, kb_ref))
        chunks = [u[:, j * P:(j + 1) * P] + bias for j in range(n_chunks)]
        fwd = [pltpu.roll(c, 1, axis=1) for c in chunks]
        bwd = [pltpu.roll(c, P - 1, axis=1) for c in chunks]
        for j in range(n_chunks):
            prev = jnp.where(lane == 0, fwd[j - 1] if j > 0 else left, fwd[j])
            nxt = jnp.where(lane == P - 1, bwd[j + 1] if j + 1 < n_chunks else right, bwd[j])
            o_ref[pl.ds(r * rows * n_chunks + j, rows, stride=n_chunks), :] = (
                k0 * prev + k1 * chunks[j] + k2 * nxt + kb)


def hyena_in_projection(x, g, w_t, bias, conv_w, conv_b, *, bc):
    B, L, K = x.shape
    C = w_t.shape[0]
    bt = IN_BT
    nt = L // bt
    xb = x.reshape(B, nt, bt, K)
    x_edge = jnp.concatenate([jnp.roll(xb[:, :, bt - 1], 1, axis=1), jnp.roll(xb[:, :, 0], -1, axis=1)], axis=1)
    edge = hyena_edge_projection(x_edge, g, w_t, bias, bc=bc)
    edge = jnp.broadcast_to(edge[:, :, None, :], (B, 2 * nt, V7X_SUBLANES, C))
    col = lambda v: jnp.broadcast_to(v.astype(F32).reshape(C, 1), (C, V7X_LANES))
    cspec = pl.BlockSpec((bc, V7X_LANES), lambda b, t, c: (c, 0))
    return pl.pallas_call(
        _hyena_in_body,
        grid=(B, nt, C // bc),
        in_specs=[
            pl.BlockSpec((1, bt, K), lambda b, t, c: (b, t, 0)),
            pl.BlockSpec((1, K), lambda b, t, c: (0, 0)),
            pl.BlockSpec((bc, K), lambda b, t, c: (c, 0)),
            cspec, cspec, cspec, cspec, cspec,
            pl.BlockSpec((1, 2 * nt, V7X_SUBLANES, bc), lambda b, t, c: (b, 0, 0, c)),
        ],
        out_specs=pl.BlockSpec((None, None, bc * V7X_SUBLANES, V7X_LANES), lambda b, t, c: (b, t, c, 0)),
        out_shape=jax.ShapeDtypeStruct((B, nt, C * V7X_SUBLANES, V7X_LANES), F32),
        scratch_shapes=[pltpu.VMEM((bt, K), BF16)],
        compiler_params=_cparams("parallel", "parallel", "arbitrary"),
        name="hyena_in_projection",
    )(x, g.reshape(1, K), w_t, col(bias), col(conv_w[0]), col(conv_w[1]), col(conv_w[2]), col(conv_b), edge)


def _res_mm_body(a_ref, w_ref, x_ref, o_ref):
    o_ref[...] = x_ref[...] + jnp.dot(a_ref[...], w_ref[...], preferred_element_type=F32)


def residual_matmul(x, a, w, *, bm):
    T, K = a.shape
    N = w.shape[1]
    return pl.pallas_call(
        _res_mm_body,
        grid=(T // bm,),
        in_specs=[
            pl.BlockSpec((bm, K), lambda i: (i, 0)),
            pl.BlockSpec((K, N), lambda i: (0, 0), pipeline_mode=pl.Buffered(1)),
            pl.BlockSpec((bm, N), lambda i: (i, 0)),
        ],
        out_specs=pl.BlockSpec((bm, N), lambda i: (i, 0)),
        out_shape=jax.ShapeDtypeStruct((T, N), F32),
        compiler_params=_cparams("parallel"),
        name="residual_matmul",
    )(a, w, x)


def _res_mm_t_body(z_ref, w_ref, b_ref, x_ref, o_ref):
    y = lax.dot_general(z_ref[0], w_ref[...], (((0,), (0,)), ((), ())), preferred_element_type=F32)
    o_ref[0] = x_ref[0] + y + b_ref[...]


def residual_matmul_t(x, z_t, w, bias, *, bt):
    B, K, L = z_t.shape
    N = w.shape[1]
    return pl.pallas_call(
        _res_mm_t_body,
        grid=(B, L // bt),
        in_specs=[
            pl.BlockSpec((1, K, bt), lambda b, t: (b, 0, t)),
            pl.BlockSpec((K, N), lambda b, t: (0, 0), pipeline_mode=pl.Buffered(1)),
            pl.BlockSpec((1, N), lambda b, t: (0, 0)),
            pl.BlockSpec((1, bt, N), lambda b, t: (b, t, 0)),
        ],
        out_specs=pl.BlockSpec((1, bt, N), lambda b, t: (b, t, 0)),
        out_shape=jax.ShapeDtypeStruct((B, L, N), F32),
        compiler_params=_cparams("parallel", "parallel"),
        name="residual_matmul_t",
    )(z_t, w, bias.reshape(1, N), x)


def _ffn_body(x_ref, g_ref, wg_ref, wu_ref, wd_ref, o_ref, hn_ref):
    @pl.when(pl.program_id(1) == 0)
    def _():
        x = x_ref[...]
        hn_ref[...] = _rms_bf16(x, g_ref[...])
        o_ref[...] = x

    h = hn_ref[...]
    gate = jnp.dot(h, wg_ref[...], preferred_element_type=F32)
    up = jnp.dot(h, wu_ref[...], preferred_element_type=F32)
    act = (gate * jax.nn.sigmoid(gate) * up).astype(BF16)
    o_ref[...] += jnp.dot(act, wd_ref[...], preferred_element_type=F32)


def ffn_block(x, g, w_gate_up, w_down, *, bm, bf):
    T, K = x.shape
    nf = D_FF // bf
    return pl.pallas_call(
        _ffn_body,
        grid=(T // bm, nf),
        in_specs=[
            pl.BlockSpec((bm, K), lambda i, f: (i, 0)),
            pl.BlockSpec((1, K), lambda i, f: (0, 0)),
            pl.BlockSpec((K, bf), lambda i, f: (0, f)),
            pl.BlockSpec((K, bf), lambda i, f: (0, f + nf)),
            pl.BlockSpec((bf, K), lambda i, f: (f, 0)),
        ],
        out_specs=pl.BlockSpec((bm, K), lambda i, f: (i, 0)),
        out_shape=jax.ShapeDtypeStruct((T, K), F32),
        scratch_shapes=[pltpu.VMEM((bm, K), BF16)],
        compiler_params=_cparams("parallel", "arbitrary"),
        name="ffn_block",
    )(x, g.reshape(1, K), w_gate_up, w_gate_up, w_down)


ATT_TQ = 4 * BLOCK
KV_W = N_KV * HEAD_DIM


def _band_structure():
    qi = np.arange(BLOCK)[:, None]
    ki = np.arange(3 * BLOCK)[None, :]
    rel = ki - BLOCK - qi
    nb = N_BUCKETS // 2
    max_exact = nb // 2
    n = np.abs(rel)
    large = max_exact + (np.log(np.maximum(n, 1) / max_exact) / math.log(MAX_DIST / max_exact)
                         * (nb - max_exact)).astype(np.int32)
    large = np.minimum(large, nb - 1)
    buckets = (rel > 0).astype(np.int32) * nb + np.where(n < max_exact, n, large).astype(np.int32)
    band = n <= WINDOW
    return buckets, band


def _attn_body(q_ref, kp_ref, kc_ref, kn_ref, vp_ref, vc_ref, vn_ref, bias_ref, sink_ref,
               o_ref, kbuf, vbuf):
    i = pl.program_id(1)
    last = pl.num_programs(1) - 1

    kbuf[0:BLOCK] = kp_ref[0]
    kbuf[BLOCK:BLOCK + ATT_TQ] = kc_ref[0]
    kbuf[BLOCK + ATT_TQ:] = kn_ref[0]
    ones = jnp.ones((ATT_TQ + 2 * BLOCK, HEAD_DIM), BF16)
    for g in range(N_KV):
        src = slice(g * HEAD_DIM, (g + 1) * HEAD_DIM)
        dst = slice(2 * g * HEAD_DIM, (2 * g + 1) * HEAD_DIM)
        vbuf[0:BLOCK, dst] = vp_ref[0, :, src]
        vbuf[BLOCK:BLOCK + ATT_TQ, dst] = vc_ref[0, :, src]
        vbuf[BLOCK + ATT_TQ:, dst] = vn_ref[0, :, src]
        vbuf[:, (2 * g + 1) * HEAD_DIM:(2 * g + 2) * HEAD_DIM] = ones

    lane = lax.broadcasted_iota(jnp.int32, (1, 3 * BLOCK), 1)
    first_edge = jnp.where((lane < BLOCK) & (i == 0), NEG, 0.0).astype(F32)
    last_edge = jnp.where((lane >= 2 * BLOCK) & (i == last), NEG, 0.0).astype(F32)
    exp2_scale = HEAD_DIM ** -0.5 * math.log2(math.e)

    n_sub = ATT_TQ // BLOCK

    def scores(j, g):
        r0 = j * BLOCK
        qs = jnp.concatenate(
            [q_ref[0, r0:r0 + BLOCK, (GQA_G * g + h) * HEAD_DIM:(GQA_G * g + h + 1) * HEAD_DIM]
             for h in range(GQA_G)], axis=0)
        kw = kbuf[r0:r0 + 3 * BLOCK, g * HEAD_DIM:(g + 1) * HEAD_DIM]
        s = lax.dot_general(qs, kw, (((1,), (1,)), ((), ())), preferred_element_type=F32)
        s = s + bias_ref[g]
        if j == 0:
            s = s + first_edge
        if j == n_sub - 1:
            s = s + last_edge
        return s

    def softmax_numerators(g, s):
        sk = sink_ref[g][:, 0:1]
        m = jnp.maximum(jnp.max(s, axis=-1, keepdims=True), sk)
        return jnp.exp2((s - m) * exp2_scale).astype(BF16), jnp.exp2((sk - m) * exp2_scale)

    def outputs(j, g, p, sink_p):
        r0 = j * BLOCK
        vw = vbuf[r0:r0 + 3 * BLOCK, 2 * g * HEAD_DIM:(2 * g + 2) * HEAD_DIM]
        pv = jnp.dot(p, vw, preferred_element_type=F32)
        o = pv[:, :HEAD_DIM] / (pv[:, HEAD_DIM:] + sink_p)
        for h in range(GQA_G):
            c0 = (GQA_G * g + h) * HEAD_DIM
            o_ref[0, r0:r0 + BLOCK, c0:c0 + HEAD_DIM] = o[h * BLOCK:(h + 1) * BLOCK].astype(o_ref.dtype)

    chains = [(j, g) for j in range(n_sub) for g in range(N_KV)]
    s_q, p_q = {}, {}
    for step in range(len(chains) + 2):
        if step >= 2:
            outputs(*chains[step - 2], *p_q.pop(step - 2))
        if 1 <= step <= len(chains):
            p_q[step - 1] = softmax_numerators(chains[step - 1][1], s_q.pop(step - 1))
        if step < len(chains):
            s_q[step] = scores(*chains[step])


def window_attention(qkv, sink, rel_bias):
    B, L, _ = qkv.shape
    nq = N_HEADS * HEAD_DIM
    sub = ATT_TQ // BLOCK
    nblk = L // BLOCK
    kcol = nq // KV_W
    vcol = kcol + 1

    buckets, band = _band_structure()
    onehot = (jnp.asarray(buckets)[:, :, None] == jnp.arange(N_BUCKETS)).astype(F32)
    bias = jnp.einsum("qkn,nh->qkh", onehot, rel_bias.astype(F32), precision=lax.Precision.HIGHEST)
    inv_scale = HEAD_DIM ** 0.5
    bias = jnp.where(band[:, :, None], bias * inv_scale, NEG)
    bias = jnp.transpose(bias, (2, 0, 1)).reshape(N_KV, GQA_G * BLOCK, 3 * BLOCK)
    sink_rows = jnp.broadcast_to((sink.astype(F32) * inv_scale).reshape(N_KV, GQA_G, 1, 1),
                                 (N_KV, GQA_G, BLOCK, V7X_LANES)).reshape(N_KV, GQA_G * BLOCK, V7X_LANES)

    return pl.pallas_call(
        _attn_body,
        grid=(B, L // ATT_TQ),
        in_specs=[
            pl.BlockSpec((1, ATT_TQ, nq), lambda b, i: (b, i, 0)),
            pl.BlockSpec((1, BLOCK, KV_W), lambda b, i: (b, jnp.maximum(sub * i - 1, 0), kcol)),
            pl.BlockSpec((1, ATT_TQ, KV_W), lambda b, i: (b, i, kcol)),
            pl.BlockSpec((1, BLOCK, KV_W), lambda b, i: (b, jnp.minimum(sub * i + sub, nblk - 1), kcol)),
            pl.BlockSpec((1, BLOCK, KV_W), lambda b, i: (b, jnp.maximum(sub * i - 1, 0), vcol)),
            pl.BlockSpec((1, ATT_TQ, KV_W), lambda b, i: (b, i, vcol)),
            pl.BlockSpec((1, BLOCK, KV_W), lambda b, i: (b, jnp.minimum(sub * i + sub, nblk - 1), vcol)),
            pl.BlockSpec((N_KV, GQA_G * BLOCK, 3 * BLOCK), lambda b, i: (0, 0, 0)),
            pl.BlockSpec((N_KV, GQA_G * BLOCK, V7X_LANES), lambda b, i: (0, 0, 0)),
        ],
        out_specs=pl.BlockSpec((1, ATT_TQ, nq), lambda b, i: (b, i, 0)),
        out_shape=jax.ShapeDtypeStruct((B, L, nq), BF16),
        scratch_shapes=[pltpu.VMEM((ATT_TQ + 2 * BLOCK, KV_W), BF16),
                        pltpu.VMEM((ATT_TQ + 2 * BLOCK, 2 * KV_W), BF16)],
        compiler_params=_cparams("parallel", "parallel"),
        name="window_attention",
    )(qkv, qkv, qkv, qkv, qkv, qkv, qkv, bias, sink_rows)


FEAT_PAD = V7X_LANES


def _filter_mlp_body(ff_ref, fb_ref, w1_ref, b1_ref, w2_ref, b2_ref, w3_ref, b3_ref, fr_ref, af_ref, ab_ref):
    fr = fr_ref[...]
    for feat_ref, a_ref in ((ff_ref, af_ref), (fb_ref, ab_ref)):
        a = jnp.sin(fr * (jnp.dot(w1_ref[...], feat_ref[...], preferred_element_type=F32) + b1_ref[...]))
        a = jnp.sin(fr * (jnp.dot(w2_ref[...], a.astype(BF16), preferred_element_type=F32) + b2_ref[...]))
        a = jnp.sin(fr * (jnp.dot(w3_ref[...], a.astype(BF16), preferred_element_type=F32) + b3_ref[...]))
        a_ref[...] = a.astype(BF16)


def hyena_filter_mlp(L, f_w1, f_b1, f_w2, f_b2, f_w3, f_b3, f_freq, *, bt=2048):
    W = HY_FILTER_W
    f = jnp.linspace(1e-4, HY_BANDS - 1, HY_BANDS, dtype=F32)

    def features(pos):
        t = pos / (L - 1)
        ang = (2.0 * math.pi * pos / L)[None, :] * f[:, None]
        rows = jnp.concatenate([t[None, :], jnp.cos(ang), -jnp.sin(ang), jnp.zeros((FEAT_PAD - HY_EMB, L), F32)])
        return rows.astype(BF16)

    pos = jnp.arange(L, dtype=F32)
    w1_t = jnp.concatenate([f_w1.astype(F32).T, jnp.zeros((W, FEAT_PAD - HY_EMB), F32)], axis=1)
    col = lambda v: v.astype(F32).reshape(W, 1)
    const = lambda r, c: pl.BlockSpec((r, c), lambda i: (0, 0))
    fspec = pl.BlockSpec((FEAT_PAD, bt), lambda i: (0, i))
    ospec = pl.BlockSpec((W, bt), lambda i: (0, i))
    return pl.pallas_call(
        _filter_mlp_body,
        grid=(L // bt,),
        in_specs=[fspec, fspec, const(W, FEAT_PAD), const(W, 1), const(W, W), const(W, 1), const(W, W), const(W, 1),
                  const(W, 1)],
        out_specs=[ospec, ospec],
        out_shape=[jax.ShapeDtypeStruct((W, L), BF16), jax.ShapeDtypeStruct((W, L), BF16)],
        compiler_params=_cparams("parallel"),
        name="hyena_filter_mlp",
    )(features(pos), features(L - pos), w1_t.astype(BF16), col(f_b1), f_w2.T.astype(BF16), col(f_b2),
      f_w3.T.astype(BF16), col(f_b3), col(f_freq))


def _dft_tables(L, paired):
    N = 2 * L
    P = FFT_P
    Q = N // P
    S = L // P
    b = np.arange(Q, dtype=np.float64)
    fq = np.exp(-2j * np.pi * np.outer(b, np.arange(S)) / Q)
    fp =np.exp(-2j * np.pi * np.outer(np.arange(P), np.arange(P)) / P)
    tw = np.exp(-2j * np.pi * np.outer(b, np.arange(P)) / N)
    ci = np.conj(fq).T / N

    def stack(c):
        return np.block([[c.real, -c.imag], [c.imag, c.real]])

    if paired:
        g1 = stack(fq)
        g4 = stack(ci)
    else:
        g1 = np.concatenate([fq.real, fq.imag], axis=0)
        g4 = np.concatenate([ci.real, -ci.imag], axis=1)
    fq_full = np.exp(-2j * np.pi * np.outer(b, np.arange(Q)) / Q)
    g1_full = np.concatenate([fq_full.real, fq_full.imag], axis=0)
    g2 =np.block([[fp.real, fp.imag], [-fp.imag, fp.real]])
    g2c = np.block([[fp.real, -fp.imag], [fp.imag, fp.real]])
    twr = np.tile(tw.real, (1, 2))
    twi = np.tile(tw.imag, (1, 2))
    f32 = lambda a: np.ascontiguousarray(a, dtype=np.float32)
    return dict(g1=f32(g1), g4=f32(g4), g1_full=f32(g1_full), g2=f32(g2), g2c=f32(g2c),
                twr=f32(twr), twi=f32(twi), Q=Q, S=S, N=N)


def _to_tiles(x, n_tiles):
    chunks = [x[:, s * FFT_P:(s + 1) * FFT_P] for s in range(n_tiles)]
    return jnp.swapaxes(jnp.stack(chunks, axis=0), 0, 1)


def _from_tiles(x):
    y = jnp.swapaxes(x, 0, 1)
    return jnp.concatenate([y[s] for s in range(y.shape[0])], axis=1)


def _fwd_fft(re_tiles, im_tiles, g1, twr, twi, g2):
    G = len(re_tiles)
    Q = twr.shape[0]
    P = FFT_P
    rows = []
    for c in range(0, G, 2):
        top = jnp.concatenate([re_tiles[c], re_tiles[c + 1]], axis=1)
        if im_tiles is None:
            rhs = top
        else:
            rhs = jnp.concatenate([top, jnp.concatenate([im_tiles[c], im_tiles[c + 1]], axis=1)], axis=0)
        y = jnp.dot(g1, rhs.astype(BF16), preferred_element_type=F32)
        y = y.astype(twr.dtype)
        yr, yi = y[:Q], y[Q:]
        zr = yr * twr - yi * twi
        zi = yr * twi + yi * twr
        rows.append(jnp.concatenate([zr[:, :P], zi[:, :P]], axis=1))
        rows.append(jnp.concatenate([zr[:, P:], zi[:, P:]], axis=1))
    lhs = jnp.concatenate(rows, axis=0).astype(BF16)
    return jnp.dot(lhs, g2, preferred_element_type=F32)


def _inv_fft(spec, g2c, twr, twi, g4, want_imag):
    Q = twr.shape[0]
    P = FFT_P
    G = spec.shape[0] // Q
    S = g4.shape[0] // 2 if want_imag else g4.shape[0]
    y = jnp.dot(spec.astype(BF16), g2c, preferred_element_type=F32)
    y = y.astype(twr.dtype)
    out_re, out_im = [], []
    for c in range(0, G, 2):
        ya = y[c * Q:(c + 1) * Q]
        yb = y[(c + 1) * Q:(c + 2) * Q]
        yr = jnp.concatenate([ya[:, :P], yb[:, :P]], axis=1)
        yi = jnp.concatenate([ya[:, P:], yb[:, P:]], axis=1)
        zr = yr * twr + yi * twi
        zi = yi * twr - yr * twi
        rhs = jnp.concatenate([zr, zi], axis=0).astype(BF16)
        o = jnp.dot(g4, rhs, preferred_element_type=F32)
        out_re += [o[:S, :P], o[:S, P:]]
        if want_imag:
            out_im += [o[S:, :P], o[S:, P:]]
    return out_re, out_im


def _hyena_body(v_ref, x1_ref, x2_ref, af_ref, ab_ref, wof_ref, wob_ref, delta_ref, tf_ref, tb_ref, skip_ref,
                g1_ref, g1f_ref, twr_ref, twi_ref, g2_ref, g2c_ref, g4_ref, o_ref, taps_ref, kf_ref,
                *, S, Q, paired):
    g1, twr, twi = g1_ref[...], twr_ref[...], twi_ref[...]
    g2, g2c, g4 = g2_ref[...], g2c_ref[...], g4_ref[...]
    nb = 2 if paired else 1
    cb = kf_ref.shape[1]
    G = min(cb, CH_GROUP)
    n_groups = cb // G
    P = FFT_P

    @pl.when(pl.program_id(1) == 0)
    def _():
        g1f = g1f_ref[...]
        L = S * P
        delta = delta_ref[...]
        halves = ((wof_ref, af_ref, jnp.exp(-(delta * tf_ref[...]))),
                  (wob_ref, ab_ref, jnp.where(lax.broadcasted_iota(jnp.int32, (cb, L), 1) == 0, 0.0,
                                              jnp.exp(-(delta * tb_ref[...])))))
        for half, (wo_ref, a_ref, decay) in enumerate(halves):
            wo = wo_ref[...].reshape(HY_ORDER * cb, HY_FILTER_W)
            h = jnp.dot(wo, a_ref[...], preferred_element_type=F32)
            for o in range(HY_ORDER):
                taps_ref[o, :, half * L:(half + 1) * L] = h[o * cb:(o + 1) * cb] * decay

        def filter_group(gi, carry):
            c0 = pl.multiple_of(gi * G, G)
            for o in range(HY_ORDER):
                k = taps_ref[o, pl.ds(c0, G), :]
                norm = jnp.sum(jnp.abs(k), axis=-1, keepdims=True)
                tiles = _to_tiles(k, Q)
                spec = _fwd_fft([tiles[c] for c in range(G)], None, g1f, twr, twi, g2)
                kf = spec.reshape(G, Q, 2 * P) * (1.0 / norm)[:, :, None]
                kf_ref[o, pl.ds(c0, G)] = kf.astype(kf_ref.dtype)
            return carry

        lax.fori_loop(0, n_groups, filter_group, 0)

    def group(gi, carry):
        c0 = pl.multiple_of(gi * G, G)
        r0 = pl.multiple_of(gi * (G * V7X_SUBLANES), G * V7X_SUBLANES)

        def tiles(ref, b):
            blk = ref[b, :, pl.ds(r0, G * V7X_SUBLANES), :]
            return [blk[:, c * V7X_SUBLANES:(c + 1) * V7X_SUBLANES, :].reshape(S, P) for c in range(G)]

        z = [tiles(v_ref, b) for b in range(nb)]
        gates = [[tiles(x1_ref, b) for b in range(nb)], [tiles(x2_ref, b) for b in range(nb)]]
        for o in range(HY_ORDER):
            re = z[0]
            im = z[1] if paired else None
            spec = _fwd_fft(re, im, g1, twr, twi, g2)
            kf = kf_ref[o, pl.ds(c0, G)].reshape(G * Q, 2 * P)
            spec = spec.astype(kf.dtype)
            xr, xi = spec[:, :P], spec[:, P:]
            kr, ki = kf[:, :P], kf[:, P:]
            prod = jnp.concatenate([xr * kr - xi * ki, xr * ki + xi * kr], axis=1)
            out_re, out_im = _inv_fft(prod, g2c, twr, twi, g4, paired)
            skip = skip_ref[o, pl.ds(c0, G)]
            conv = [out_re] + ([out_im] if paired else [])
            z = [[gates[o][b][c] * (conv[b][c] + skip[c] * z[b][c]) for c in range(G)] for b in range(nb)]
        for b in range(nb):
            o_ref[b, pl.ds(c0, G), :] = _from_tiles(jnp.stack(z[b], axis=0)).astype(o_ref.dtype)
        return carry

    lax.fori_loop(0, n_groups, group, 0)


def hyena_operator(u_tiles, a, a_b, f_wout, skip, tabs, *, cb):
    B, nt, rows, P = u_tiles.shape
    D = rows // (3 * V7X_SUBLANES)
    L = nt * IN_BT
    S = L // P
    Q = tabs["Q"]
    W = HY_FILTER_W
    paired = B % 2 == 0
    nb = 2 if paired else 1
    wo = f_wout.T.reshape(HY_ORDER, 2, D, W).astype(BF16)
    wo_f, wo_b = wo[:, 0], wo[:, 1]
    deltas = np.abs(np.linspace(math.log(HY_TARGET) / HY_SLOW_PCT,
                                math.log(HY_TARGET) / HY_FAST_PCT, D)).astype(np.float32).reshape(D, 1)
    t_f = jnp.linspace(0.0, 1.0, L, dtype=F32).reshape(1, L)
    t_b = ((L - jnp.arange(L, dtype=F32)) / (L - 1)).reshape(1, L)
    skip_rows = jnp.broadcast_to(skip.astype(F32)[:, :, None, None], (HY_ORDER, D, 1, P))

    bf = lambda name: jnp.asarray(tabs[name]).astype(BF16)
    g1, g1f, g2, g2c, g4 = bf("g1"), bf("g1_full"), bf("g2"), bf("g2c"), bf("g4")
    twr, twi = bf("twr"), bf("twi")
    const = lambda arr: pl.BlockSpec(arr.shape, lambda c, p: (0,) * arr.ndim, pipeline_mode=pl.Buffered(1))
    ncb = D // cb
    return pl.pallas_call(
        functools.partial(_hyena_body, S=S, Q=Q, paired=paired),
        grid=(ncb, B // nb),
        in_specs=[
            pl.BlockSpec((nb, nt, cb * V7X_SUBLANES, P), lambda c, p: (p, 0, c, 0)),
            pl.BlockSpec((nb, nt, cb * V7X_SUBLANES, P), lambda c, p: (p, 0, c + ncb, 0)),
            pl.BlockSpec((nb, nt, cb * V7X_SUBLANES, P), lambda c, p: (p, 0, c + 2 * ncb, 0)),
            const(a), const(a_b),
            pl.BlockSpec((HY_ORDER, cb, W), lambda c, p: (0, c, 0)),
            pl.BlockSpec((HY_ORDER, cb, W), lambda c, p: (0, c, 0)),
            pl.BlockSpec((cb, 1), lambda c, p: (c, 0)),
            const(t_f), const(t_b),
            pl.BlockSpec((HY_ORDER, cb, 1, P), lambda c, p: (0, c, 0, 0)),
            const(g1), const(g1f), const(twr), const(twi), const(g2), const(g2c), const(g4),
        ],
        out_specs=pl.BlockSpec((nb, cb, L), lambda c, p: (p, c, 0)),
        out_shape=jax.ShapeDtypeStruct((B, D, L), BF16),
        scratch_shapes=[pltpu.VMEM((HY_ORDER, cb, 2 * L), F32), pltpu.VMEM((HY_ORDER, cb, Q, 2 * P), BF16)],
        compiler_params=_cparams("parallel", "arbitrary"),
        name="hyena_operator",
    )(u_tiles, u_tiles, u_tiles, a, a_b, wo_f, wo_b, jnp.asarray(deltas), t_f, t_b, skip_rows,
      g1, g1f, twr, twi, g2, g2c, g4)


def _trunk(x, p):
    B, L, D = x.shape
    tabs = _dft_tables(L, paired=(B % 2 == 0))
    for i in range(DEPTH):
        j = i // N_MIXERS
        if i % N_MIXERS == 0:
            a, a_b = hyena_filter_mlp(L, p["hy_f_w1"][j], p["hy_f_b1"][j], p["hy_f_w2"][j], p["hy_f_b2"][j],
                                      p["hy_f_w3"][j], p["hy_f_b3"][j], p["hy_f_freq"][j])
            u_tiles = hyena_in_projection(x, p["norm_mix_g"][i], p["hy_w_in_t"][j], p["hy_b_in"][j],
                                          p["hy_conv_w"][j], p["hy_conv_b"][j], bc=IN_PROJ_CHANNELS)
            z_t = hyena_operator(u_tiles, a, a_b, p["hy_f_wout"][j], p["hy_skip"][j], tabs, cb=CONV_CHANNELS)
            x = residual_matmul_t(x, z_t, p["hy_w_out"][j], p["hy_b_out"][j], bt=OUT_PROJ_T)
        else:
            x2 = x.reshape(B * L, D)
            qkv = qkv_projection(x2, p["norm_mix_g"][i], p["at_w_qkv"][j], p["at_q_g"][j], p["at_k_g"][j],
                                 bm=ROW_BLOCK)
            att = window_attention(qkv.reshape(B, L, -1), p["at_sink"][j], p["rel_bias"])
            x = residual_matmul(x2, att.reshape(B * L, -1), p["at_w_o"][j], bm=ROW_BLOCK).reshape(B, L, D)
        x = ffn_block(x.reshape(B * L, D), p["norm_ffn_g"][i], p["ffn_w_gate_up"][i], p["ffn_w_down"][i],
                      bm=ROW_BLOCK, bf=FFN_CHUNK).reshape(B, L, D)
    return x


def kernel(x_prompt, x_sample, norm_mix_g, norm_ffn_g, hy_w_in, hy_b_in, hy_conv_w, hy_conv_b, hy_f_w1, hy_f_b1,
           hy_f_w2, hy_f_b2, hy_f_w3, hy_f_b3, hy_f_wout, hy_f_freq, hy_skip, hy_w_out, hy_b_out, at_w_qkv, at_q_g,
           at_k_g, at_sink, at_w_o, rel_bias, ffn_w_gate_up, ffn_w_down):
    per_layer = lambda w: [w[i].astype(BF16) for i in range(w.shape[0])]
    p = dict(
        norm_mix_g=norm_mix_g.astype(F32), norm_ffn_g=norm_ffn_g.astype(F32),
        hy_w_in_t=[hy_w_in[j].T.astype(BF16) for j in range(hy_w_in.shape[0])], hy_b_in=hy_b_in,
        hy_conv_w=hy_conv_w, hy_conv_b=hy_conv_b,
        hy_f_w1=hy_f_w1, hy_f_b1=hy_f_b1, hy_f_w2=hy_f_w2, hy_f_b2=hy_f_b2, hy_f_w3=hy_f_w3, hy_f_b3=hy_f_b3,
        hy_f_wout=hy_f_wout, hy_f_freq=hy_f_freq, hy_skip=hy_skip,
        hy_w_out=per_layer(hy_w_out), hy_b_out=hy_b_out,
        at_w_qkv=per_layer(at_w_qkv), at_q_g=at_q_g, at_k_g=at_k_g, at_sink=at_sink,
        at_w_o=per_layer(at_w_o), rel_bias=rel_bias,
        ffn_w_gate_up=per_layer(ffn_w_gate_up), ffn_w_down=per_layer(ffn_w_down),
    )
    y_prompt = _trunk(x_prompt, p)
    y_sample = _trunk(x_sample, p)
    return (y_prompt, y_sample)
```

```python
import functools
import math

import jax
import jax.numpy as jnp
import numpy as np
from jax import lax
from jax.experimental import pallas as pl
from jax.experimental.pallas import tpu as pltpu

F32 = jnp.float32
BF16 = jnp.bfloat16

D_MODEL = 2048
DEPTH = 4
N_MIXERS = 2
HY_ORDER = 2
HY_EMB = 33
HY_BANDS = (HY_EMB - 1) // 2
HY_FILTER_W = 64
HY_FAST_PCT = 0.3
HY_SLOW_PCT = 1.5
HY_TARGET = 1e-2
N_HEADS = 16
HEAD_DIM = 128
N_KV = 4
GQA_G = N_HEADS // N_KV
WINDOW = 128
BLOCK = 128
N_BUCKETS = 32
MAX_DIST = 128
D_FF = -(-(8 * D_MODEL) // (3 * 256)) * 256
EPS = 1e-6
NEG = -1e30

V7X_LANES = 128
V7X_SUBLANES = 8
VMEM_LIMIT = 56 * 1024 * 1024

FFT_P = V7X_LANES
CH_GROUP = 4 * V7X_SUBLANES

ROW_BLOCK = 1024
FFN_CHUNK = 512
IN_PROJ_CHANNELS = 1024
OUT_PROJ_T = 512
CONV_CHANNELS = 32


def _cparams(*sem):
    return pltpu.CompilerParams(dimension_semantics=sem, vmem_limit_bytes=VMEM_LIMIT)


def _rms_bf16(x, g):
    ms = jnp.mean(x * x, axis=-1, keepdims=True)
    return (x * lax.rsqrt(ms + EPS) * g).astype(BF16)


def _qkv_proj_body(x_ref, g_ref, w_ref, qg_ref, kg_ref, o_ref):
    hn = _rms_bf16(x_ref[...], g_ref[...])
    nq, nk = N_HEADS * HEAD_DIM, N_KV * HEAD_DIM
    pair = 2 * HEAD_DIM
    for c0 in range(0, nq + nk, pair):
        acc = jnp.dot(hn, w_ref[:, c0:c0 + pair], preferred_element_type=F32)
        gain = qg_ref[...] if c0 < nq else kg_ref[...]
        for c in (0, HEAD_DIM):
            t = acc[:, c:c + HEAD_DIM]
            ms = jnp.mean(t * t, axis=-1, keepdims=True)
            o_ref[:, c0 + c:c0 + c + HEAD_DIM] = (t * lax.rsqrt(ms + EPS) * gain).astype(o_ref.dtype)
    o_ref[:, nq + nk:] = jnp.dot(hn, w_ref[:, nq + nk:], preferred_element_type=F32).astype(o_ref.dtype)


def qkv_projection(x, g, w, q_g, k_g, *, bm):
    T, K = x.shape
    N = w.shape[1]
    return pl.pallas_call(
        _qkv_proj_body,
        grid=(T // bm,),
        in_specs=[
            pl.BlockSpec((bm, K), lambda i: (i, 0)),
            pl.BlockSpec((1, K), lambda i: (0, 0)),
            pl.BlockSpec((K, N), lambda i: (0, 0), pipeline_mode=pl.Buffered(1)),
            pl.BlockSpec((1, HEAD_DIM), lambda i: (0, 0)),
            pl.BlockSpec((1, HEAD_DIM), lambda i: (0, 0)),
        ],
        out_specs=pl.BlockSpec((bm, N), lambda i: (i, 0)),
        out_shape=jax.ShapeDtypeStruct((T, N), BF16),
        compiler_params=_cparams("parallel"),
        name="qkv_projection",
    )(x, g.reshape(1, K), w, q_g.astype(F32).reshape(1, HEAD_DIM), k_g.astype(F32).reshape(1, HEAD_DIM))


IN_BT = V7X_SUBLANES * V7X_LANES
IN_ROW_SPLIT = 4


def _hyena_edge_body(x_ref, g_ref, w_ref, b_ref, o_ref):
    hn = _rms_bf16(x_ref[0], g_ref[...])
    o_ref[0] = lax.dot_general(hn, w_ref[...], (((1,), (1,)), ((), ())), preferred_element_type=F32) + b_ref[...]


def hyena_edge_projection(x_edge, g, w_t, bias, *, bc):
    B, n, K = x_edge.shape
    C = w_t.shape[0]
    return pl.pallas_call(
        _hyena_edge_body,
        grid=(B, C // bc),
        in_specs=[
            pl.BlockSpec((1, n, K), lambda b, c: (b, 0, 0)),
            pl.BlockSpec((1, K), lambda b, c: (0, 0)),
            pl.BlockSpec((bc, K), lambda b, c: (c, 0)),
            pl.BlockSpec((1, bc), lambda b, c: (0, c)),
        ],
        out_specs=pl.BlockSpec((1, n, bc), lambda b, c: (b, 0, c)),
        out_shape=jax.ShapeDtypeStruct((B, n, C), F32),
        compiler_params=_cparams("parallel", "parallel"),
        name="hyena_edge_projection",
    )(x_edge, g.reshape(1, K), w_t, bias.astype(F32).reshape(1, C))


def _hyena_in_body(x_ref, g_ref, w_ref, b_ref, k0_ref, k1_ref, k2_ref, kb_ref, edge_ref, o_ref, hn_ref):
    t = pl.program_id(1)
    nt = pl.num_programs(1)

    @pl.when(pl.program_id(2) == 0)
    def _():
        hn_ref[...] = _rms_bf16(x_ref[0], g_ref[...])

    P = V7X_LANES
    bt = hn_ref.shape[0]
    n_chunks = bt // P
    rows = w_ref.shape[0] // IN_ROW_SPLIT
    for r in range(IN_ROW_SPLIT):
        rs = slice(r * rows, (r + 1) * rows)
        u = lax.dot_general(w_ref[rs, :], hn_ref[...], (((1,), (1,)), ((), ())), preferred_element_type=F32)
        def border(k):
            parts = [jnp.transpose(jnp.broadcast_to(edge_ref[0, k, 0:1, c0:c0 + P], (P, P)))
                     for c0 in range(r * rows, (r + 1) * rows, P)]
            return jnp.concatenate(parts, axis=0)

        left = jnp.where(t == 0, 0.0, border(t))
        right = jnp.where(t == nt - 1, 0.0, border(nt + t))
        lane = lax.broadcasted_iota(jnp.int32, (rows, P), 1)
        bias, k0, k1, k2, kb = (ref[rs, :] for ref in (b_ref, k0_ref, k1_ref, k2_ref, kb_ref))
        chunks = [u[:, j * P:(j + 1) * P] + bias for j in range(n_chunks)]
        fwd = [pltpu.roll(c, 1, axis=1) for c in chunks]
        bwd = [pltpu.roll(c, P - 1, axis=1) for c in chunks]
        for j in range(n_chunks):
            prev = jnp.where(lane == 0, fwd[j - 1] if j > 0 else left, fwd[j])
            nxt = jnp.where(lane == P - 1, bwd[j + 1] if j + 1 < n_chunks else right, bwd[j])
            o_ref[pl.ds(r * rows * n_chunks + j, rows, stride=n_chunks), :] = (
                k0 * prev + k1 * chunks[j] + k2 * nxt + kb)


def hyena_in_projection(x, g, w_t, bias, conv_w, conv_b, *, bc):
    B, L, K = x.shape
    C = w_t.shape[0]
    bt = IN_BT
    nt = L // bt
    xb = x.reshape(B, nt, bt, K)
    x_edge = jnp.concatenate([jnp.roll(xb[:, :, bt - 1], 1, axis=1), jnp.roll(xb[:, :, 0], -1, axis=1)], axis=1)
    edge = hyena_edge_projection(x_edge, g, w_t, bias, bc=bc)
    edge = jnp.broadcast_to(edge[:, :, None, :], (B, 2 * nt, V7X_SUBLANES, C))
    col = lambda v: jnp.broadcast_to(v.astype(F32).reshape(C, 1), (C, V7X_LANES))
    cspec = pl.BlockSpec((bc, V7X_LANES), lambda b, t, c: (c, 0))
    return pl.pallas_call(
        _hyena_in_body,
        grid=(B, nt, C // bc),
        in_specs=[
            pl.BlockSpec((1, bt, K), lambda b, t, c: (b, t, 0)),
            pl.BlockSpec((1, K), lambda b, t, c: (0, 0)),
            pl.BlockSpec((bc, K), lambda b, t, c: (c, 0)),
            cspec, cspec, cspec, cspec, cspec,
            pl.BlockSpec((1, 2 * nt, V7X_SUBLANES, bc), lambda b, t, c: (b, 0, 0, c)),
        ],
        out_specs=pl.BlockSpec((None, None, bc * V7X_SUBLANES, V7X_LANES), lambda b, t, c: (b, t, c, 0)),
        out_shape=jax.ShapeDtypeStruct((B, nt, C * V7X_SUBLANES, V7X_LANES), F32),
        scratch_shapes=[pltpu.VMEM((bt, K), BF16)],
        compiler_params=_cparams("parallel", "parallel", "arbitrary"),
        name="hyena_in_projection",
    )(x, g.reshape(1, K), w_t, col(bias), col(conv_w[0]), col(conv_w[1]), col(conv_w[2]), col(conv_b), edge)


def _res_mm_body(a_ref, w_ref, x_ref, o_ref):
    o_ref[...] = x_ref[...] + jnp.dot(a_ref[...], w_ref[...], preferred_element_type=F32)


def residual_matmul(x, a, w, *, bm):
    T, K = a.shape
    N = w.shape[1]
    return pl.pallas_call(
        _res_mm_body,
        grid=(T // bm,),
        in_specs=[
            pl.BlockSpec((bm, K), lambda i: (i, 0)),
            pl.BlockSpec((K, N), lambda i: (0, 0), pipeline_mode=pl.Buffered(1)),
            pl.BlockSpec((bm, N), lambda i: (i, 0)),
        ],
        out_specs=pl.BlockSpec((bm, N), lambda i: (i, 0)),
        out_shape=jax.ShapeDtypeStruct((T, N), F32),
        compiler_params=_cparams("parallel"),
        name="residual_matmul",
    )(a, w, x)


def _res_mm_t_body(z_ref, w_ref, b_ref, x_ref, o_ref):
    y = lax.dot_general(z_ref[0], w_ref[...], (((0,), (0,)), ((), ())), preferred_element_type=F32)
    o_ref[0] = x_ref[0] + y + b_ref[...]


def residual_matmul_t(x, z_t, w, bias, *, bt):
    B, K, L = z_t.shape
    N = w.shape[1]
    return pl.pallas_call(
        _res_mm_t_body,
        grid=(B, L // bt),
        in_specs=[
            pl.BlockSpec((1, K, bt), lambda b, t: (b, 0, t)),
            pl.BlockSpec((K, N), lambda b, t: (0, 0), pipeline_mode=pl.Buffered(1)),
            pl.BlockSpec((1, N), lambda b, t: (0, 0)),
            pl.BlockSpec((1, bt, N), lambda b, t: (b, t, 0)),
        ],
        out_specs=pl.BlockSpec((1, bt, N), lambda b, t: (b, t, 0)),
        out_shape=jax.ShapeDtypeStruct((B, L, N), F32),
        compiler_params=_cparams("parallel", "parallel"),
        name="residual_matmul_t",
    )(z_t, w, bias.reshape(1, N), x)


def _ffn_body(x_ref, g_ref, wg_ref, wu_ref, wd_ref, o_ref, hn_ref):
    @pl.when(pl.program_id(1) == 0)
    def _():
        x = x_ref[...]
        hn_ref[...] = _rms_bf16(x, g_ref[...])
        o_ref[...] = x

    h = hn_ref[...]
    gate = jnp.dot(h, wg_ref[...], preferred_element_type=F32)
    up = jnp.dot(h, wu_ref[...], preferred_element_type=F32)
    act = (gate * jax.nn.sigmoid(gate) * up).astype(BF16)
    o_ref[...] += jnp.dot(act, wd_ref[...], preferred_element_type=F32)


def ffn_block(x, g, w_gate_up, w_down, *, bm, bf):
    T, K = x.shape
    nf = D_FF // bf
    return pl.pallas_call(
        _ffn_body,
        grid=(T // bm, nf),
        in_specs=[
            pl.BlockSpec((bm, K), lambda i, f: (i, 0)),
            pl.BlockSpec((1, K), lambda i, f: (0, 0)),
            pl.BlockSpec((K, bf), lambda i, f: (0, f)),
            pl.BlockSpec((K, bf), lambda i, f: (0, f + nf)),
            pl.BlockSpec((bf, K), lambda i, f: (f, 0)),
        ],
        out_specs=pl.BlockSpec((bm, K), lambda i, f: (i, 0)),
        out_shape=jax.ShapeDtypeStruct((T, K), F32),
        scratch_shapes=[pltpu.VMEM((bm, K), BF16)],
        compiler_params=_cparams("parallel", "arbitrary"),
        name="ffn_block",
    )(x, g.reshape(1, K), w_gate_up, w_gate_up, w_down)


ATT_TQ = 4 * BLOCK
KV_W = N_KV * HEAD_DIM


def _band_structure():
    qi = np.arange(BLOCK)[:, None]
    ki = np.arange(3 * BLOCK)[None, :]
    rel = ki - BLOCK - qi
    nb = N_BUCKETS // 2
    max_exact = nb // 2
    n = np.abs(rel)
    large = max_exact + (np.log(np.maximum(n, 1) / max_exact) / math.log(MAX_DIST / max_exact)
                         * (nb - max_exact)).astype(np.int32)
    large = np.minimum(large, nb - 1)
    buckets = (rel > 0).astype(np.int32) * nb + np.where(n < max_exact, n, large).astype(np.int32)
    band = n <= WINDOW
    return buckets, band


def _attn_body(q_ref, kp_ref, kc_ref, kn_ref, vp_ref, vc_ref, vn_ref, bias_ref, sink_ref,
               o_ref, kbuf, vbuf):
    i = pl.program_id(1)
    last = pl.num_programs(1) - 1

    kbuf[0:BLOCK] = kp_ref[0]
    kbuf[BLOCK:BLOCK + ATT_TQ] = kc_ref[0]
    kbuf[BLOCK + ATT_TQ:] = kn_ref[0]
    ones = jnp.ones((ATT_TQ + 2 * BLOCK, HEAD_DIM), BF16)
    for g in range(N_KV):
        src = slice(g * HEAD_DIM, (g + 1) * HEAD_DIM)
        dst = slice(2 * g * HEAD_DIM, (2 * g + 1) * HEAD_DIM)
        vbuf[0:BLOCK, dst] = vp_ref[0, :, src]
        vbuf[BLOCK:BLOCK + ATT_TQ, dst] = vc_ref[0, :, src]
        vbuf[BLOCK + ATT_TQ:, dst] = vn_ref[0, :, src]
        vbuf[:, (2 * g + 1) * HEAD_DIM:(2 * g + 2) * HEAD_DIM] = ones

    lane = lax.broadcasted_iota(jnp.int32, (1, 3 * BLOCK), 1)
    first_edge = jnp.where((lane < BLOCK) & (i == 0), NEG, 0.0).astype(F32)
    last_edge = jnp.where((lane >= 2 * BLOCK) & (i == last), NEG, 0.0).astype(F32)
    exp2_scale = HEAD_DIM ** -0.5 * math.log2(math.e)

    n_sub = ATT_TQ // BLOCK

    def scores(j, g):
        r0 = j * BLOCK
        qs = jnp.concatenate(
            [q_ref[0, r0:r0 + BLOCK, (GQA_G * g + h) * HEAD_DIM:(GQA_G * g + h + 1) * HEAD_DIM]
             for h in range(GQA_G)], axis=0)
        kw = kbuf[r0:r0 + 3 * BLOCK, g * HEAD_DIM:(g + 1) * HEAD_DIM]
        s = lax.dot_general(qs, kw, (((1,), (1,)), ((), ())), preferred_element_type=F32)
        s = s + bias_ref[g]
        if j == 0:
            s = s + first_edge
        if j == n_sub - 1:
            s = s + last_edge
        return s

    def softmax_numerators(g, s):
        sk = sink_ref[g][:, 0:1]
        m = jnp.maximum(jnp.max(s, axis=-1, keepdims=True), sk)
        return jnp.exp2((s - m) * exp2_scale).astype(BF16), jnp.exp2((sk - m) * exp2_scale)

    def outputs(j, g, p, sink_p):
        r0 = j * BLOCK
        vw = vbuf[r0:r0 + 3 * BLOCK, 2 * g * HEAD_DIM:(2 * g + 2) * HEAD_DIM]
        pv = jnp.dot(p, vw, preferred_element_type=F32)
        o = pv[:, :HEAD_DIM] / (pv[:, HEAD_DIM:] + sink_p)
        for h in range(GQA_G):
            c0 = (GQA_G * g + h) * HEAD_DIM
            o_ref[0, r0:r0 + BLOCK, c0:c0 + HEAD_DIM] = o[h * BLOCK:(h + 1) * BLOCK].astype(o_ref.dtype)

    chains = [(j, g) for j in range(n_sub) for g in range(N_KV)]
    s_q, p_q = {}, {}
    for step in range(len(chains) + 2):
        if step >= 2:
            outputs(*chains[step - 2], *p_q.pop(step - 2))
        if 1 <= step <= len(chains):
            p_q[step - 1] = softmax_numerators(chains[step - 1][1], s_q.pop(step - 1))
        if step < len(chains):
            s_q[step] = scores(*chains[step])


def window_attention(qkv, sink, rel_bias):
    B, L, _ = qkv.shape
    nq = N_HEADS * HEAD_DIM
    sub = ATT_TQ // BLOCK
    nblk = L // BLOCK
    kcol = nq // KV_W
    vcol = kcol + 1

    buckets, band = _band_structure()
    onehot = (jnp.asarray(buckets)[:, :, None] == jnp.arange(N_BUCKETS)).astype(F32)
    bias = jnp.einsum("qkn,nh->qkh", onehot, rel_bias.astype(F32), precision=lax.Precision.HIGHEST)
    inv_scale = HEAD_DIM ** 0.5
    bias = jnp.where(band[:, :, None], bias * inv_scale, NEG)
    bias = jnp.transpose(bias, (2, 0, 1)).reshape(N_KV, GQA_G * BLOCK, 3 * BLOCK)
    sink_rows = jnp.broadcast_to((sink.astype(F32) * inv_scale).reshape(N_KV, GQA_G, 1, 1),
                                 (N_KV, GQA_G, BLOCK, V7X_LANES)).reshape(N_KV, GQA_G * BLOCK, V7X_LANES)

    return pl.pallas_call(
        _attn_body,
        grid=(B, L // ATT_TQ),
        in_specs=[
            pl.BlockSpec((1, ATT_TQ, nq), lambda b, i: (b, i, 0)),
            pl.BlockSpec((1, BLOCK, KV_W), lambda b, i: (b, jnp.maximum(sub * i - 1, 0), kcol)),
            pl.BlockSpec((1, ATT_TQ, KV_W), lambda b, i: (b, i, kcol)),
            pl.BlockSpec((1, BLOCK, KV_W), lambda b, i: (b, jnp.minimum(sub * i + sub, nblk - 1), kcol)),
            pl.BlockSpec((1, BLOCK, KV_W), lambda b, i: (b, jnp.maximum(sub * i - 1, 0), vcol)),
            pl.BlockSpec((1, ATT_TQ, KV_W), lambda b, i: (b, i, vcol)),
            pl.BlockSpec((1, BLOCK, KV_W), lambda b, i: (b, jnp.minimum(sub * i + sub, nblk - 1), vcol)),
            pl.BlockSpec((N_KV, GQA_G * BLOCK, 3 * BLOCK), lambda b, i: (0, 0, 0)),
            pl.BlockSpec((N_KV, GQA_G * BLOCK, V7X_LANES), lambda b, i: (0, 0, 0)),
        ],
        out_specs=pl.BlockSpec((1, ATT_TQ, nq), lambda b, i: (b, i, 0)),
        out_shape=jax.ShapeDtypeStruct((B, L, nq), BF16),
        scratch_shapes=[pltpu.VMEM((ATT_TQ + 2 * BLOCK, KV_W), BF16),
                        pltpu.VMEM((ATT_TQ + 2 * BLOCK, 2 * KV_W), BF16)],
        compiler_params=_cparams("parallel", "parallel"),
        name="window_attention",
    )(qkv, qkv, qkv, qkv, qkv, qkv, qkv, bias, sink_rows)


FEAT_PAD = V7X_LANES


def _filter_mlp_body(ff_ref, fb_ref, w1_ref, b1_ref, w2_ref, b2_ref, w3_ref, b3_ref, fr_ref, af_ref, ab_ref):
    fr = fr_ref[...]
    for feat_ref, a_ref in ((ff_ref, af_ref), (fb_ref, ab_ref)):
        a = jnp.sin(fr * (jnp.dot(w1_ref[...], feat_ref[...], preferred_element_type=F32) + b1_ref[...]))
        a = jnp.sin(fr * (jnp.dot(w2_ref[...], a.astype(BF16), preferred_element_type=F32) + b2_ref[...]))
        a = jnp.sin(fr * (jnp.dot(w3_ref[...], a.astype(BF16), preferred_element_type=F32) + b3_ref[...]))
        a_ref[...] = a.astype(BF16)


def hyena_filter_mlp(L, f_w1, f_b1, f_w2, f_b2, f_w3, f_b3, f_freq, *, bt=2048):
    W = HY_FILTER_W
    f = jnp.linspace(1e-4, HY_BANDS - 1, HY_BANDS, dtype=F32)

    def features(pos):
        t = pos / (L - 1)
        ang = (2.0 * math.pi * pos / L)[None, :] * f[:, None]
        rows = jnp.concatenate([t[None, :], jnp.cos(ang), -jnp.sin(ang), jnp.zeros((FEAT_PAD - HY_EMB, L), F32)])
        return rows.astype(BF16)

    pos = jnp.arange(L, dtype=F32)
    w1_t = jnp.concatenate([f_w1.astype(F32).T, jnp.zeros((W, FEAT_PAD - HY_EMB), F32)], axis=1)
    col = lambda v: v.astype(F32).reshape(W, 1)
    const = lambda r, c: pl.BlockSpec((r, c), lambda i: (0, 0))
    fspec = pl.BlockSpec((FEAT_PAD, bt), lambda i: (0, i))
    ospec = pl.BlockSpec((W, bt), lambda i: (0, i))
    return pl.pallas_call(
        _filter_mlp_body,
        grid=(L // bt,),
        in_specs=[fspec, fspec, const(W, FEAT_PAD), const(W, 1), const(W, W), const(W, 1), const(W, W), const(W, 1),
                  const(W, 1)],
        out_specs=[ospec, ospec],
        out_shape=[jax.ShapeDtypeStruct((W, L), BF16), jax.ShapeDtypeStruct((W, L), BF16)],
        compiler_params=_cparams("parallel"),
        name="hyena_filter_mlp",
    )(features(pos), features(L - pos), w1_t.astype(BF16), col(f_b1), f_w2.T.astype(BF16), col(f_b2),
      f_w3.T.astype(BF16), col(f_b3), col(f_freq))


def _dft_tables(L, paired):
    N = 2 * L
    P = FFT_P
    Q = N // P
    S = L // P
    b = np.arange(Q, dtype=np.float64)
    fq = np.exp(-2j * np.pi * np.outer(b, np.arange(S)) / Q)
    fp =np.exp(-2j * np.pi * np.outer(np.arange(P), np.arange(P)) / P)
    tw = np.exp(-2j * np.pi * np.outer(b, np.arange(P)) / N)
    ci = np.conj(fq).T / N

    def stack(c):
        return np.block([[c.real, -c.imag], [c.imag, c.real]])

    if paired:
        g1 = stack(fq)
        g4 = stack(ci)
    else:
        g1 = np.concatenate([fq.real, fq.imag], axis=0)
        g4 = np.concatenate([ci.real, -ci.imag], axis=1)
    fq_full = np.exp(-2j * np.pi * np.outer(b, np.arange(Q)) / Q)
    g1_full = np.concatenate([fq_full.real, fq_full.imag], axis=0)
    g2 =np.block([[fp.real, fp.imag], [-fp.imag, fp.real]])
    g2c = np.block([[fp.real, -fp.imag], [fp.imag, fp.real]])
    twr = np.tile(tw.real, (1, 2))
    twi = np.tile(tw.imag, (1, 2))
    f32 = lambda a: np.ascontiguousarray(a, dtype=np.float32)
    return dict(g1=f32(g1), g4=f32(g4), g1_full=f32(g1_full), g2=f32(g2), g2c=f32(g2c),
                twr=f32(twr), twi=f32(twi), Q=Q, S=S, N=N)


def _to_tiles(x, n_tiles):
    chunks = [x[:, s * FFT_P:(s + 1) * FFT_P] for s in range(n_tiles)]
    return jnp.swapaxes(jnp.stack(chunks, axis=0), 0, 1)


def _from_tiles(x):
    y = jnp.swapaxes(x, 0, 1)
    return jnp.concatenate([y[s] for s in range(y.shape[0])], axis=1)


def _fwd_fft(re_tiles, im_tiles, g1, twr, twi, g2):
    G = len(re_tiles)
    Q = twr.shape[0]
    P = FFT_P
    rows = []
    for c in range(0, G, 2):
        top = jnp.concatenate([re_tiles[c], re_tiles[c + 1]], axis=1)
        if im_tiles is None:
            rhs = top
        else:
            rhs = jnp.concatenate([top, jnp.concatenate([im_tiles[c], im_tiles[c + 1]], axis=1)], axis=0)
        y = jnp.dot(g1, rhs.astype(BF16), preferred_element_type=F32)
        y = y.astype(twr.dtype)
        yr, yi = y[:Q], y[Q:]
        zr = yr * twr - yi * twi
        zi = yr * twi + yi * twr
        rows.append(jnp.concatenate([zr[:, :P], zi[:, :P]], axis=1))
        rows.append(jnp.concatenate([zr[:, P:], zi[:, P:]], axis=1))
    lhs = jnp.concatenate(rows, axis=0).astype(BF16)
    return jnp.dot(lhs, g2, preferred_element_type=F32)


def _inv_fft(spec, g2c, twr, twi, g4, want_imag):
    Q = twr.shape[0]
    P = FFT_P
    G = spec.shape[0] // Q
    S = g4.shape[0] // 2 if want_imag else g4.shape[0]
    y = jnp.dot(spec.astype(BF16), g2c, preferred_element_type=F32)
    y = y.astype(twr.dtype)
    out_re, out_im = [], []
    for c in range(0, G, 2):
        ya = y[c * Q:(c + 1) * Q]
        yb = y[(c + 1) * Q:(c + 2) * Q]
        yr = jnp.concatenate([ya[:, :P], yb[:, :P]], axis=1)
        yi = jnp.concatenate([ya[:, P:], yb[:, P:]], axis=1)
        zr = yr * twr + yi * twi
        zi = yi * twr - yr * twi
        rhs = jnp.concatenate([zr, zi], axis=0).astype(BF16)
        o = jnp.dot(g4, rhs, preferred_element_type=F32)
        out_re += [o[:S, :P], o[:S, P:]]
        if want_imag:
            out_im += [o[S:, :P], o[S:, P:]]
    return out_re, out_im


def _hyena_body(v_ref, x1_ref, x2_ref, af_ref, ab_ref, wof_ref, wob_ref, delta_ref, tf_ref, tb_ref, skip_ref,
                g1_ref, g1f_ref, twr_ref, twi_ref, g2_ref, g2c_ref, g4_ref, o_ref, taps_ref, kf_ref,
                *, S, Q, paired):
    g1, twr, twi = g1_ref[...], twr_ref[...], twi_ref[...]
    g2, g2c, g4 = g2_ref[...], g2c_ref[...], g4_ref[...]
    nb = 2 if paired else 1
    cb = kf_ref.shape[1]
    G = min(cb, CH_GROUP)
    n_groups = cb // G
    P = FFT_P

    @pl.when(pl.program_id(1) == 0)
    def _():
        g1f = g1f_ref[...]
        L = S * P
        delta = delta_ref[...]
        halves = ((wof_ref, af_ref, jnp.exp(-(delta * tf_ref[...]))),
                  (wob_ref, ab_ref, jnp.where(lax.broadcasted_iota(jnp.int32, (cb, L), 1) == 0, 0.0,
                                              jnp.exp(-(delta * tb_ref[...])))))
        for half, (wo_ref, a_ref, decay) in enumerate(halves):
            wo = wo_ref[...].reshape(HY_ORDER * cb, HY_FILTER_W)
            h = jnp.dot(wo, a_ref[...], preferred_element_type=F32)
            for o in range(HY_ORDER):
                taps_ref[o, :, half * L:(half + 1) * L] = h[o * cb:(o + 1) * cb] * decay

        def filter_group(gi, carry):
            c0 = pl.multiple_of(gi * G, G)
            for o in range(HY_ORDER):
                k = taps_ref[o, pl.ds(c0, G), :]
                norm = jnp.sum(jnp.abs(k), axis=-1, keepdims=True)
                tiles = _to_tiles(k, Q)
                spec = _fwd_fft([tiles[c] for c in range(G)], None, g1f, twr, twi, g2)
                kf = spec.reshape(G, Q, 2 * P) * (1.0 / norm)[:, :, None]
                kf_ref[o, pl.ds(c0, G)] = kf.astype(kf_ref.dtype)
            return carry

        lax.fori_loop(0, n_groups, filter_group, 0)

    def group(gi, carry):
        c0 = pl.multiple_of(gi * G, G)
        r0 = pl.multiple_of(gi * (G * V7X_SUBLANES), G * V7X_SUBLANES)

        def tiles(ref, b):
            blk = ref[b, :, pl.ds(r0, G * V7X_SUBLANES), :]
            return [blk[:, c * V7X_SUBLANES:(c + 1) * V7X_SUBLANES, :].reshape(S, P) for c in range(G)]

        z = [tiles(v_ref, b) for b in range(nb)]
        gates = [[tiles(x1_ref, b) for b in range(nb)], [tiles(x2_ref, b) for b in range(nb)]]
        for o in range(HY_ORDER):
            re = z[0]
            im = z[1] if paired else None
            spec = _fwd_fft(re, im, g1, twr, twi, g2)
            kf = kf_ref[o, pl.ds(c0, G)].reshape(G * Q, 2 * P)
            spec = spec.astype(kf.dtype)
            xr, xi = spec[:, :P], spec[:, P:]
            kr, ki = kf[:, :P], kf[:, P:]
            prod = jnp.concatenate([xr * kr - xi * ki, xr * ki + xi * kr], axis=1)
            out_re, out_im = _inv_fft(prod, g2c, twr, twi, g4, paired)
            skip = skip_ref[o, pl.ds(c0, G)]
            conv = [out_re] + ([out_im] if paired else [])
            z = [[gates[o][b][c] * (conv[b][c] + skip[c] * z[b][c]) for c in range(G)] for b in range(nb)]
        for b in range(nb):
            o_ref[b, pl.ds(c0, G), :] = _from_tiles(jnp.stack(z[b], axis=0)).astype(o_ref.dtype)
        return carry

    lax.fori_loop(0, n_groups, group, 0)


def hyena_operator(u_tiles, a, a_b, f_wout, skip, tabs, *, cb):
    B, nt, rows, P = u_tiles.shape
    D = rows // (3 * V7X_SUBLANES)
    L = nt * IN_BT
    S = L // P
    Q = tabs["Q"]
    W = HY_FILTER_W
    paired = B % 2 == 0
    nb = 2 if paired else 1
    wo = f_wout.T.reshape(HY_ORDER, 2, D, W).astype(BF16)
    wo_f, wo_b = wo[:, 0], wo[:, 1]
    deltas = np.abs(np.linspace(math.log(HY_TARGET) / HY_SLOW_PCT,
                                math.log(HY_TARGET) / HY_FAST_PCT, D)).astype(np.float32).reshape(D, 1)
    t_f = jnp.linspace(0.0, 1.0, L, dtype=F32).reshape(1, L)
    t_b = ((L - jnp.arange(L, dtype=F32)) / (L - 1)).reshape(1, L)
    skip_rows = jnp.broadcast_to(skip.astype(F32)[:, :, None, None], (HY_ORDER, D, 1, P))

    bf = lambda name: jnp.asarray(tabs[name]).astype(BF16)
    g1, g1f, g2, g2c, g4 = bf("g1"), bf("g1_full"), bf("g2"), bf("g2c"), bf("g4")
    twr, twi = bf("twr"), bf("twi")
    const = lambda arr: pl.BlockSpec(arr.shape, lambda c, p: (0,) * arr.ndim, pipeline_mode=pl.Buffered(1))
    ncb = D // cb
    return pl.pallas_call(
        functools.partial(_hyena_body, S=S, Q=Q, paired=paired),
        grid=(ncb, B // nb),
        in_specs=[
            pl.BlockSpec((nb, nt, cb * V7X_SUBLANES, P), lambda c, p: (p, 0, c, 0)),
            pl.BlockSpec((nb, nt, cb * V7X_SUBLANES, P), lambda c, p: (p, 0, c + ncb, 0)),
            pl.BlockSpec((nb, nt, cb * V7X_SUBLANES, P), lambda c, p: (p, 0, c + 2 * ncb, 0)),
            const(a), const(a_b),
            pl.BlockSpec((HY_ORDER, cb, W), lambda c, p: (0, c, 0)),
            pl.BlockSpec((HY_ORDER, cb, W), lambda c, p: (0, c, 0)),
            pl.BlockSpec((cb, 1), lambda c, p: (c, 0)),
            const(t_f), const(t_b),
            pl.BlockSpec((HY_ORDER, cb, 1, P), lambda c, p: (0, c, 0, 0)),
            const(g1), const(g1f), const(twr), const(twi), const(g2), const(g2c), const(g4),
        ],
        out_specs=pl.BlockSpec((nb, cb, L), lambda c, p: (p, c, 0)),
        out_shape=jax.ShapeDtypeStruct((B, D, L), BF16),
        scratch_shapes=[pltpu.VMEM((HY_ORDER, cb, 2 * L), F32), pltpu.VMEM((HY_ORDER, cb, Q, 2 * P), BF16)],
        compiler_params=_cparams("parallel", "arbitrary"),
        name="hyena_operator",
    )(u_tiles, u_tiles, u_tiles, a, a_b, wo_f, wo_b, jnp.asarray(deltas), t_f, t_b, skip_rows,
      g1, g1f, twr, twi, g2, g2c, g4)


def _trunk(x, p):
    B, L, D = x.shape
    tabs = _dft_tables(L, paired=(B % 2 == 0))
    for i in range(DEPTH):
        j = i // N_MIXERS
        if i % N_MIXERS == 0:
            a, a_b = hyena_filter_mlp(L, p["hy_f_w1"][j], p["hy_f_b1"][j], p["hy_f_w2"][j], p["hy_f_b2"][j],
                                      p["hy_f_w3"][j], p["hy_f_b3"][j], p["hy_f_freq"][j])
            u_tiles = hyena_in_projection(x, p["norm_mix_g"][i], p["hy_w_in_t"][j], p["hy_b_in"][j],
                                          p["hy_conv_w"][j], p["hy_conv_b"][j], bc=IN_PROJ_CHANNELS)
            conv_cb = 2 * CONV_CHANNELS if B % 2 == 0 else CONV_CHANNELS
            z_t = hyena_operator(u_tiles, a, a_b, p["hy_f_wout"][j], p["hy_skip"][j], tabs, cb=conv_cb)
            x = residual_matmul_t(x, z_t, p["hy_w_out"][j], p["hy_b_out"][j], bt=OUT_PROJ_T)
        else:
            x2 = x.reshape(B * L, D)
            qkv = qkv_projection(x2, p["norm_mix_g"][i], p["at_w_qkv"][j], p["at_q_g"][j], p["at_k_g"][j],
                                 bm=ROW_BLOCK)
            att = window_attention(qkv.reshape(B, L, -1), p["at_sink"][j], p["rel_bias"])
            x = residual_matmul(x2, att.reshape(B * L, -1), p["at_w_o"][j], bm=ROW_BLOCK).reshape(B, L, D)
        x = ffn_block(x.reshape(B * L, D), p["norm_ffn_g"][i], p["ffn_w_gate_up"][i], p["ffn_w_down"][i],
                      bm=ROW_BLOCK, bf=FFN_CHUNK).reshape(B, L, D)
    return x


def kernel(x_prompt, x_sample, norm_mix_g, norm_ffn_g, hy_w_in, hy_b_in, hy_conv_w, hy_conv_b, hy_f_w1, hy_f_b1,
           hy_f_w2, hy_f_b2, hy_f_w3, hy_f_b3, hy_f_wout, hy_f_freq, hy_skip, hy_w_out, hy_b_out, at_w_qkv, at_q_g,
           at_k_g, at_sink, at_w_o, rel_bias, ffn_w_gate_up, ffn_w_down):
    per_layer = lambda w: [w[i].astype(BF16) for i in range(w.shape[0])]
    p = dict(
        norm_mix_g=norm_mix_g.astype(F32), norm_ffn_g=norm_ffn_g.astype(F32),
        hy_w_in_t=[hy_w_in[j].T.astype(BF16) for j in range(hy_w_in.shape[0])], hy_b_in=hy_b_in,
        hy_conv_w=hy_conv_w, hy_conv_b=hy_conv_b,
        hy_f_w1=hy_f_w1, hy_f_b1=hy_f_b1, hy_f_w2=hy_f_w2, hy_f_b2=hy_f_b2, hy_f_w3=hy_f_w3, hy_f_b3=hy_f_b3,
        hy_f_wout=hy_f_wout, hy_f_freq=hy_f_freq, hy_skip=hy_skip,
        hy_w_out=per_layer(hy_w_out), hy_b_out=hy_b_out,
        at_w_qkv=per_layer(at_w_qkv), at_q_g=at_q_g, at_k_g=at_k_g, at_sink=at_sink,
        at_w_o=per_layer(at_w_o), rel_bias=rel_bias,
        ffn_w_gate_up=per_layer(ffn_w_gate_up), ffn_w_down=per_layer(ffn_w_down),
    )
    y_prompt = _trunk(x_prompt, p)
    y_sample = _trunk(x_sample, p)
    return (y_prompt, y_sample)
```
